```python
import functools
import jax, jax.numpy as jnp
from jax import lax
import numpy as np

D_MODEL = 1024
BATCH = 8
SEQ = 2048
DEPTH = 4
DEC_BATCH = 128
DEC_SEQ = 1
PAST_LEN = 8192
PAGE_SIZE = 128

N_HEADS = 16
N_KV_HEADS = 2
HEAD_DIM = 64
GROUP = N_HEADS // N_KV_HEADS
WINDOW = 128
ATTN_W = N_HEADS * HEAD_DIM
KV_W = N_KV_HEADS * HEAD_DIM
D_RNN = 1280
N_RNN_BLOCKS = 10
RNN_BLOCK = D_RNN // N_RNN_BLOCKS
CONV_W = 4
LRU_C = 8.0
N_EXPERTS = 32
TOP_K = 4
D_FF = 1024
SWIGLU_LIMIT = 7.0
SWIGLU_ALPHA = 1.702
MOE_BLOCK = 128
LN_EPS = 1e-5
DEEPNORM_ALPHA = (2 * DEPTH) ** 0.25
DEEPNORM_BETA = (8 * DEPTH) ** -0.25
PROJ_SPLITS = [ATTN_W, KV_W, KV_W, D_RNN, D_RNN, D_MODEL, D_MODEL]
PROJ_W = sum(PROJ_SPLITS)

kernel_name = "hybrid_swa_rglru_moe_deepnorm_step"


def layer_norm(x, g, b):
    xf = x.astype(jnp.float32)
    mu = jnp.mean(xf, axis=-1, keepdims=True)
    var = jnp.mean(jnp.square(xf - mu), axis=-1, keepdims=True)
    return ((xf - mu) * lax.rsqrt(var + LN_EPS) * g + b).astype(x.dtype)


def alibi_slopes():
    h = jnp.arange(1, N_HEADS + 1, dtype=jnp.float32)
    return (2.0 ** (-8.0 * h / N_HEADS)).reshape(N_KV_HEADS, GROUP)


def window_attention(q, k, v, q_pos, k_pos, sinks):
    scale = HEAD_DIM ** -0.5
    scores = jnp.einsum('bnqhgd,bnshd->bnhgqs', q, k,
                        preferred_element_type=jnp.float32) * scale
    dist = (q_pos[:, :, None] - k_pos[:, None, :])
    valid = (dist >= 0) & (dist < WINDOW) & (k_pos[:, None, :] >= 0)
    slopes = alibi_slopes()[None, None, :, :, None, None]
    logits = scores - slopes * dist.astype(jnp.float32)[None, :, None, None]
    logits = jnp.where(valid[None, :, None, None], logits, -jnp.inf)
    sink = sinks.astype(jnp.float32).reshape(N_KV_HEADS, GROUP)[None, None, :, :, None, None]
    m = jnp.maximum(jnp.max(logits, axis=-1, keepdims=True), sink)
    p = jnp.exp(logits - m)
    denom = jnp.sum(p, axis=-1, keepdims=True) + jnp.exp(sink - m)
    probs = (p / denom).astype(v.dtype)
    return jnp.einsum('bnhgqs,bnshd->bnqhgd', probs, v)


def prompt_attend(q, k, v, sinks, cache_len):
    B, S = q.shape[:2]
    nb = S // WINDOW
    qb = q.reshape(B, nb, WINDOW, N_KV_HEADS, GROUP, HEAD_DIM)

    def band(t):
        tb = t.reshape(B, nb, WINDOW, N_KV_HEADS, HEAD_DIM)
        prev = jnp.pad(tb[:, :-1], ((0, 0), (1, 0), (0, 0), (0, 0), (0, 0)))
        return jnp.concatenate([prev, tb], axis=2)

    pos = jnp.arange(S, dtype=jnp.int32).reshape(nb, WINDOW)
    kpos = jnp.concatenate([pos - WINDOW, pos], axis=1)
    o = window_attention(qb, band(k), band(v), pos, kpos, sinks)
    return o.reshape(B, S, ATTN_W), k[:, S - cache_len:], v[:, S - cache_len:]


def sample_attend(q, k, v, sinks, ck, cv):
    B, T = q.shape[:2]
    cw = ck.shape[1]
    keys = jnp.concatenate([ck.astype(k.dtype), k], axis=1)
    vals = jnp.concatenate([cv.astype(v.dtype), v], axis=1)
    qpos = PAST_LEN + jnp.arange(T, dtype=jnp.int32)
    kpos = PAST_LEN - cw + jnp.arange(cw + T, dtype=jnp.int32)
    o = window_attention(q.reshape(B, 1, T, N_KV_HEADS, GROUP, HEAD_DIM),
                         keys[:, None], vals[:, None], qpos[None], kpos[None], sinks)
    return o.reshape(B, T, ATTN_W), keys[:, -cw:], vals[:, -cw:]


def rglru_branch(xr, conv_buf, h0, conv_w, conv_b, wa, ba, wx, bx, lam):
    B, T, _ = xr.shape
    xp = jnp.concatenate([conv_buf.astype(xr.dtype), xr], axis=1)
    xc = conv_b + xp[:, 0:T] * conv_w[0]
    for j in range(1, CONV_W):
        xc = xc + xp[:, j:j + T] * conv_w[j]
    new_buf = xp[:, T:]
    xcb = xc.reshape(B, T, N_RNN_BLOCKS, RNN_BLOCK)
    r = jax.nn.sigmoid(jnp.einsum('btnc,ncd->btnd', xcb, wa).reshape(B, T, D_RNN) + ba)
    i = jax.nn.sigmoid(jnp.einsum('btnc,ncd->btnd', xcb, wx).reshape(B, T, D_RNN) + bx)
    log_a = (-LRU_C * jax.nn.softplus(-lam.astype(jnp.float32))) * r.astype(jnp.float32)
    a = jnp.exp(log_a)
    b = jnp.sqrt(-jnp.expm1(2.0 * log_a)) * (i * xc).astype(jnp.float32)

    def step(h, ab):
        a_t, b_t = ab
        h = a_t * h + b_t
        return h, h

    h_last, hs = lax.scan(step, h0.astype(jnp.float32), (a.swapaxes(0, 1), b.swapaxes(0, 1)))
    return hs.swapaxes(0, 1).astype(xr.dtype), new_buf, h_last


def temporal_block(x, p, attend, conv_buf, h0):
    B, T, _ = x.shape
    proj = x @ p['w_in']
    offs = list(np.cumsum(PROJ_SPLITS)[:-1])
    q, k, v, xr, yr, ga, gb = jnp.split(proj, offs, axis=-1)
    q = q.reshape(B, T, N_HEADS, HEAD_DIM)
    k = k.reshape(B, T, N_KV_HEADS, HEAD_DIM)
    v = v.reshape(B, T, N_KV_HEADS, HEAD_DIM)
    o_attn, nk, nv = attend(q, k, v, p['attn_sinks'])
    y_rnn, new_buf, h_last = rglru_branch(xr, conv_buf, h0, p['conv_w'], p['conv_b'],
                                          p['rg_wa'], p['rg_ba'], p['rg_wx'], p['rg_bx'],
                                          p['rg_lambda'])
    y_rnn = y_rnn * jax.nn.gelu(yr)
    merged = (jax.nn.sigmoid(ga) * (o_attn @ p['w_attn_out'])
              + jax.nn.sigmoid(gb) * (y_rnn @ p['w_rnn_out']))
    return merged @ p['w_out'], nk, nv, new_buf, h_last


def moe(x, w_router, b_router, w_up, b_up, w_down, b_down):
    B, T, D = x.shape
    xt = x.reshape(-1, D)
    n_tok = xt.shape[0]
    logits = (xt @ w_router + b_router).astype(jnp.float32)
    top_v, top_i = lax.top_k(logits, TOP_K)
    gate = jax.nn.softmax(top_v, axis=-1)
    n = n_tok * TOP_K
    flat_e = top_i.reshape(-1).astype(jnp.int32)
    flat_tok = jnp.arange(n, dtype=jnp.int32) // TOP_K
    flat_g = gate.reshape(-1)
    order = jnp.argsort(flat_e)
    se = flat_e[order]
    counts = jnp.zeros((N_EXPERTS,), jnp.int32).at[flat_e].add(1)
    padded = (counts + MOE_BLOCK - 1) // MOE_BLOCK * MOE_BLOCK
    pad_end = jnp.cumsum(padded)
    pad_start = pad_end - padded
    start = jnp.cumsum(counts) - counts
    dest = pad_start[se] + jnp.arange(n, dtype=jnp.int32) - start[se]
    nb = -(-(n + N_EXPERTS * (MOE_BLOCK - 1)) // MOE_BLOCK)
    rows = nb * MOE_BLOCK
    row_tok = jnp.full((rows,), n_tok, jnp.int32).at[dest].set(flat_tok[order])
    row_g = jnp.zeros((rows,), jnp.float32).at[dest].set(flat_g[order])
    block_e = jnp.minimum(
        jnp.searchsorted(pad_end, jnp.arange(nb, dtype=jnp.int32) * MOE_BLOCK, side='right'),
        N_EXPERTS - 1).astype(jnp.int32)
    x_pad = jnp.concatenate([xt, jnp.zeros((1, D), xt.dtype)], axis=0)
    xb = x_pad[row_tok].reshape(nb, MOE_BLOCK, D)

    def expert_block(args):
        xblk, e = args
        h = xblk @ w_up[e] + b_up[e]
        glu, lin = h[:, :D_FF], h[:, D_FF:]
        glu = jnp.minimum(glu, SWIGLU_LIMIT)
        lin = jnp.clip(lin, -SWIGLU_LIMIT, SWIGLU_LIMIT)
        act = glu * jax.nn.sigmoid(SWIGLU_ALPHA * glu) * (lin + 1.0)
        return act @ w_down[e] + b_down[e]

    yb = lax.map(expert_block, (xb, block_e)).reshape(rows, D)
    y = jnp.zeros((n_tok + 1, D), jnp.float32).at[row_tok].add(
        yb.astype(jnp.float32) * row_g[:, None])[:n_tok]
    return y.astype(x.dtype).reshape(B, T, D)


def setup_inputs(seed: int = 0) -> dict:
    key = jax.random.key(seed)
    ks = jax.random.split(key, 28)
    f32 = jnp.float32

    def nrm(k, shape, s):
        return jax.random.normal(k, shape, f32) * s

    cw = min(WINDOW, PAST_LEN)
    beta = DEEPNORM_BETA
    col_scale = jnp.concatenate([
        jnp.ones((ATTN_W + KV_W,), f32), jnp.full((KV_W,), beta, f32),
        jnp.ones((2 * D_RNN + 2 * D_MODEL,), f32)])
    u = jax.random.uniform(ks[14], (DEPTH, D_RNN), f32, 0.9, 0.999)
    s = u ** (1.0 / LRU_C)
    return {
        "x_prompt": nrm(ks[0], (BATCH, SEQ, D_MODEL), 1.0),
        "x_sample": nrm(ks[1], (DEC_BATCH, DEC_SEQ, D_MODEL), 1.0),
        "cache_k": nrm(ks[2], (DEPTH, DEC_BATCH, cw, N_KV_HEADS, HEAD_DIM), 1.0),
        "cache_v": nrm(ks[3], (DEPTH, DEC_BATCH, cw, N_KV_HEADS, HEAD_DIM), beta),
        "state_conv": nrm(ks[4], (DEPTH, DEC_BATCH, CONV_W - 1, D_RNN), 1.0),
        "state_h": nrm(ks[5], (DEPTH, DEC_BATCH, D_RNN), 0.5),
        "w_in": nrm(ks[6], (DEPTH, D_MODEL, PROJ_W), D_MODEL ** -0.5) * col_scale,
        "attn_sinks": nrm(ks[7], (DEPTH, N_HEADS), 0.5),
        "conv_w": nrm(ks[8], (DEPTH, CONV_W, D_RNN), CONV_W ** -0.5),
        "conv_b": nrm(ks[9], (DEPTH, D_RNN), 0.02),
        "rg_wa": nrm(ks[10], (DEPTH, N_RNN_BLOCKS, RNN_BLOCK, RNN_BLOCK), RNN_BLOCK ** -0.5),
        "rg_ba": nrm(ks[11], (DEPTH, D_RNN), 0.02),
        "rg_wx": nrm(ks[12], (DEPTH, N_RNN_BLOCKS, RNN_BLOCK, RNN_BLOCK), RNN_BLOCK ** -0.5),
        "rg_bx": nrm(ks[13], (DEPTH, D_RNN), 0.02),
        "rg_lambda": jnp.log(s) - jnp.log1p(-s),
        "w_attn_out": nrm(ks[15], (DEPTH, ATTN_W, D_MODEL), beta * ATTN_W ** -0.5),
        "w_rnn_out": nrm(ks[16], (DEPTH, D_RNN, D_MODEL), beta * D_RNN ** -0.5),
        "w_out": nrm(ks[17], (DEPTH, D_MODEL, D_MODEL), beta * D_MODEL ** -0.5),
        "ln1_g": 1.0 + nrm(ks[18], (DEPTH, D_MODEL), 0.02),
        "ln1_b": nrm(ks[19], (DEPTH, D_MODEL), 0.02),
        "w_router": nrm(ks[20], (DEPTH, D_MODEL, N_EXPERTS), D_MODEL ** -0.5),
        "b_router": nrm(ks[21], (DEPTH, N_EXPERTS), 0.01),
        "w_up": nrm(ks[22], (DEPTH, N_EXPERTS, D_MODEL, 2 * D_FF), beta * D_MODEL ** -0.5),
        "b_up": nrm(ks[23], (DEPTH, N_EXPERTS, 2 * D_FF), 0.02),
        "w_down": nrm(ks[24], (DEPTH, N_EXPERTS, D_FF, D_MODEL), beta * D_FF ** -0.5),
        "b_down": nrm(ks[25], (DEPTH, N_EXPERTS, D_MODEL), 0.02),
        "ln2_g": 1.0 + nrm(ks[26], (DEPTH, D_MODEL), 0.02),
        "ln2_b": nrm(ks[27], (DEPTH, D_MODEL), 0.02),
    }


def reference(x_prompt, x_sample, cache_k, cache_v, state_conv, state_h,
              w_in, attn_sinks, conv_w, conv_b, rg_wa, rg_ba, rg_wx, rg_bx, rg_lambda,
              w_attn_out, w_rnn_out, w_out, ln1_g, ln1_b,
              w_router, b_router, w_up, b_up, w_down, b_down, ln2_g, ln2_b):
    cw = cache_k.shape[2]
    xp, xs = x_prompt, x_sample
    Bp, Bs = xp.shape[0], xs.shape[0]
    nk_p, nv_p, nc_p, nh_p = [], [], [], []
    nk_s, nv_s, nc_s, nh_s = [], [], [], []
    for l in range(DEPTH):
        p = {
            'w_in': w_in[l], 'attn_sinks': attn_sinks[l], 'conv_w': conv_w[l], 'conv_b': conv_b[l],
            'rg_wa': rg_wa[l], 'rg_ba': rg_ba[l], 'rg_wx': rg_wx[l], 'rg_bx': rg_bx[l],
            'rg_lambda': rg_lambda[l], 'w_attn_out': w_attn_out[l], 'w_rnn_out': w_rnn_out[l],
            'w_out': w_out[l],
        }
        moe_l = functools.partial(moe, w_router=w_router[l], b_router=b_router[l], w_up=w_up[l],
                                  b_up=b_up[l], w_down=w_down[l], b_down=b_down[l])
        mix_p, k_p, v_p, c_p, h_p = temporal_block(
            xp, p, functools.partial(prompt_attend, cache_len=cw),
            jnp.zeros((Bp, CONV_W - 1, D_RNN), xp.dtype), jnp.zeros((Bp, D_RNN), jnp.float32))
        xp = layer_norm(DEEPNORM_ALPHA * xp + mix_p, ln1_g[l], ln1_b[l])
        xp = layer_norm(DEEPNORM_ALPHA * xp + moe_l(xp), ln2_g[l], ln2_b[l])
        mix_s, k_s, v_s, c_s, h_s = temporal_block(
            xs, p, functools.partial(sample_attend, ck=cache_k[l], cv=cache_v[l]),
            state_conv[l], state_h[l])
        xs = layer_norm(DEEPNORM_ALPHA * xs + mix_s, ln1_g[l], ln1_b[l])
        xs = layer_norm(DEEPNORM_ALPHA * xs + moe_l(xs), ln2_g[l], ln2_b[l])
        nk_p.append(k_p); nv_p.append(v_p); nc_p.append(c_p); nh_p.append(h_p)
        nk_s.append(k_s); nv_s.append(v_s); nc_s.append(c_s); nh_s.append(h_s)
    return (xp, xs,
            jnp.stack(nk_p), jnp.stack(nv_p), jnp.stack(nc_p), jnp.stack(nh_p),
            jnp.stack(nk_s), jnp.stack(nv_s), jnp.stack(nc_s), jnp.stack(nh_s))
```

```python
import functools

import numpy as np
import jax
import jax.numpy as jnp
from jax import lax
from jax.experimental import pallas as pl
from jax.experimental.pallas import tpu as pltpu

F32 = jnp.float32
BF16 = jnp.bfloat16

D_MODEL = 1024
N_HEADS = 16
N_KV_HEADS = 2
HEAD_DIM = 64
GROUP = N_HEADS // N_KV_HEADS
WINDOW = 128
ATTN_W = N_HEADS * HEAD_DIM
KV_W = N_KV_HEADS * HEAD_DIM
D_RNN = 1280
RNN_BLOCK = 128
N_RNN_BLOCKS = D_RNN // RNN_BLOCK
CONV_W = 4
LRU_C = 8.0
N_EXPERTS = 32
TOP_K = 4
D_FF = 1024
SWIGLU_LIMIT = 7.0
SWIGLU_ALPHA = 1.702
LN_EPS = 1e-5
PAST_LEN = 8192
PROJ_OFFS = (0, ATTN_W, ATTN_W + 2 * KV_W, ATTN_W + 2 * KV_W + D_RNN,
             ATTN_W + 2 * KV_W + 2 * D_RNN, ATTN_W + 2 * KV_W + 2 * D_RNN + D_MODEL,
             ATTN_W + 2 * KV_W + 2 * D_RNN + 2 * D_MODEL)
PROJ_W = PROJ_OFFS[-1]
NEG_BIG = -1e30

ROW_TILE = 256
RNN_TIME_TILE = 64
MOE_BLOCK = 256
GATHER_TILE = 128
SAMPLE_ATTN_TILE = 32
MIB = 1 << 20


def _alibi_slopes():
    h = np.arange(1, N_HEADS + 1, dtype=np.float32)
    return (np.float32(2.0) ** (np.float32(-8.0) * h / np.float32(N_HEADS))).astype(np.float32)


def _params(semantics, vmem_mib):
    return pltpu.CompilerParams(dimension_semantics=semantics, vmem_limit_bytes=vmem_mib * MIB)


def _const_spec(shape):
    nd = len(shape)
    return pl.BlockSpec(shape, lambda *_: (0,) * nd)


def _proj_kernel(x_ref, w_ref, q_ref, kv_ref, xr_ref, yr_ref, ga_ref, gb_ref):
    xb = x_ref[...].astype(BF16)
    outs = (q_ref, kv_ref, xr_ref, yr_ref, ga_ref, gb_ref)
    for n, o_ref in enumerate(outs):
        w = w_ref[:, PROJ_OFFS[n]:PROJ_OFFS[n + 1]]
        o_ref[...] = jnp.dot(xb, w, preferred_element_type=F32).astype(o_ref.dtype)


def _project(x, w_in_bf):
    n = x.shape[0]
    tm = min(ROW_TILE, n)
    widths = [PROJ_OFFS[i + 1] - PROJ_OFFS[i] for i in range(6)]
    dtypes = [BF16, F32, F32, F32, F32, F32]
    return pl.pallas_call(
        _proj_kernel,
        grid=(n // tm,),
        in_specs=[pl.BlockSpec((tm, D_MODEL), lambda i: (i, 0)),
                  _const_spec((D_MODEL, PROJ_W))],
        out_specs=[pl.BlockSpec((tm, w), lambda i: (i, 0)) for w in widths],
        out_shape=[jax.ShapeDtypeStruct((n, w), dt) for w, dt in zip(widths, dtypes)],
        compiler_params=_params(("parallel",), 56),
        name="in_proj",
    )(x, w_in_bf)


def _prompt_attn_kernel(sink_ref, q_ref, kp_ref, kc_ref, vp_ref, vc_ref, o_ref):
    tb = pl.program_id(1)
    kcat = jnp.concatenate([kp_ref[...], kc_ref[...]], axis=0).astype(BF16)
    vcat = jnp.concatenate([vp_ref[...], vc_ref[...]], axis=0).astype(BF16)
    qi = lax.broadcasted_iota(jnp.int32, (WINDOW, 2 * WINDOW), 0)
    kj = lax.broadcasted_iota(jnp.int32, (WINDOW, 2 * WINDOW), 1)
    dist = qi - kj + WINDOW
    first_ok = jnp.where(kj >= WINDOW, 1, jnp.where(tb > 0, 1, 0))
    ok = jnp.where(dist >= 0, jnp.where(dist < WINDOW, first_ok, 0), 0)
    neg_dist = jnp.where(ok > 0, -dist.astype(F32), NEG_BIG)
    slopes = _alibi_slopes()
    scale = HEAD_DIM ** -0.5
    for h in range(N_HEADS):
        g = h // GROUP
        qh = q_ref[:, h * HEAD_DIM:(h + 1) * HEAD_DIM]
        kg = kcat[:, g * HEAD_DIM:(g + 1) * HEAD_DIM]
        vg = vcat[:, g * HEAD_DIM:(g + 1) * HEAD_DIM]
        s = lax.dot_general(qh, kg, (((1,), (1,)), ((), ())), preferred_element_type=F32)
        logits = s * scale + float(slopes[h]) * neg_dist
        sink = sink_ref[h]
        m = jnp.maximum(jnp.max(logits, axis=1, keepdims=True), sink)
        p = jnp.exp(logits - m)
        denom = jnp.sum(p, axis=1, keepdims=True) + jnp.exp(sink - m)
        o = jnp.dot(p.astype(BF16), vg, preferred_element_type=F32) / denom
        o_ref[:, h * HEAD_DIM:(h + 1) * HEAD_DIM] = o.astype(o_ref.dtype)


def _prompt_attention(q, kv, sinks, batch, seq):
    nb = seq // WINDOW
    qv = q.reshape(seq, batch * ATTN_W)
    kvv = kv.reshape(seq, batch * 2 * KV_W)
    cur = lambda off: (lambda b, t: (t, 2 * b + off))
    prev = lambda off: (lambda b, t: (jnp.maximum(t - 1, 0), 2 * b + off))
    o = pl.pallas_call(
        _prompt_attn_kernel,
        grid=(batch, nb),
        in_specs=[pl.BlockSpec(memory_space=pltpu.SMEM),
                  pl.BlockSpec((WINDOW, ATTN_W), lambda b, t: (t, b)),
                  pl.BlockSpec((WINDOW, KV_W), prev(0)),
                  pl.BlockSpec((WINDOW, KV_W), cur(0)),
                  pl.BlockSpec((WINDOW, KV_W), prev(1)),
                  pl.BlockSpec((WINDOW, KV_W), cur(1))],
        out_specs=pl.BlockSpec((WINDOW, ATTN_W), lambda b, t: (t, b)),
        out_shape=jax.ShapeDtypeStruct((seq, batch * ATTN_W), BF16),
        compiler_params=_params(("parallel", "parallel"), 32),
        name="prompt_attn",
    )(sinks, qv, kvv, kvv, kvv, kvv)
    return o.reshape(seq * batch, ATTN_W)


def _sample_attn_kernel(qz_ref, ck_ref, cv_ref, kn_ref, vn_ref, bias_ref, sink_ref, o_ref):
    qz = qz_ref[...]
    s_c = jnp.einsum('bhd,bjd->bhj', qz, ck_ref[...].astype(BF16),
                     preferred_element_type=F32) * (HEAD_DIM ** -0.5)
    s_n = jnp.sum(qz.astype(F32) * kn_ref[...], axis=-1, keepdims=True) * (HEAD_DIM ** -0.5)
    logits = s_c + bias_ref[...][None]
    sink = sink_ref[...][None]
    m = jnp.maximum(jnp.maximum(jnp.max(logits, axis=-1, keepdims=True), s_n), sink)
    p_c = jnp.exp(logits - m)
    p_n = jnp.exp(s_n - m)
    denom = jnp.sum(p_c, axis=-1, keepdims=True) + p_n + jnp.exp(sink - m)
    o = jnp.einsum('bhj,bjd->bhd', p_c.astype(BF16), cv_ref[...].astype(BF16),
                   preferred_element_type=F32)
    o_ref[...] = (o + p_n * vn_ref[...]) / denom


def _sample_attention(q, kv, sinks, ck, cv):
    b, cw = ck.shape[0], ck.shape[1]
    bt = min(SAMPLE_ATTN_TILE, b)
    q4 = q.reshape(b, N_KV_HEADS, GROUP, HEAD_DIM)
    eye = jnp.eye(N_KV_HEADS, dtype=q.dtype)
    qz = (q4[:, :, :, None, :] * eye[None, :, None, :, None]).reshape(b, N_HEADS, KV_W)
    kn = kv[:, :KV_W].reshape(b, 1, KV_W)
    vn = kv[:, KV_W:].reshape(b, 1, KV_W)
    dist = (cw - np.arange(cw)).astype(np.float32)
    bias = np.where(dist[None, :] < WINDOW, -_alibi_slopes()[:, None] * dist[None, :], NEG_BIG)
    o = pl.pallas_call(
        _sample_attn_kernel,
        grid=(b // bt,),
        in_specs=[pl.BlockSpec((bt, N_HEADS, KV_W), lambda i: (i, 0, 0)),
                  pl.BlockSpec((bt, cw, KV_W), lambda i: (i, 0, 0)),
                  pl.BlockSpec((bt, cw, KV_W), lambda i: (i, 0, 0)),
                  pl.BlockSpec((bt, 1, KV_W), lambda i: (i, 0, 0)),
                  pl.BlockSpec((bt, 1, KV_W), lambda i: (i, 0, 0)),
                  _const_spec((N_HEADS, cw)),
                  _const_spec((N_HEADS, 1))],
        out_specs=pl.BlockSpec((bt, N_HEADS, KV_W), lambda i: (i, 0, 0)),
        out_shape=jax.ShapeDtypeStruct((b, N_HEADS, KV_W), F32),
        compiler_params=_params(("parallel",), 32),
        name="sample_attn",
    )(qz, ck.reshape(b, cw, KV_W), cv.reshape(b, cw, KV_W), kn, vn,
      jnp.asarray(bias, F32), sinks.reshape(N_HEADS, 1))
    o4 = o.reshape(b, N_KV_HEADS, GROUP, N_KV_HEADS, HEAD_DIM)
    o_sel = jnp.stack([o4[:, g, :, g, :] for g in range(N_KV_HEADS)], axis=1)
    return o_sel.reshape(b, ATTN_W).astype(BF16)


def _rnn_kernel(xr_ref, yr_ref, cw_ref, cb_ref, wa_ref, ba_ref, wx_ref, bx_ref, lam_ref,
                conv0_ref, h0_ref, y_ref, convo_ref, ho_ref, xp_s, a_s, b_s, h_s, *, nb, tt):
    rows = tt * nb
    halo = (CONV_W - 1) * nb

    @pl.when(pl.program_id(0) == 0)
    def _():
        xp_s[0:halo] = conv0_ref[...]
        h_s[...] = h0_ref[...]

    xp_s[halo:halo + rows] = xr_ref[...]
    nl = -lam_ref[...]
    coef = -LRU_C * (jnp.maximum(nl, 0.0) + jnp.log1p(jnp.exp(-jnp.abs(nl))))
    for n in range(N_RNN_BLOCKS):
        sl = slice(n * RNN_BLOCK, (n + 1) * RNN_BLOCK)
        xc = cb_ref[:, sl] + xp_s[0:rows, sl] * cw_ref[0:1, sl]
        for j in range(1, CONV_W):
            xc = xc + xp_s[j * nb:j * nb + rows, sl] * cw_ref[j:j + 1, sl]
        xcb = xc.astype(BF16)
        r = jax.nn.sigmoid(jnp.dot(xcb, wa_ref[n], preferred_element_type=F32) + ba_ref[:, sl])
        ig = jax.nn.sigmoid(jnp.dot(xcb, wx_ref[n], preferred_element_type=F32) + bx_ref[:, sl])
        log_a = coef[:, sl] * r
        a_s[:, sl] = jnp.exp(log_a)
        b_s[:, sl] = jnp.sqrt(1.0 - jnp.exp(2.0 * log_a)) * (ig * xc)

    def step(t, h):
        off = pl.multiple_of(t * nb, nb)
        h = a_s[pl.ds(off, nb), :] * h + b_s[pl.ds(off, nb), :]
        a_s[pl.ds(off, nb), :] = h
        return h

    h = lax.fori_loop(0, tt, step, h_s[...])
    h_s[...] = h
    yr = yr_ref[...]
    gelu = 0.5 * yr * (1.0 + jnp.tanh(np.float32(np.sqrt(2.0 / np.pi))
                                      * (yr + np.float32(0.044715) * (yr * yr * yr))))
    y_ref[...] = (a_s[...] * gelu).astype(y_ref.dtype)
    tail = xp_s[rows:rows + halo]
    convo_ref[...] = tail
    ho_ref[...] = h
    xp_s[0:halo] = tail


def _rnn_branch(xr, yr, p, conv0, h0, nb):
    rows_total = xr.shape[0]
    t_total = rows_total // nb
    tt = min(RNN_TIME_TILE, t_total)
    rows = tt * nb
    halo = (CONV_W - 1) * nb
    row_spec = pl.BlockSpec((rows, D_RNN), lambda i: (i, 0))
    vec = _const_spec((1, D_RNN))
    return pl.pallas_call(
        functools.partial(_rnn_kernel, nb=nb, tt=tt),
        grid=(t_total // tt,),
        in_specs=[row_spec, row_spec, _const_spec((CONV_W, D_RNN)), vec,
                  _const_spec((N_RNN_BLOCKS, RNN_BLOCK, RNN_BLOCK)), vec,
                  _const_spec((N_RNN_BLOCKS, RNN_BLOCK, RNN_BLOCK)), vec, vec,
                  _const_spec((halo, D_RNN)), _const_spec((nb, D_RNN))],
        out_specs=[row_spec, _const_spec((halo, D_RNN)), _const_spec((nb, D_RNN))],
        out_shape=[jax.ShapeDtypeStruct((rows_total, D_RNN), BF16),
                   jax.ShapeDtypeStruct((halo, D_RNN), F32),
                   jax.ShapeDtypeStruct((nb, D_RNN), F32)],
        scratch_shapes=[pltpu.VMEM((rows + halo, D_RNN), F32),
                        pltpu.VMEM((rows, D_RNN), F32),
                        pltpu.VMEM((rows, D_RNN), F32),
                        pltpu.VMEM((nb, D_RNN), F32)],
        compiler_params=_params(("arbitrary",), 48),
        name="rnn_branch",
    )(xr, yr, p['conv_w'], p['conv_b'], p['rg_wa'], p['rg_ba'], p['rg_wx'], p['rg_bx'],
      p['rg_lambda'], conv0, h0)


def _layer_norm(z, g, b):
    mu = jnp.mean(z, axis=-1, keepdims=True)
    zc = z - mu
    var = jnp.mean(zc * zc, axis=-1, keepdims=True)
    return zc * lax.rsqrt(var + LN_EPS) * g + b


def _mix_kernel(o_ref, y_ref, ga_ref, gb_ref, x_ref, wao_ref, wro_ref, wo_ref, g1_ref, b1_ref,
                wrt_ref, br_ref, cnt0_ref, x1_ref, tope_ref, gate_ref, rank_ref, cnt_ref, run_s,
                *, alpha, tm):
    @pl.when(pl.program_id(0) == 0)
    def _():
        run_s[...] = cnt0_ref[...]

    att = jnp.dot(o_ref[...], wao_ref[...], preferred_element_type=F32)
    rn = jnp.dot(y_ref[...], wro_ref[...], preferred_element_type=F32)
    merged = jax.nn.sigmoid(ga_ref[...]) * att + jax.nn.sigmoid(gb_ref[...]) * rn
    mixed = jnp.dot(merged.astype(BF16), wo_ref[...], preferred_element_type=F32)
    x1 = _layer_norm(alpha * x_ref[...] + mixed, g1_ref[...], b1_ref[...])
    x1_ref[...] = x1

    logits = lax.dot_general(wrt_ref[...], x1.astype(BF16), (((1,), (1,)), ((), ())),
                             preferred_element_type=F32) + br_ref[...]
    eidx = lax.broadcasted_iota(jnp.int32, (N_EXPERTS, tm), 0)
    vals, idxs, hots = [], [], []
    for _ in range(TOP_K):
        m = jnp.max(logits, axis=0, keepdims=True)
        idx = jnp.min(jnp.where(logits == m, eidx, N_EXPERTS), axis=0, keepdims=True)
        hot = eidx == idx
        logits = jnp.where(hot, -jnp.inf, logits)
        vals.append(m)
        idxs.append(idx)
        hots.append(hot)
    exps = [jnp.exp(v - vals[0]) for v in vals]
    total = exps[0] + exps[1] + exps[2] + exps[3]
    gate_ref[...] = jnp.concatenate([e / total for e in exps], axis=0)
    tope_ref[...] = jnp.concatenate(idxs, axis=0)

    assigned = sum(h.astype(F32) for h in hots)
    earlier = (lax.broadcasted_iota(jnp.int32, (tm, tm), 0)
               < lax.broadcasted_iota(jnp.int32, (tm, tm), 1)).astype(BF16)
    before = jnp.dot(assigned.astype(BF16), earlier, preferred_element_type=F32) + run_s[...]
    ranks = [jnp.sum(jnp.where(h, before, 0.0), axis=0, keepdims=True) for h in hots]
    rank_ref[...] = jnp.concatenate(ranks, axis=0).astype(jnp.int32)
    run_s[...] = run_s[...] + jnp.sum(assigned, axis=1, keepdims=True)
    cnt_ref[...] = run_s[...]


def _mix(o_attn, y_rnn, ga, gb, x, p, cnt0, alpha):
    n = x.shape[0]
    tm = min(ROW_TILE, n)
    row = lambda w: pl.BlockSpec((tm, w), lambda i: (i, 0))
    tok = pl.BlockSpec((TOP_K, tm), lambda i: (0, i))
    vec = _const_spec((1, D_MODEL))
    cnt = _const_spec((N_EXPERTS, 1))
    return pl.pallas_call(
        functools.partial(_mix_kernel, alpha=alpha, tm=tm),
        grid=(n // tm,),
        in_specs=[row(ATTN_W), row(D_RNN), row(D_MODEL), row(D_MODEL), row(D_MODEL),
                  _const_spec((ATTN_W, D_MODEL)), _const_spec((D_RNN, D_MODEL)),
                  _const_spec((D_MODEL, D_MODEL)), vec, vec,
                  _const_spec((N_EXPERTS, D_MODEL)), cnt, cnt],
        out_specs=[row(D_MODEL), tok, tok, tok, cnt],
        out_shape=[jax.ShapeDtypeStruct((n, D_MODEL), F32),
                   jax.ShapeDtypeStruct((TOP_K, n), jnp.int32),
                   jax.ShapeDtypeStruct((TOP_K, n), F32),
                   jax.ShapeDtypeStruct((TOP_K, n), jnp.int32),
                   jax.ShapeDtypeStruct((N_EXPERTS, 1), F32)],
        scratch_shapes=[pltpu.VMEM((N_EXPERTS, 1), F32)],
        compiler_params=_params(("arbitrary",), 48),
        name="mix_norm_router",
    )(o_attn, y_rnn, ga, gb, x, p['w_attn_out'], p['w_rnn_out'], p['w_out'], p['ln1_g'],
      p['ln1_b'], p['w_router_t'], p['b_router'], cnt0)


def _row_copy(src, src_row, dst, dst_row, sem):
    return pltpu.make_async_copy(src.at[pl.ds(src_row, 1)], dst.at[pl.ds(dst_row, 1)], sem)


def _dispatch_kernel(dest_ref, x_ref, xs_in_ref, xs_ref, sem, *, n, tm):
    del xs_in_ref
    base = pl.program_id(0) * tm

    def start(r, c):
        for k in range(TOP_K):
            _row_copy(x_ref, r, xs_ref, dest_ref[k * n + base + r], sem).start()
        return c

    def wait(r, c):
        for k in range(TOP_K):
            _row_copy(x_ref, r, xs_ref, dest_ref[k * n + base + r], sem).wait()
        return c

    lax.fori_loop(0, tm, start, 0)
    lax.fori_loop(0, tm, wait, 0)


def _dispatch(x1, dest, xs):
    n = x1.shape[0]
    tm = min(GATHER_TILE, n)
    return pl.pallas_call(
        functools.partial(_dispatch_kernel, n=n, tm=tm),
        grid_spec=pltpu.PrefetchScalarGridSpec(
            num_scalar_prefetch=1,
            grid=(n // tm,),
            in_specs=[pl.BlockSpec((tm, D_MODEL), lambda i, d: (i, 0)),
                      pl.BlockSpec(memory_space=pl.ANY)],
            out_specs=pl.BlockSpec(memory_space=pl.ANY),
            scratch_shapes=[pltpu.SemaphoreType.DMA(())]),
        out_shape=jax.ShapeDtypeStruct(xs.shape, xs.dtype),
        input_output_aliases={2: 0},
        compiler_params=_params(("arbitrary",), 32),
        name="moe_dispatch",
    )(dest, x1, xs)


def _expert_kernel(be_ref, nused_ref, x_ref, wu_ref, bu_ref, wd_ref, bd_ref, y_ref, wu_s, wd_s):
    i = pl.program_id(0)

    @pl.when(i < nused_ref[0])
    def _():
        prev = be_ref[jnp.maximum(i - 1, 0)]

        @pl.when(jnp.logical_or(i == 0, be_ref[i] != prev))
        def _():
            wu_s[...] = wu_ref[0].astype(BF16)
            wd_s[...] = wd_ref[0].astype(BF16)

        h = jnp.dot(x_ref[...].astype(BF16), wu_s[...], preferred_element_type=F32) + bu_ref[0]
        glu = jnp.minimum(h[:, :D_FF], SWIGLU_LIMIT)
        lin = jnp.clip(h[:, D_FF:], -SWIGLU_LIMIT, SWIGLU_LIMIT)
        act = glu * jax.nn.sigmoid(SWIGLU_ALPHA * glu) * (lin + 1.0)
        y_ref[...] = jnp.dot(act.astype(BF16), wd_s[...], preferred_element_type=F32) + bd_ref[0]

    @pl.when(i >= nused_ref[0])
    def _():
        y_ref[...] = jnp.zeros_like(y_ref)


def _experts(xs, block_e, nused, w_up, b_up, w_down, b_down):
    rows = xs.shape[0]
    nblk = rows // MOE_BLOCK
    blk = lambda i, be, nu: (jnp.minimum(i, nu[0] - 1), 0)
    out_blk = lambda i, be, nu: (i, 0)
    exp3 = lambda i, be, nu: (be[i], 0, 0)
    return pl.pallas_call(
        _expert_kernel,
        grid_spec=pltpu.PrefetchScalarGridSpec(
            num_scalar_prefetch=2,
            grid=(nblk,),
            in_specs=[pl.BlockSpec((MOE_BLOCK, D_MODEL), blk),
                      pl.BlockSpec((1, D_MODEL, 2 * D_FF), exp3),
                      pl.BlockSpec((1, 1, 2 * D_FF), exp3),
                      pl.BlockSpec((1, D_FF, D_MODEL), exp3),
                      pl.BlockSpec((1, 1, D_MODEL), exp3)],
            out_specs=pl.BlockSpec((MOE_BLOCK, D_MODEL), out_blk),
            scratch_shapes=[pltpu.VMEM((D_MODEL, 2 * D_FF), BF16),
                            pltpu.VMEM((D_FF, D_MODEL), BF16)]),
        out_shape=jax.ShapeDtypeStruct((rows, D_MODEL), F32),
        compiler_params=_params(("arbitrary",), 56),
        name="moe_experts",
    )(block_e, nused, xs, w_up, b_up.reshape(N_EXPERTS, 1, 2 * D_FF), w_down,
      b_down.reshape(N_EXPERTS, 1, D_MODEL))


def _combine_kernel(dest_ref, yb_ref, x1_ref, gate_ref, g2_ref, b2_ref, x2_ref, buf, sem,
                    *, n, tm, alpha):
    base = pl.program_id(0) * tm

    def start(r, c):
        for k in range(TOP_K):
            _row_copy(yb_ref, dest_ref[k * n + base + r], buf.at[k], r, sem).start()
        return c

    def wait(r, c):
        for k in range(TOP_K):
            _row_copy(yb_ref, dest_ref[k * n + base + r], buf.at[k], r, sem).wait()
        return c

    lax.fori_loop(0, tm, start, 0)
    lax.fori_loop(0, tm, wait, 0)
    gate = gate_ref[...]
    y = buf[0] * gate[:, 0:1]
    for k in range(1, TOP_K):
        y = y + buf[k] * gate[:, k:k + 1]
    x2_ref[...] = _layer_norm(alpha * x1_ref[...] + y, g2_ref[...], b2_ref[...])


def _combine(yb, x1, dest, gate_rows, ln_g, ln_b, alpha):
    n = x1.shape[0]
    tm = min(GATHER_TILE, n)
    return pl.pallas_call(
        functools.partial(_combine_kernel, n=n, tm=tm, alpha=alpha),
        grid_spec=pltpu.PrefetchScalarGridSpec(
            num_scalar_prefetch=1,
            grid=(n // tm,),
            in_specs=[pl.BlockSpec(memory_space=pl.ANY),
                      pl.BlockSpec((tm, D_MODEL), lambda i, d: (i, 0)),
                      pl.BlockSpec((tm, TOP_K), lambda i, d: (i, 0)),
                      pl.BlockSpec((1, D_MODEL), lambda i, d: (0, 0)),
                      pl.BlockSpec((1, D_MODEL), lambda i, d: (0, 0))],
            out_specs=pl.BlockSpec((tm, D_MODEL), lambda i, d: (i, 0)),
            scratch_shapes=[pltpu.VMEM((TOP_K, tm, D_MODEL), F32),
                            pltpu.SemaphoreType.DMA(())]),
        out_shape=jax.ShapeDtypeStruct((n, D_MODEL), F32),
        compiler_params=_params(("arbitrary",), 32),
        name="moe_combine",
    )(dest, yb, x1, gate_rows, ln_g, ln_b)


def _moe_layout(counts, groups):
    n_assign = sum(g[0].shape[1] for g in groups) * TOP_K
    nblk = -(-(n_assign + N_EXPERTS * (MOE_BLOCK - 1)) // MOE_BLOCK)
    padded = (counts + MOE_BLOCK - 1) // MOE_BLOCK * MOE_BLOCK
    pad_end = jnp.cumsum(padded)
    pad_start = pad_end - padded
    nused = pad_end[-1] // MOE_BLOCK
    block_e = jnp.minimum(jnp.searchsorted(pad_end, jnp.arange(nblk, dtype=jnp.int32) * MOE_BLOCK,
                                           side='right'), N_EXPERTS - 1).astype(jnp.int32)
    block_e = jnp.where(jnp.arange(nblk) < nused, block_e, jnp.take(block_e, nused - 1))
    dests = [(pad_start[top_e] + rank).reshape(-1).astype(jnp.int32) for top_e, rank in groups]
    return nblk, block_e, nused.astype(jnp.int32).reshape(1), dests


def _temporal(x, p, attend, conv0, h0, nb, cnt0, alpha):
    q, kv, xr, yr, ga, gb = _project(x, p['w_in'])
    o_attn = attend(q, kv)
    y_rnn, conv_new, h_new = _rnn_branch(xr, yr, p, conv0, h0, nb)
    x1, top_e, gate, rank, cnt = _mix(o_attn, y_rnn, ga, gb, x, p, cnt0, alpha)
    return x1, top_e, gate, rank, cnt, kv, conv_new, h_new


def kernel(x_prompt, x_sample, cache_k, cache_v, state_conv, state_h, w_in, attn_sinks, conv_w,
           conv_b, rg_wa, rg_ba, rg_wx, rg_bx, rg_lambda, w_attn_out, w_rnn_out, w_out, ln1_g,
           ln1_b, w_router, b_router, w_up, b_up, w_down, b_down, ln2_g, ln2_b):
    depth = w_in.shape[0]
    alpha = float((2 * depth) ** 0.25)
    bp, seq, _ = x_prompt.shape
    bs = x_sample.shape[0]
    cw = cache_k.shape[2]
    halo = CONV_W - 1

    xp = x_prompt.transpose(1, 0, 2).reshape(seq * bp, D_MODEL)
    xs = x_sample.reshape(bs, D_MODEL)
    outs = [[] for _ in range(8)]
    zeros_conv = jnp.zeros((halo * bp, D_RNN), F32)
    zeros_h = jnp.zeros((bp, D_RNN), F32)
    zeros_cnt = jnp.zeros((N_EXPERTS, 1), F32)

    for l in range(depth):
        p = {
            'w_in': w_in[l].astype(BF16),
            'conv_w': conv_w[l], 'conv_b': conv_b[l].reshape(1, D_RNN),
            'rg_wa': rg_wa[l].astype(BF16), 'rg_ba': rg_ba[l].reshape(1, D_RNN),
            'rg_wx': rg_wx[l].astype(BF16), 'rg_bx': rg_bx[l].reshape(1, D_RNN),
            'rg_lambda': rg_lambda[l].reshape(1, D_RNN),
            'w_attn_out': w_attn_out[l].astype(BF16), 'w_rnn_out': w_rnn_out[l].astype(BF16),
            'w_out': w_out[l].astype(BF16),
            'ln1_g': ln1_g[l].reshape(1, D_MODEL), 'ln1_b': ln1_b[l].reshape(1, D_MODEL),
            'w_router_t': w_router[l].T.astype(BF16),
            'b_router': b_router[l].reshape(N_EXPERTS, 1),
        }
        sinks = attn_sinks[l]
        g2, b2 = ln2_g[l].reshape(1, D_MODEL), ln2_b[l].reshape(1, D_MODEL)

        x1p, te_p, gt_p, rk_p, cnt_p, kv_p, conv_p, h_p = _temporal(
            xp, p, lambda q, kv: _prompt_attention(q, kv, sinks, bp, seq),
            zeros_conv, zeros_h, bp, zeros_cnt, alpha)
        conv0_s = state_conv[l].transpose(1, 0, 2).reshape(halo * bs, D_RNN)
        x1s, te_s, gt_s, rk_s, cnt_s, kv_s, conv_s, h_s = _temporal(
            xs, p, lambda q, kv: _sample_attention(q, kv, sinks, cache_k[l], cache_v[l]),
            conv0_s, state_h[l], bs, cnt_p, alpha)

        counts = cnt_s.reshape(N_EXPERTS).astype(jnp.int32)
        nblk, block_e, nused, (dest_p, dest_s) = _moe_layout(counts, [(te_p, rk_p), (te_s, rk_s)])
        sorted_x = jnp.zeros((nblk * MOE_BLOCK, D_MODEL), F32)
        sorted_x = _dispatch(x1p, dest_p, sorted_x)
        sorted_x = _dispatch(x1s, dest_s, sorted_x)
        yb = _experts(sorted_x, block_e, nused, w_up[l], b_up[l], w_down[l], b_down[l])
        xp = _combine(yb, x1p, dest_p, gt_p.T, g2, b2, alpha)
        xs = _combine(yb, x1s, dest_s, gt_s.T, g2, b2, alpha)

        kv_p4 = kv_p.reshape(seq, bp, 2, N_KV_HEADS, HEAD_DIM)[seq - cw:]
        outs[0].append(kv_p4[:, :, 0].transpose(1, 0, 2, 3))
        outs[1].append(kv_p4[:, :, 1].transpose(1, 0, 2, 3))
        outs[2].append(conv_p.reshape(halo, bp, D_RNN).transpose(1, 0, 2))
        outs[3].append(h_p)
        kv_s4 = kv_s.reshape(bs, 1, 2, N_KV_HEADS, HEAD_DIM)
        outs[4].append(jnp.concatenate([cache_k[l], kv_s4[:, :, 0]], axis=1)[:, -cw:])
        outs[5].append(jnp.concatenate([cache_v[l], kv_s4[:, :, 1]], axis=1)[:, -cw:])
        outs[6].append(conv_s.reshape(halo, bs, D_RNN).transpose(1, 0, 2))
        outs[7].append(h_s)

    y_prompt = xp.reshape(seq, bp, D_MODEL).transpose(1, 0, 2)
    y_sample = xs.reshape(bs, 1, D_MODEL)
    return (y_prompt, y_sample) + tuple(jnp.stack(o) for o in outs)
```

```python
import functools

import numpy as np
import jax
import jax.numpy as jnp
from jax import lax
from jax.experimental import pallas as pl
from jax.experimental.pallas import tpu as pltpu

F32 = jnp.float32
BF16 = jnp.bfloat16

D_MODEL = 1024
N_HEADS = 16
N_KV_HEADS = 2
HEAD_DIM = 64
GROUP = N_HEADS // N_KV_HEADS
WINDOW = 128
ATTN_W = N_HEADS * HEAD_DIM
KV_W = N_KV_HEADS * HEAD_DIM
D_RNN = 1280
RNN_BLOCK = 128
N_RNN_BLOCKS = D_RNN // RNN_BLOCK
CONV_W = 4
LRU_C = 8.0
N_EXPERTS = 32
TOP_K = 4
D_FF = 1024
SWIGLU_LIMIT = 7.0
SWIGLU_ALPHA = 1.702
LN_EPS = 1e-5
PAST_LEN = 8192
PROJ_OFFS = (0, ATTN_W, ATTN_W + 2 * KV_W, ATTN_W + 2 * KV_W + D_RNN,
             ATTN_W + 2 * KV_W + 2 * D_RNN, ATTN_W + 2 * KV_W + 2 * D_RNN + D_MODEL,
             ATTN_W + 2 * KV_W + 2 * D_RNN + 2 * D_MODEL)
PROJ_W = PROJ_OFFS[-1]
NEG_BIG = -1e30

ROW_TILE = 256
RNN_TIME_TILE = 64
MOE_BLOCK = 512
GATHER_TILE = 128
ISSUE_UNROLL = 4
SAMPLE_ATTN_TILE = 32
MIB = 1 << 20


def _alibi_slopes():
    h = np.arange(1, N_HEADS + 1, dtype=np.float32)
    return (np.float32(2.0) ** (np.float32(-8.0) * h / np.float32(N_HEADS))).astype(np.float32)


def _params(semantics, vmem_mib):
    return pltpu.CompilerParams(dimension_semantics=semantics, vmem_limit_bytes=vmem_mib * MIB)


def _const_spec(shape):
    nd = len(shape)
    return pl.BlockSpec(shape, lambda *_: (0,) * nd)


def _proj_kernel(x_ref, w_ref, q_ref, kv_ref, xr_ref, yr_ref, ga_ref, gb_ref):
    xb = x_ref[...].astype(BF16)
    outs = (q_ref, kv_ref, xr_ref, yr_ref, ga_ref, gb_ref)
    for n, o_ref in enumerate(outs):
        w = w_ref[:, PROJ_OFFS[n]:PROJ_OFFS[n + 1]]
        o_ref[...] = jnp.dot(xb, w, preferred_element_type=F32).astype(o_ref.dtype)


def _project(x, w_in_bf):
    n = x.shape[0]
    tm = min(ROW_TILE, n)
    widths = [PROJ_OFFS[i + 1] - PROJ_OFFS[i] for i in range(6)]
    dtypes = [BF16, F32, F32, F32, F32, F32]
    return pl.pallas_call(
        _proj_kernel,
        grid=(n // tm,),
        in_specs=[pl.BlockSpec((tm, D_MODEL), lambda i: (i, 0)),
                  _const_spec((D_MODEL, PROJ_W))],
        out_specs=[pl.BlockSpec((tm, w), lambda i: (i, 0)) for w in widths],
        out_shape=[jax.ShapeDtypeStruct((n, w), dt) for w, dt in zip(widths, dtypes)],
        compiler_params=_params(("parallel",), 56),
        name="in_proj",
    )(x, w_in_bf)


def _prompt_attn_kernel(sink_ref, q_ref, kp_ref, kc_ref, vp_ref, vc_ref, o_ref):
    tb = pl.program_id(1)
    kcat = jnp.concatenate([kp_ref[...], kc_ref[...]], axis=0).astype(BF16)
    vcat = jnp.concatenate([vp_ref[...], vc_ref[...]], axis=0).astype(BF16)
    qi = lax.broadcasted_iota(jnp.int32, (WINDOW, 2 * WINDOW), 0)
    kj = lax.broadcasted_iota(jnp.int32, (WINDOW, 2 * WINDOW), 1)
    dist = qi - kj + WINDOW
    first_ok = jnp.where(kj >= WINDOW, 1, jnp.where(tb > 0, 1, 0))
    ok = jnp.where(dist >= 0, jnp.where(dist < WINDOW, first_ok, 0), 0)
    neg_dist = jnp.where(ok > 0, -dist.astype(F32), NEG_BIG)
    slopes = _alibi_slopes()
    scale = HEAD_DIM ** -0.5
    for h in range(N_HEADS):
        g = h // GROUP
        qh = q_ref[:, h * HEAD_DIM:(h + 1) * HEAD_DIM]
        kg = kcat[:, g * HEAD_DIM:(g + 1) * HEAD_DIM]
        vg = vcat[:, g * HEAD_DIM:(g + 1) * HEAD_DIM]
        s = lax.dot_general(qh, kg, (((1,), (1,)), ((), ())), preferred_element_type=F32)
        logits = s * scale + float(slopes[h]) * neg_dist
        sink = sink_ref[h]
        m = jnp.maximum(jnp.max(logits, axis=1, keepdims=True), sink)
        p = jnp.exp(logits - m)
        denom = jnp.sum(p, axis=1, keepdims=True) + jnp.exp(sink - m)
        o = jnp.dot(p.astype(BF16), vg, preferred_element_type=F32) / denom
        o_ref[:, h * HEAD_DIM:(h + 1) * HEAD_DIM] = o.astype(o_ref.dtype)


def _prompt_attention(q, kv, sinks, batch, seq):
    nb = seq // WINDOW
    qv = q.reshape(seq, batch * ATTN_W)
    kvv = kv.reshape(seq, batch * 2 * KV_W)
    cur = lambda off: (lambda b, t: (t, 2 * b + off))
    prev = lambda off: (lambda b, t: (jnp.maximum(t - 1, 0), 2 * b + off))
    o = pl.pallas_call(
        _prompt_attn_kernel,
        grid=(batch, nb),
        in_specs=[pl.BlockSpec(memory_space=pltpu.SMEM),
                  pl.BlockSpec((WINDOW, ATTN_W), lambda b, t: (t, b)),
                  pl.BlockSpec((WINDOW, KV_W), prev(0)),
                  pl.BlockSpec((WINDOW, KV_W), cur(0)),
                  pl.BlockSpec((WINDOW, KV_W), prev(1)),
                  pl.BlockSpec((WINDOW, KV_W), cur(1))],
        out_specs=pl.BlockSpec((WINDOW, ATTN_W), lambda b, t: (t, b)),
        out_shape=jax.ShapeDtypeStruct((seq, batch * ATTN_W), BF16),
        compiler_params=_params(("parallel", "parallel"), 32),
        name="prompt_attn",
    )(sinks, qv, kvv, kvv, kvv, kvv)
    return o.reshape(seq * batch, ATTN_W)


def _sample_attn_kernel(qz_ref, ck_ref, cv_ref, kn_ref, vn_ref, bias_ref, sink_ref, o_ref):
    qz = qz_ref[...]
    s_c = jnp.einsum('bhd,bjd->bhj', qz, ck_ref[...].astype(BF16),
                     preferred_element_type=F32) * (HEAD_DIM ** -0.5)
    s_n = jnp.sum(qz.astype(F32) * kn_ref[...], axis=-1, keepdims=True) * (HEAD_DIM ** -0.5)
    logits = s_c + bias_ref[...][None]
    sink = sink_ref[...][None]
    m = jnp.maximum(jnp.maximum(jnp.max(logits, axis=-1, keepdims=True), s_n), sink)
    p_c = jnp.exp(logits - m)
    p_n = jnp.exp(s_n - m)
    denom = jnp.sum(p_c, axis=-1, keepdims=True) + p_n + jnp.exp(sink - m)
    o = jnp.einsum('bhj,bjd->bhd', p_c.astype(BF16), cv_ref[...].astype(BF16),
                   preferred_element_type=F32)
    o_ref[...] = (o + p_n * vn_ref[...]) / denom


def _sample_attention(q, kv, sinks, ck, cv):
    b, cw = ck.shape[0], ck.shape[1]
    bt = min(SAMPLE_ATTN_TILE, b)
    q4 = q.reshape(b, N_KV_HEADS, GROUP, HEAD_DIM)
    eye = jnp.eye(N_KV_HEADS, dtype=q.dtype)
    qz = (q4[:, :, :, None, :] * eye[None, :, None, :, None]).reshape(b, N_HEADS, KV_W)
    kn = kv[:, :KV_W].reshape(b, 1, KV_W)
    vn = kv[:, KV_W:].reshape(b, 1, KV_W)
    dist = (cw - np.arange(cw)).astype(np.float32)
    bias = np.where(dist[None, :] < WINDOW, -_alibi_slopes()[:, None] * dist[None, :], NEG_BIG)
    o = pl.pallas_call(
        _sample_attn_kernel,
        grid=(b // bt,),
        in_specs=[pl.BlockSpec((bt, N_HEADS, KV_W), lambda i: (i, 0, 0)),
                  pl.BlockSpec((bt, cw, KV_W), lambda i: (i, 0, 0)),
                  pl.BlockSpec((bt, cw, KV_W), lambda i: (i, 0, 0)),
                  pl.BlockSpec((bt, 1, KV_W), lambda i: (i, 0, 0)),
                  pl.BlockSpec((bt, 1, KV_W), lambda i: (i, 0, 0)),
                  _const_spec((N_HEADS, cw)),
                  _const_spec((N_HEADS, 1))],
        out_specs=pl.BlockSpec((bt, N_HEADS, KV_W), lambda i: (i, 0, 0)),
        out_shape=jax.ShapeDtypeStruct((b, N_HEADS, KV_W), F32),
        compiler_params=_params(("parallel",), 32),
        name="sample_attn",
    )(qz, ck.reshape(b, cw, KV_W), cv.reshape(b, cw, KV_W), kn, vn,
      jnp.asarray(bias, F32), sinks.reshape(N_HEADS, 1))
    o4 = o.reshape(b, N_KV_HEADS, GROUP, N_KV_HEADS, HEAD_DIM)
    o_sel = jnp.stack([o4[:, g, :, g, :] for g in range(N_KV_HEADS)], axis=1)
    return o_sel.reshape(b, ATTN_W).astype(BF16)


def _rnn_kernel(xr_ref, yr_ref, cw_ref, cb_ref, wa_ref, ba_ref, wx_ref, bx_ref, lam_ref,
                conv0_ref, h0_ref, y_ref, convo_ref, ho_ref, xp_s, a_s, b_s, h_s, *, nb, tt):
    rows = tt * nb
    halo = (CONV_W - 1) * nb

    @pl.when(pl.program_id(0) == 0)
    def _():
        xp_s[0:halo] = conv0_ref[...]
        h_s[...] = h0_ref[...]

    xp_s[halo:halo + rows] = xr_ref[...]
    nl = -lam_ref[...]
    coef = -LRU_C * (jnp.maximum(nl, 0.0) + jnp.log1p(jnp.exp(-jnp.abs(nl))))
    for n in range(N_RNN_BLOCKS):
        sl = slice(n * RNN_BLOCK, (n + 1) * RNN_BLOCK)
        xc = cb_ref[:, sl] + xp_s[0:rows, sl] * cw_ref[0:1, sl]
        for j in range(1, CONV_W):
            xc = xc + xp_s[j * nb:j * nb + rows, sl] * cw_ref[j:j + 1, sl]
        xcb = xc.astype(BF16)
        r = jax.nn.sigmoid(jnp.dot(xcb, wa_ref[n], preferred_element_type=F32) + ba_ref[:, sl])
        ig = jax.nn.sigmoid(jnp.dot(xcb, wx_ref[n], preferred_element_type=F32) + bx_ref[:, sl])
        log_a = coef[:, sl] * r
        a_s[:, sl] = jnp.exp(log_a)
        b_s[:, sl] = jnp.sqrt(1.0 - jnp.exp(2.0 * log_a)) * (ig * xc)

    def step(t, h):
        off = pl.multiple_of(t * nb, nb)
        h = a_s[pl.ds(off, nb), :] * h + b_s[pl.ds(off, nb), :]
        a_s[pl.ds(off, nb), :] = h
        return h

    h = lax.fori_loop(0, tt, step, h_s[...])
    h_s[...] = h
    yr = yr_ref[...]
    gelu = 0.5 * yr * (1.0 + jnp.tanh(np.float32(np.sqrt(2.0 / np.pi))
                                      * (yr + np.float32(0.044715) * (yr * yr * yr))))
    y_ref[...] = (a_s[...] * gelu).astype(y_ref.dtype)
    tail = xp_s[rows:rows + halo]
    convo_ref[...] = tail
    ho_ref[...] = h
    xp_s[0:halo] = tail


def _rnn_branch(xr, yr, p, conv0, h0, nb):
    rows_total = xr.shape[0]
    t_total = rows_total // nb
    tt = min(RNN_TIME_TILE, t_total)
    rows = tt * nb
    halo = (CONV_W - 1) * nb
    row_spec = pl.BlockSpec((rows, D_RNN), lambda i: (i, 0))
    vec = _const_spec((1, D_RNN))
    return pl.pallas_call(
        functools.partial(_rnn_kernel, nb=nb, tt=tt),
        grid=(t_total // tt,),
        in_specs=[row_spec, row_spec, _const_spec((CONV_W, D_RNN)), vec,
                  _const_spec((N_RNN_BLOCKS, RNN_BLOCK, RNN_BLOCK)), vec,
                  _const_spec((N_RNN_BLOCKS, RNN_BLOCK, RNN_BLOCK)), vec, vec,
                  _const_spec((halo, D_RNN)), _const_spec((nb, D_RNN))],
        out_specs=[row_spec, _const_spec((halo, D_RNN)), _const_spec((nb, D_RNN))],
        out_shape=[jax.ShapeDtypeStruct((rows_total, D_RNN), BF16),
                   jax.ShapeDtypeStruct((halo, D_RNN), F32),
                   jax.ShapeDtypeStruct((nb, D_RNN), F32)],
        scratch_shapes=[pltpu.VMEM((rows + halo, D_RNN), F32),
                        pltpu.VMEM((rows, D_RNN), F32),
                        pltpu.VMEM((rows, D_RNN), F32),
                        pltpu.VMEM((nb, D_RNN), F32)],
        compiler_params=_params(("arbitrary",), 48),
        name="rnn_branch",
    )(xr, yr, p['conv_w'], p['conv_b'], p['rg_wa'], p['rg_ba'], p['rg_wx'], p['rg_bx'],
      p['rg_lambda'], conv0, h0)


def _layer_norm(z, g, b):
    mu = jnp.mean(z, axis=-1, keepdims=True)
    zc = z - mu
    var = jnp.mean(zc * zc, axis=-1, keepdims=True)
    return zc * lax.rsqrt(var + LN_EPS) * g + b


def _mix_kernel(o_ref, y_ref, ga_ref, gb_ref, x_ref, wao_ref, wro_ref, wo_ref, g1_ref, b1_ref,
                wrt_ref, br_ref, cnt0_ref, x1_ref, tope_ref, gate_ref, rank_ref, cnt_ref, run_s,
                *, alpha, tm):
    @pl.when(pl.program_id(0) == 0)
    def _():
        run_s[...] = cnt0_ref[...]

    att = jnp.dot(o_ref[...], wao_ref[...], preferred_element_type=F32)
    rn = jnp.dot(y_ref[...], wro_ref[...], preferred_element_type=F32)
    merged = jax.nn.sigmoid(ga_ref[...]) * att + jax.nn.sigmoid(gb_ref[...]) * rn
    mixed = jnp.dot(merged.astype(BF16), wo_ref[...], preferred_element_type=F32)
    x1 = _layer_norm(alpha * x_ref[...] + mixed, g1_ref[...], b1_ref[...])
    x1_ref[...] = x1

    logits = lax.dot_general(wrt_ref[...], x1.astype(BF16), (((1,), (1,)), ((), ())),
                             preferred_element_type=F32) + br_ref[...]
    eidx = lax.broadcasted_iota(jnp.int32, (N_EXPERTS, tm), 0)
    vals, idxs, hots = [], [], []
    for _ in range(TOP_K):
        m = jnp.max(logits, axis=0, keepdims=True)
        idx = jnp.min(jnp.where(logits == m, eidx, N_EXPERTS), axis=0, keepdims=True)
        hot = eidx == idx
        logits = jnp.where(hot, -jnp.inf, logits)
        vals.append(m)
        idxs.append(idx)
        hots.append(hot)
    exps = [jnp.exp(v - vals[0]) for v in vals]
    total = exps[0] + exps[1] + exps[2] + exps[3]
    gate_ref[...] = jnp.concatenate([e / total for e in exps], axis=0)
    tope_ref[...] = jnp.concatenate(idxs, axis=0)

    assigned = sum(h.astype(F32) for h in hots)
    earlier = (lax.broadcasted_iota(jnp.int32, (tm, tm), 0)
               < lax.broadcasted_iota(jnp.int32, (tm, tm), 1)).astype(BF16)
    before = jnp.dot(assigned.astype(BF16), earlier, preferred_element_type=F32) + run_s[...]
    ranks = [jnp.sum(jnp.where(h, before, 0.0), axis=0, keepdims=True) for h in hots]
    rank_ref[...] = jnp.concatenate(ranks, axis=0).astype(jnp.int32)
    run_s[...] = run_s[...] + jnp.sum(assigned, axis=1, keepdims=True)
    cnt_ref[...] = run_s[...]


def _mix(o_attn, y_rnn, ga, gb, x, p, cnt0, alpha):
    n = x.shape[0]
    tm = min(ROW_TILE, n)
    row = lambda w: pl.BlockSpec((tm, w), lambda i: (i, 0))
    tok = pl.BlockSpec((TOP_K, tm), lambda i: (0, i))
    vec = _const_spec((1, D_MODEL))
    cnt = _const_spec((N_EXPERTS, 1))
    return pl.pallas_call(
        functools.partial(_mix_kernel, alpha=alpha, tm=tm),
        grid=(n // tm,),
        in_specs=[row(ATTN_W), row(D_RNN), row(D_MODEL), row(D_MODEL), row(D_MODEL),
                  _const_spec((ATTN_W, D_MODEL)), _const_spec((D_RNN, D_MODEL)),
                  _const_spec((D_MODEL, D_MODEL)), vec, vec,
                  _const_spec((N_EXPERTS, D_MODEL)), cnt, cnt],
        out_specs=[row(D_MODEL), tok, tok, tok, cnt],
        out_shape=[jax.ShapeDtypeStruct((n, D_MODEL), F32),
                   jax.ShapeDtypeStruct((TOP_K, n), jnp.int32),
                   jax.ShapeDtypeStruct((TOP_K, n), F32),
                   jax.ShapeDtypeStruct((TOP_K, n), jnp.int32),
                   jax.ShapeDtypeStruct((N_EXPERTS, 1), F32)],
        scratch_shapes=[pltpu.VMEM((N_EXPERTS, 1), F32)],
        compiler_params=_params(("arbitrary",), 48),
        name="mix_norm_router",
    )(o_attn, y_rnn, ga, gb, x, p['w_attn_out'], p['w_rnn_out'], p['w_out'], p['ln1_g'],
      p['ln1_b'], p['w_router_t'], p['b_router'], cnt0)


def _row_copy(src, src_row, dst, dst_row, sem):
    return pltpu.make_async_copy(src.at[pl.ds(src_row, 1)], dst.at[pl.ds(dst_row, 1)], sem)


def _dispatch_kernel(dest_ref, x_ref, xs_in_ref, xs_ref, sems, *, n, tm):
    del xs_in_ref
    i = pl.program_id(0)
    slot = i % 2

    def copies(step, sem_slot, act):
        base = step * tm

        def body(r, c):
            for k in range(TOP_K):
                act(_row_copy(x_ref, base + r, xs_ref, dest_ref[k * n + base + r],
                              sems.at[sem_slot]), k)
            return c
        return body

    start = lambda cp, k: cp.start(priority=k % 2)
    wait = lambda cp, k: cp.wait()
    lax.fori_loop(0, tm, copies(i, slot, start), 0, unroll=ISSUE_UNROLL)

    @pl.when(i > 0)
    def _():
        lax.fori_loop(0, tm, copies(i - 1, 1 - slot, wait), 0)

    @pl.when(i == pl.num_programs(0) - 1)
    def _():
        lax.fori_loop(0, tm, copies(i, slot, wait), 0)


def _dispatch(x1, dest, xs):
    n = x1.shape[0]
    tm = min(GATHER_TILE, n)
    return pl.pallas_call(
        functools.partial(_dispatch_kernel, n=n, tm=tm),
        grid_spec=pltpu.PrefetchScalarGridSpec(
            num_scalar_prefetch=1,
            grid=(n // tm,),
            in_specs=[pl.BlockSpec(memory_space=pl.ANY),
                      pl.BlockSpec(memory_space=pl.ANY)],
            out_specs=pl.BlockSpec(memory_space=pl.ANY),
            scratch_shapes=[pltpu.SemaphoreType.DMA((2,))]),
        out_shape=jax.ShapeDtypeStruct(xs.shape, xs.dtype),
        input_output_aliases={2: 0},
        compiler_params=_params(("arbitrary",), 32),
        name="moe_dispatch",
    )(dest, x1, xs)


def _expert_kernel(be_ref, nused_ref, x_ref, wu_ref, bu_ref, wd_ref, bd_ref, y_ref, wu_s, wd_s):
    i = pl.program_id(0)

    @pl.when(i < nused_ref[0])
    def _():
        prev = be_ref[jnp.maximum(i - 1, 0)]

        @pl.when(jnp.logical_or(i == 0, be_ref[i] != prev))
        def _():
            wu_s[...] = wu_ref[0, 0].astype(BF16)
            wd_s[...] = wd_ref[0, 0].astype(BF16)

        h = jnp.dot(x_ref[...].astype(BF16), wu_s[...], preferred_element_type=F32) + bu_ref[0, 0]
        glu = jnp.minimum(h[:, :D_FF], SWIGLU_LIMIT)
        lin = jnp.clip(h[:, D_FF:], -SWIGLU_LIMIT, SWIGLU_LIMIT)
        act = glu * jax.nn.sigmoid(SWIGLU_ALPHA * glu) * (lin + 1.0)
        y_ref[...] = jnp.dot(act.astype(BF16), wd_s[...], preferred_element_type=F32) + bd_ref[0, 0]

    @pl.when(i >= nused_ref[0])
    def _():
        y_ref[...] = jnp.zeros_like(y_ref)


def _experts(xs, block_e, nused, layer, w_up, b_up, w_down, b_down):
    rows = xs.shape[0]
    nblk = rows // MOE_BLOCK
    depth = w_up.shape[0]
    blk = lambda i, be, nu: (jnp.minimum(i, nu[0] - 1), 0)
    out_blk = lambda i, be, nu: (i, 0)
    exp3 = lambda i, be, nu: (layer, be[i], 0, 0)
    return pl.pallas_call(
        _expert_kernel,
        grid_spec=pltpu.PrefetchScalarGridSpec(
            num_scalar_prefetch=2,
            grid=(nblk,),
            in_specs=[pl.BlockSpec((MOE_BLOCK, D_MODEL), blk),
                      pl.BlockSpec((1, 1, D_MODEL, 2 * D_FF), exp3),
                      pl.BlockSpec((1, 1, 1, 2 * D_FF), exp3),
                      pl.BlockSpec((1, 1, D_FF, D_MODEL), exp3),
                      pl.BlockSpec((1, 1, 1, D_MODEL), exp3)],
            out_specs=pl.BlockSpec((MOE_BLOCK, D_MODEL), out_blk),
            scratch_shapes=[pltpu.VMEM((D_MODEL, 2 * D_FF), BF16),
                            pltpu.VMEM((D_FF, D_MODEL), BF16)]),
        out_shape=jax.ShapeDtypeStruct((rows, D_MODEL), F32),
        compiler_params=_params(("arbitrary",), 56),
        name="moe_experts",
    )(block_e, nused, xs, w_up, b_up.reshape(depth, N_EXPERTS, 1, 2 * D_FF), w_down,
      b_down.reshape(depth, N_EXPERTS, 1, D_MODEL))


def _combine_kernel(dest_ref, yb_ref, x1_ref, gate_ref, g2_ref, b2_ref, x2_ref, buf, sems,
                    *, n, tm, alpha):
    i = pl.program_id(0)
    slot = i % 2

    def copies(step, buf_slot, act):
        base = step * tm

        def body(r, c):
            for k in range(TOP_K):
                act(_row_copy(yb_ref, dest_ref[k * n + base + r], buf.at[buf_slot, k], r,
                              sems.at[buf_slot]), k)
            return c
        return body

    start = lambda cp, k: cp.start(priority=k % 2)
    wait = lambda cp, k: cp.wait()

    @pl.when(i == 0)
    def _():
        lax.fori_loop(0, tm, copies(0, 0, start), 0, unroll=ISSUE_UNROLL)

    @pl.when(i + 1 < pl.num_programs(0))
    def _():
        lax.fori_loop(0, tm, copies(i + 1, 1 - slot, start), 0, unroll=ISSUE_UNROLL)

    lax.fori_loop(0, tm, copies(i, slot, wait), 0)
    gate = gate_ref[...]
    y = buf[slot, 0] * gate[:, 0:1]
    for k in range(1, TOP_K):
        y = y + buf[slot, k] * gate[:, k:k + 1]
    x2_ref[...] = _layer_norm(alpha * x1_ref[...] + y, g2_ref[...], b2_ref[...])


def _combine(yb, x1, dest, gate_rows, ln_g, ln_b, alpha):
    n = x1.shape[0]
    tm = min(GATHER_TILE, n)
    return pl.pallas_call(
        functools.partial(_combine_kernel, n=n, tm=tm, alpha=alpha),
        grid_spec=pltpu.PrefetchScalarGridSpec(
            num_scalar_prefetch=1,
            grid=(n // tm,),
            in_specs=[pl.BlockSpec(memory_space=pl.ANY),
                      pl.BlockSpec((tm, D_MODEL), lambda i, d: (i, 0)),
                      pl.BlockSpec((tm, TOP_K), lambda i, d: (i, 0)),
                      pl.BlockSpec((1, D_MODEL), lambda i, d: (0, 0)),
                      pl.BlockSpec((1, D_MODEL), lambda i, d: (0, 0))],
            out_specs=pl.BlockSpec((tm, D_MODEL), lambda i, d: (i, 0)),
            scratch_shapes=[pltpu.VMEM((2, TOP_K, tm, D_MODEL), F32),
                            pltpu.SemaphoreType.DMA((2,))]),
        out_shape=jax.ShapeDtypeStruct((n, D_MODEL), F32),
        compiler_params=_params(("arbitrary",), 32),
        name="moe_combine",
    )(dest, yb, x1, gate_rows, ln_g, ln_b)


def _moe_layout(counts, groups):
    n_assign = sum(g[0].shape[1] for g in groups) * TOP_K
    nblk = -(-(n_assign + N_EXPERTS * (MOE_BLOCK - 1)) // MOE_BLOCK)
    padded = (counts + MOE_BLOCK - 1) // MOE_BLOCK * MOE_BLOCK
    pad_end = jnp.cumsum(padded)
    pad_start = pad_end - padded
    nused = pad_end[-1] // MOE_BLOCK

    def expert_of(row):
        ended = (pad_end[None, :] <= row[:, None]).astype(jnp.int32)
        return jnp.minimum(jnp.sum(ended, axis=1), N_EXPERTS - 1)

    blk = jnp.arange(nblk, dtype=jnp.int32)
    block_e = expert_of(jnp.minimum(blk, nused - 1) * MOE_BLOCK)
    eids = jnp.arange(N_EXPERTS, dtype=jnp.int32)
    dests = [(jnp.sum(jnp.where(top_e[..., None] == eids, pad_start, 0), axis=-1) + rank)
             .reshape(-1).astype(jnp.int32) for top_e, rank in groups]
    return nblk, block_e, nused.astype(jnp.int32).reshape(1), dests


def _temporal(x, p, attend, conv0, h0, nb, cnt0, alpha):
    q, kv, xr, yr, ga, gb = _project(x, p['w_in'])
    o_attn = attend(q, kv)
    y_rnn, conv_new, h_new = _rnn_branch(xr, yr, p, conv0, h0, nb)
    x1, top_e, gate, rank, cnt = _mix(o_attn, y_rnn, ga, gb, x, p, cnt0, alpha)
    return x1, top_e, gate, rank, cnt, kv, conv_new, h_new


def kernel(x_prompt, x_sample, cache_k, cache_v, state_conv, state_h, w_in, attn_sinks, conv_w,
           conv_b, rg_wa, rg_ba, rg_wx, rg_bx, rg_lambda, w_attn_out, w_rnn_out, w_out, ln1_g,
           ln1_b, w_router, b_router, w_up, b_up, w_down, b_down, ln2_g, ln2_b):
    depth = w_in.shape[0]
    alpha = float((2 * depth) ** 0.25)
    bp, seq, _ = x_prompt.shape
    bs = x_sample.shape[0]
    cw = cache_k.shape[2]
    halo = CONV_W - 1

    xp = x_prompt.transpose(1, 0, 2).reshape(seq * bp, D_MODEL)
    xs = x_sample.reshape(bs, D_MODEL)
    outs = [[] for _ in range(8)]
    zeros_conv = jnp.zeros((halo * bp, D_RNN), F32)
    zeros_h = jnp.zeros((bp, D_RNN), F32)
    zeros_cnt = jnp.zeros((N_EXPERTS, 1), F32)

    for l in range(depth):
        p = {
            'w_in': w_in[l].astype(BF16),
            'conv_w': conv_w[l], 'conv_b': conv_b[l].reshape(1, D_RNN),
            'rg_wa': rg_wa[l].astype(BF16), 'rg_ba': rg_ba[l].reshape(1, D_RNN),
            'rg_wx': rg_wx[l].astype(BF16), 'rg_bx': rg_bx[l].reshape(1, D_RNN),
            'rg_lambda': rg_lambda[l].reshape(1, D_RNN),
            'w_attn_out': w_attn_out[l].astype(BF16), 'w_rnn_out': w_rnn_out[l].astype(BF16),
            'w_out': w_out[l].astype(BF16),
            'ln1_g': ln1_g[l].reshape(1, D_MODEL), 'ln1_b': ln1_b[l].reshape(1, D_MODEL),
            'w_router_t': w_router[l].T.astype(BF16),
            'b_router': b_router[l].reshape(N_EXPERTS, 1),
        }
        sinks = attn_sinks[l]
        g2, b2 = ln2_g[l].reshape(1, D_MODEL), ln2_b[l].reshape(1, D_MODEL)

        x1p, te_p, gt_p, rk_p, cnt_p, kv_p, conv_p, h_p = _temporal(
            xp, p, lambda q, kv: _prompt_attention(q, kv, sinks, bp, seq),
            zeros_conv, zeros_h, bp, zeros_cnt, alpha)
        conv0_s = state_conv[l].transpose(1, 0, 2).reshape(halo * bs, D_RNN)
        x1s, te_s, gt_s, rk_s, cnt_s, kv_s, conv_s, h_s = _temporal(
            xs, p, lambda q, kv: _sample_attention(q, kv, sinks, cache_k[l], cache_v[l]),
            conv0_s, state_h[l], bs, cnt_p, alpha)

        counts = cnt_s.reshape(N_EXPERTS).astype(jnp.int32)
        nblk, block_e, nused, (dest_p, dest_s) = _moe_layout(counts, [(te_p, rk_p), (te_s, rk_s)])
        sorted_x = jnp.zeros((nblk * MOE_BLOCK, D_MODEL), F32)
        sorted_x = _dispatch(x1p, dest_p, sorted_x)
        sorted_x = _dispatch(x1s, dest_s, sorted_x)
        yb = _experts(sorted_x, block_e, nused, l, w_up, b_up, w_down, b_down)
        xp = _combine(yb, x1p, dest_p, gt_p.T, g2, b2, alpha)
        xs = _combine(yb, x1s, dest_s, gt_s.T, g2, b2, alpha)

        kv_p4 = kv_p.reshape(seq, bp, 2, N_KV_HEADS, HEAD_DIM)[seq - cw:]
        outs[0].append(kv_p4[:, :, 0].transpose(1, 0, 2, 3))
        outs[1].append(kv_p4[:, :, 1].transpose(1, 0, 2, 3))
        outs[2].append(conv_p.reshape(halo, bp, D_RNN).transpose(1, 0, 2))
        outs[3].append(h_p)
        kv_s4 = kv_s.reshape(bs, 1, 2, N_KV_HEADS, HEAD_DIM)
        outs[4].append(jnp.concatenate([cache_k[l], kv_s4[:, :, 0]], axis=1)[:, -cw:])
        outs[5].append(jnp.concatenate([cache_v[l], kv_s4[:, :, 1]], axis=1)[:, -cw:])
        outs[6].append(conv_s.reshape(halo, bs, D_RNN).transpose(1, 0, 2))
        outs[7].append(h_s)

    y_prompt = xp.reshape(seq, bp, D_MODEL).transpose(1, 0, 2)
    y_sample = xs.reshape(bs, 1, D_MODEL)
    return (y_prompt, y_sample) + tuple(jnp.stack(o) for o in outs)
```

```python
import functools

import numpy as np
import jax
import jax.numpy as jnp
from jax import lax
from jax.experimental import pallas as pl
from jax.experimental.pallas import tpu as pltpu

F32 = jnp.float32
BF16 = jnp.bfloat16

D_MODEL = 1024
N_HEADS = 16
N_KV_HEADS = 2
HEAD_DIM = 64
GROUP = N_HEADS // N_KV_HEADS
WINDOW = 128
ATTN_W = N_HEADS * HEAD_DIM
KV_W = N_KV_HEADS * HEAD_DIM
D_RNN = 1280
RNN_BLOCK = 128
N_RNN_BLOCKS = D_RNN // RNN_BLOCK
CONV_W = 4
LRU_C = 8.0
N_EXPERTS = 32
TOP_K = 4
D_FF = 1024
SWIGLU_LIMIT = 7.0
SWIGLU_ALPHA = 1.702
LN_EPS = 1e-5
PAST_LEN = 8192
PROJ_OFFS = (0, ATTN_W, ATTN_W + 2 * KV_W, ATTN_W + 2 * KV_W + D_RNN,
             ATTN_W + 2 * KV_W + 2 * D_RNN, ATTN_W + 2 * KV_W + 2 * D_RNN + D_MODEL,
             ATTN_W + 2 * KV_W + 2 * D_RNN + 2 * D_MODEL)
PROJ_W = PROJ_OFFS[-1]
NEG_BIG = -1e30

ROW_TILE = 256
RNN_TIME_TILE = 64
MOE_BLOCK = 512
GATHER_TILE = 128
ISSUE_UNROLL = 4
SAMPLE_ATTN_TILE = 32
MIB = 1 << 20


def _alibi_slopes():
    h = np.arange(1, N_HEADS + 1, dtype=np.float32)
    return (np.float32(2.0) ** (np.float32(-8.0) * h / np.float32(N_HEADS))).astype(np.float32)


def _params(semantics, vmem_mib):
    return pltpu.CompilerParams(dimension_semantics=semantics, vmem_limit_bytes=vmem_mib * MIB)


def _const_spec(shape):
    nd = len(shape)
    return pl.BlockSpec(shape, lambda *_: (0,) * nd)


def _proj_kernel(x_ref, w_ref, q_ref, kv_ref, xr_ref, yr_ref, ga_ref, gb_ref):
    xb = x_ref[...].astype(BF16)
    outs = (q_ref, kv_ref, xr_ref, yr_ref, ga_ref, gb_ref)
    for n, o_ref in enumerate(outs):
        w = w_ref[:, PROJ_OFFS[n]:PROJ_OFFS[n + 1]]
        o_ref[...] = jnp.dot(xb, w, preferred_element_type=F32).astype(o_ref.dtype)


def _project(x, w_in_bf):
    n = x.shape[0]
    tm = min(ROW_TILE, n)
    widths = [PROJ_OFFS[i + 1] - PROJ_OFFS[i] for i in range(6)]
    dtypes = [BF16, F32, F32, F32, F32, F32]
    return pl.pallas_call(
        _proj_kernel,
        grid=(n // tm,),
        in_specs=[pl.BlockSpec((tm, D_MODEL), lambda i: (i, 0)),
                  _const_spec((D_MODEL, PROJ_W))],
        out_specs=[pl.BlockSpec((tm, w), lambda i: (i, 0)) for w in widths],
        out_shape=[jax.ShapeDtypeStruct((n, w), dt) for w, dt in zip(widths, dtypes)],
        compiler_params=_params(("parallel",), 56),
        name="in_proj",
    )(x, w_in_bf)


def _prompt_attn_kernel(sink_ref, q_ref, kp_ref, kc_ref, vp_ref, vc_ref, o_ref):
    tb = pl.program_id(1)
    kcat = jnp.concatenate([kp_ref[...], kc_ref[...]], axis=0).astype(BF16)
    vcat = jnp.concatenate([vp_ref[...], vc_ref[...]], axis=0).astype(BF16)
    qi = lax.broadcasted_iota(jnp.int32, (WINDOW, 2 * WINDOW), 0)
    kj = lax.broadcasted_iota(jnp.int32, (WINDOW, 2 * WINDOW), 1)
    dist = qi - kj + WINDOW
    first_ok = jnp.where(kj >= WINDOW, 1, jnp.where(tb > 0, 1, 0))
    ok = jnp.where(dist >= 0, jnp.where(dist < WINDOW, first_ok, 0), 0)
    neg_dist = jnp.where(ok > 0, -dist.astype(F32), NEG_BIG)
    slopes = _alibi_slopes()
    scale = HEAD_DIM ** -0.5
    for h in range(N_HEADS):
        g = h // GROUP
        qh = q_ref[:, h * HEAD_DIM:(h + 1) * HEAD_DIM]
        kg = kcat[:, g * HEAD_DIM:(g + 1) * HEAD_DIM]
        vg = vcat[:, g * HEAD_DIM:(g + 1) * HEAD_DIM]
        s = lax.dot_general(qh, kg, (((1,), (1,)), ((), ())), preferred_element_type=F32)
        logits = s * scale + float(slopes[h]) * neg_dist
        sink = sink_ref[h]
        m = jnp.maximum(jnp.max(logits, axis=1, keepdims=True), sink)
        p = jnp.exp(logits - m)
        denom = jnp.sum(p, axis=1, keepdims=True) + jnp.exp(sink - m)
        o = jnp.dot(p.astype(BF16), vg, preferred_element_type=F32) / denom
        o_ref[:, h * HEAD_DIM:(h + 1) * HEAD_DIM] = o.astype(o_ref.dtype)


def _prompt_attention(q, kv, sinks, batch, seq):
    nb = seq // WINDOW
    qv = q.reshape(seq, batch * ATTN_W)
    kvv = kv.reshape(seq, batch * 2 * KV_W)
    cur = lambda off: (lambda b, t: (t, 2 * b + off))
    prev = lambda off: (lambda b, t: (jnp.maximum(t - 1, 0), 2 * b + off))
    o = pl.pallas_call(
        _prompt_attn_kernel,
        grid=(batch, nb),
        in_specs=[pl.BlockSpec(memory_space=pltpu.SMEM),
                  pl.BlockSpec((WINDOW, ATTN_W), lambda b, t: (t, b)),
                  pl.BlockSpec((WINDOW, KV_W), prev(0)),
                  pl.BlockSpec((WINDOW, KV_W), cur(0)),
                  pl.BlockSpec((WINDOW, KV_W), prev(1)),
                  pl.BlockSpec((WINDOW, KV_W), cur(1))],
        out_specs=pl.BlockSpec((WINDOW, ATTN_W), lambda b, t: (t, b)),
        out_shape=jax.ShapeDtypeStruct((seq, batch * ATTN_W), BF16),
        compiler_params=_params(("parallel", "parallel"), 32),
        name="prompt_attn",
    )(sinks, qv, kvv, kvv, kvv, kvv)
    return o.reshape(seq * batch, ATTN_W)


def _sample_attn_kernel(qz_ref, ck_ref, cv_ref, kn_ref, vn_ref, bias_ref, sink_ref, o_ref):
    qz = qz_ref[...]
    s_c = jnp.einsum('bhd,bjd->bhj', qz, ck_ref[...].astype(BF16),
                     preferred_element_type=F32) * (HEAD_DIM ** -0.5)
    s_n = jnp.sum(qz.astype(F32) * kn_ref[...], axis=-1, keepdims=True) * (HEAD_DIM ** -0.5)
    logits = s_c + bias_ref[...][None]
    sink = sink_ref[...][None]
    m = jnp.maximum(jnp.maximum(jnp.max(logits, axis=-1, keepdims=True), s_n), sink)
    p_c = jnp.exp(logits - m)
    p_n = jnp.exp(s_n - m)
    denom = jnp.sum(p_c, axis=-1, keepdims=True) + p_n + jnp.exp(sink - m)
    o = jnp.einsum('bhj,bjd->bhd', p_c.astype(BF16), cv_ref[...].astype(BF16),
                   preferred_element_type=F32)
    o_ref[...] = (o + p_n * vn_ref[...]) / denom


def _sample_attention(q, kv, sinks, ck, cv):
    b, cw = ck.shape[0], ck.shape[1]
    bt = min(SAMPLE_ATTN_TILE, b)
    q4 = q.reshape(b, N_KV_HEADS, GROUP, HEAD_DIM)
    eye = jnp.eye(N_KV_HEADS, dtype=q.dtype)
    qz = (q4[:, :, :, None, :] * eye[None, :, None, :, None]).reshape(b, N_HEADS, KV_W)
    kn = kv[:, :KV_W].reshape(b, 1, KV_W)
    vn = kv[:, KV_W:].reshape(b, 1, KV_W)
    dist = (cw - np.arange(cw)).astype(np.float32)
    bias = np.where(dist[None, :] < WINDOW, -_alibi_slopes()[:, None] * dist[None, :], NEG_BIG)
    o = pl.pallas_call(
        _sample_attn_kernel,
        grid=(b // bt,),
        in_specs=[pl.BlockSpec((bt, N_HEADS, KV_W), lambda i: (i, 0, 0)),
                  pl.BlockSpec((bt, cw, KV_W), lambda i: (i, 0, 0)),
                  pl.BlockSpec((bt, cw, KV_W), lambda i: (i, 0, 0)),
                  pl.BlockSpec((bt, 1, KV_W), lambda i: (i, 0, 0)),
                  pl.BlockSpec((bt, 1, KV_W), lambda i: (i, 0, 0)),
                  _const_spec((N_HEADS, cw)),
                  _const_spec((N_HEADS, 1))],
        out_specs=pl.BlockSpec((bt, N_HEADS, KV_W), lambda i: (i, 0, 0)),
        out_shape=jax.ShapeDtypeStruct((b, N_HEADS, KV_W), F32),
        compiler_params=_params(("parallel",), 32),
        name="sample_attn",
    )(qz, ck.reshape(b, cw, KV_W), cv.reshape(b, cw, KV_W), kn, vn,
      jnp.asarray(bias, F32), sinks.reshape(N_HEADS, 1))
    o4 = o.reshape(b, N_KV_HEADS, GROUP, N_KV_HEADS, HEAD_DIM)
    o_sel = jnp.stack([o4[:, g, :, g, :] for g in range(N_KV_HEADS)], axis=1)
    return o_sel.reshape(b, ATTN_W).astype(BF16)


def _rnn_kernel(xr_ref, yr_ref, cw_ref, cb_ref, wa_ref, ba_ref, wx_ref, bx_ref, lam_ref,
                conv0_ref, h0_ref, y_ref, convo_ref, ho_ref, xp_s, a_s, b_s, h_s, *, nb, tt):
    rows = tt * nb
    halo = (CONV_W - 1) * nb

    @pl.when(pl.program_id(0) == 0)
    def _():
        xp_s[0:halo] = conv0_ref[...]
        h_s[...] = h0_ref[...]

    xp_s[halo:halo + rows] = xr_ref[...]
    nl = -lam_ref[...]
    coef = -LRU_C * (jnp.maximum(nl, 0.0) + jnp.log1p(jnp.exp(-jnp.abs(nl))))
    for n in range(N_RNN_BLOCKS):
        sl = slice(n * RNN_BLOCK, (n + 1) * RNN_BLOCK)
        xc = cb_ref[:, sl] + xp_s[0:rows, sl] * cw_ref[0:1, sl]
        for j in range(1, CONV_W):
            xc = xc + xp_s[j * nb:j * nb + rows, sl] * cw_ref[j:j + 1, sl]
        xcb = xc.astype(BF16)
        r = jax.nn.sigmoid(jnp.dot(xcb, wa_ref[n], preferred_element_type=F32) + ba_ref[:, sl])
        ig = jax.nn.sigmoid(jnp.dot(xcb, wx_ref[n], preferred_element_type=F32) + bx_ref[:, sl])
        log_a = coef[:, sl] * r
        a_s[:, sl] = jnp.exp(log_a)
        b_s[:, sl] = jnp.sqrt(1.0 - jnp.exp(2.0 * log_a)) * (ig * xc)

    def step(t, h):
        off = pl.multiple_of(t * nb, nb)
        h = a_s[pl.ds(off, nb), :] * h + b_s[pl.ds(off, nb), :]
        a_s[pl.ds(off, nb), :] = h
        return h

    h = lax.fori_loop(0, tt, step, h_s[...])
    h_s[...] = h
    yr = yr_ref[...]
    gelu = 0.5 * yr * (1.0 + jnp.tanh(np.float32(np.sqrt(2.0 / np.pi))
                                      * (yr + np.float32(0.044715) * (yr * yr * yr))))
    y_ref[...] = (a_s[...] * gelu).astype(y_ref.dtype)
    tail = xp_s[rows:rows + halo]
    convo_ref[...] = tail
    ho_ref[...] = h
    xp_s[0:halo] = tail


def _rnn_branch(xr, yr, p, conv0, h0, nb):
    rows_total = xr.shape[0]
    t_total = rows_total // nb
    tt = min(RNN_TIME_TILE, t_total)
    rows = tt * nb
    halo = (CONV_W - 1) * nb
    row_spec = pl.BlockSpec((rows, D_RNN), lambda i: (i, 0))
    vec = _const_spec((1, D_RNN))
    return pl.pallas_call(
        functools.partial(_rnn_kernel, nb=nb, tt=tt),
        grid=(t_total // tt,),
        in_specs=[row_spec, row_spec, _const_spec((CONV_W, D_RNN)), vec,
                  _const_spec((N_RNN_BLOCKS, RNN_BLOCK, RNN_BLOCK)), vec,
                  _const_spec((N_RNN_BLOCKS, RNN_BLOCK, RNN_BLOCK)), vec, vec,
                  _const_spec((halo, D_RNN)), _const_spec((nb, D_RNN))],
        out_specs=[row_spec, _const_spec((halo, D_RNN)), _const_spec((nb, D_RNN))],
        out_shape=[jax.ShapeDtypeStruct((rows_total, D_RNN), BF16),
                   jax.ShapeDtypeStruct((halo, D_RNN), F32),
                   jax.ShapeDtypeStruct((nb, D_RNN), F32)],
        scratch_shapes=[pltpu.VMEM((rows + halo, D_RNN), F32),
                        pltpu.VMEM((rows, D_RNN), F32),
                        pltpu.VMEM((rows, D_RNN), F32),
                        pltpu.VMEM((nb, D_RNN), F32)],
        compiler_params=_params(("arbitrary",), 48),
        name="rnn_branch",
    )(xr, yr, p['conv_w'], p['conv_b'], p['rg_wa'], p['rg_ba'], p['rg_wx'], p['rg_bx'],
      p['rg_lambda'], conv0, h0)


LANES = 128
LANE_GROUPS = D_MODEL // LANES


def _load_token_tiles(ref, n, lead=()):
    return [ref[lead + (pl.ds(c, n, stride=LANE_GROUPS), slice(None))] for c in range(LANE_GROUPS)]


def _store_token_tiles(ref, x, n):
    for c in range(LANE_GROUPS):
        ref[pl.ds(c, n, stride=LANE_GROUPS), :] = x[:, c * LANES:(c + 1) * LANES]


def _layer_norm(z, g, b):
    mu = jnp.mean(z, axis=-1, keepdims=True)
    zc = z - mu
    var = jnp.mean(zc * zc, axis=-1, keepdims=True)
    return zc * lax.rsqrt(var + LN_EPS) * g + b


def _mix_kernel(o_ref, y_ref, ga_ref, gb_ref, x_ref, wao_ref, wro_ref, wo_ref, g1_ref, b1_ref,
                wrt_ref, br_ref, cnt0_ref, x1_ref, tope_ref, gate_ref, rank_ref, cnt_ref, run_s,
                *, alpha, tm):
    @pl.when(pl.program_id(0) == 0)
    def _():
        run_s[...] = cnt0_ref[...]

    att = jnp.dot(o_ref[...], wao_ref[...], preferred_element_type=F32)
    rn = jnp.dot(y_ref[...], wro_ref[...], preferred_element_type=F32)
    merged = jax.nn.sigmoid(ga_ref[...]) * att + jax.nn.sigmoid(gb_ref[...]) * rn
    mixed = jnp.dot(merged.astype(BF16), wo_ref[...], preferred_element_type=F32)
    x1 = _layer_norm(alpha * x_ref[...] + mixed, g1_ref[...], b1_ref[...])
    _store_token_tiles(x1_ref, x1, tm)

    logits = lax.dot_general(wrt_ref[...], x1.astype(BF16), (((1,), (1,)), ((), ())),
                             preferred_element_type=F32) + br_ref[...]
    eidx = lax.broadcasted_iota(jnp.int32, (N_EXPERTS, tm), 0)
    vals, idxs, hots = [], [], []
    for _ in range(TOP_K):
        m = jnp.max(logits, axis=0, keepdims=True)
        idx = jnp.min(jnp.where(logits == m, eidx, N_EXPERTS), axis=0, keepdims=True)
        hot = eidx == idx
        logits = jnp.where(hot, -jnp.inf, logits)
        vals.append(m)
        idxs.append(idx)
        hots.append(hot)
    exps = [jnp.exp(v - vals[0]) for v in vals]
    total = exps[0] + exps[1] + exps[2] + exps[3]
    gate_ref[...] = jnp.concatenate([e / total for e in exps], axis=0)
    tope_ref[...] = jnp.concatenate(idxs, axis=0)

    assigned = sum(h.astype(F32) for h in hots)
    earlier = (lax.broadcasted_iota(jnp.int32, (tm, tm), 0)
               < lax.broadcasted_iota(jnp.int32, (tm, tm), 1)).astype(BF16)
    before = jnp.dot(assigned.astype(BF16), earlier, preferred_element_type=F32) + run_s[...]
    ranks = [jnp.sum(jnp.where(h, before, 0.0), axis=0, keepdims=True) for h in hots]
    rank_ref[...] = jnp.concatenate(ranks, axis=0).astype(jnp.int32)
    run_s[...] = run_s[...] + jnp.sum(assigned, axis=1, keepdims=True)
    cnt_ref[...] = run_s[...]


def _mix(o_attn, y_rnn, ga, gb, x, p, cnt0, alpha):
    n = x.shape[0]
    tm = min(ROW_TILE, n)
    row = lambda w: pl.BlockSpec((tm, w), lambda i: (i, 0))
    tok = pl.BlockSpec((TOP_K, tm), lambda i: (0, i))
    vec = _const_spec((1, D_MODEL))
    cnt = _const_spec((N_EXPERTS, 1))
    return pl.pallas_call(
        functools.partial(_mix_kernel, alpha=alpha, tm=tm),
        grid=(n // tm,),
        in_specs=[row(ATTN_W), row(D_RNN), row(D_MODEL), row(D_MODEL), row(D_MODEL),
                  _const_spec((ATTN_W, D_MODEL)), _const_spec((D_RNN, D_MODEL)),
                  _const_spec((D_MODEL, D_MODEL)), vec, vec,
                  _const_spec((N_EXPERTS, D_MODEL)), cnt, cnt],
        out_specs=[pl.BlockSpec((tm * LANE_GROUPS, LANES), lambda i: (i, 0)), tok, tok, tok, cnt],
        out_shape=[jax.ShapeDtypeStruct((n * LANE_GROUPS, LANES), F32),
                   jax.ShapeDtypeStruct((TOP_K, n), jnp.int32),
                   jax.ShapeDtypeStruct((TOP_K, n), F32),
                   jax.ShapeDtypeStruct((TOP_K, n), jnp.int32),
                   jax.ShapeDtypeStruct((N_EXPERTS, 1), F32)],
        scratch_shapes=[pltpu.VMEM((N_EXPERTS, 1), F32)],
        compiler_params=_params(("arbitrary",), 48),
        name="mix_norm_router",
    )(o_attn, y_rnn, ga, gb, x, p['w_attn_out'], p['w_rnn_out'], p['w_out'], p['ln1_g'],
      p['ln1_b'], p['w_router_t'], p['b_router'], cnt0)


def _tile_copy(src, src_tok, dst, dst_tok, sem):
    rows = lambda t: pl.ds(pl.multiple_of(t * LANE_GROUPS, LANE_GROUPS), LANE_GROUPS)
    return pltpu.make_async_copy(src.at[rows(src_tok)], dst.at[rows(dst_tok)], sem)


_start_copy = lambda cp, k: cp.start(priority=k % 2)
_wait_copy = lambda cp, k: cp.wait()


def _dispatch_kernel(dest_ref, x_ref, xs_in_ref, xs_ref, stage, sems, *, n, tm):
    del xs_in_ref
    i = pl.program_id(0)
    last = pl.num_programs(0) - 1
    slot = i % 2

    def copies(step, s, act, unroll=1):
        base = step * tm

        def body(r, c):
            for k in range(TOP_K):
                act(_tile_copy(stage.at[s], r, xs_ref, dest_ref[k * n + base + r], sems.at[s]), k)
            return c
        lax.fori_loop(0, tm, body, 0, unroll=unroll)

    @pl.when(i >= 2)
    def _():
        copies(i - 2, slot, _wait_copy)

    stage[slot] = x_ref[...]
    copies(i, slot, _start_copy, ISSUE_UNROLL)

    @pl.when(i == last)
    def _():
        @pl.when(i >= 1)
        def _():
            copies(i - 1, 1 - slot, _wait_copy)
        copies(i, slot, _wait_copy)


def _dispatch(x1_tiles, dest, xs):
    n = x1_tiles.shape[0] // LANE_GROUPS
    tm = min(GATHER_TILE, n)
    return pl.pallas_call(
        functools.partial(_dispatch_kernel, n=n, tm=tm),
        grid_spec=pltpu.PrefetchScalarGridSpec(
            num_scalar_prefetch=1,
            grid=(n // tm,),
            in_specs=[pl.BlockSpec((tm * LANE_GROUPS, LANES), lambda i, d: (i, 0)),
                      pl.BlockSpec(memory_space=pl.ANY)],
            out_specs=pl.BlockSpec(memory_space=pl.ANY),
            scratch_shapes=[pltpu.VMEM((2, tm * LANE_GROUPS, LANES), F32),
                            pltpu.SemaphoreType.DMA((2,))]),
        out_shape=jax.ShapeDtypeStruct(xs.shape, xs.dtype),
        input_output_aliases={2: 0},
        compiler_params=_params(("arbitrary",), 32),
        name="moe_dispatch",
    )(dest, x1_tiles, xs)


def _expert_kernel(be_ref, nused_ref, x_ref, wu_ref, bu_ref, wd_ref, bd_ref, y_ref, wu_s, wd_s):
    i = pl.program_id(0)

    @pl.when(i < nused_ref[0])
    def _():
        prev = be_ref[jnp.maximum(i - 1, 0)]

        @pl.when(jnp.logical_or(i == 0, be_ref[i] != prev))
        def _():
            wu_s[...] = wu_ref[0, 0].astype(BF16)
            wd_s[...] = wd_ref[0, 0].astype(BF16)

        xb = jnp.concatenate([c.astype(BF16) for c in _load_token_tiles(x_ref, MOE_BLOCK)], axis=1)
        h = jnp.dot(xb, wu_s[...], preferred_element_type=F32) + bu_ref[0, 0]
        glu = jnp.minimum(h[:, :D_FF], SWIGLU_LIMIT)
        lin = jnp.clip(h[:, D_FF:], -SWIGLU_LIMIT, SWIGLU_LIMIT)
        act = glu * jax.nn.sigmoid(SWIGLU_ALPHA * glu) * (lin + 1.0)
        y = jnp.dot(act.astype(BF16), wd_s[...], preferred_element_type=F32) + bd_ref[0, 0]
        _store_token_tiles(y_ref, y, MOE_BLOCK)

    @pl.when(i >= nused_ref[0])
    def _():
        y_ref[...] = jnp.zeros_like(y_ref)


def _experts(xs, block_e, nused, layer, w_up, b_up, w_down, b_down):
    rows = xs.shape[0] // LANE_GROUPS
    nblk = rows // MOE_BLOCK
    depth = w_up.shape[0]
    tile_blk = (MOE_BLOCK * LANE_GROUPS, LANES)
    blk = lambda i, be, nu: (jnp.minimum(i, nu[0] - 1), 0)
    out_blk = lambda i, be, nu: (i, 0)
    exp3 = lambda i, be, nu: (layer, be[i], 0, 0)
    return pl.pallas_call(
        _expert_kernel,
        grid_spec=pltpu.PrefetchScalarGridSpec(
            num_scalar_prefetch=2,
            grid=(nblk,),
            in_specs=[pl.BlockSpec(tile_blk, blk),
                      pl.BlockSpec((1, 1, D_MODEL, 2 * D_FF), exp3),
                      pl.BlockSpec((1, 1, 1, 2 * D_FF), exp3),
                      pl.BlockSpec((1, 1, D_FF, D_MODEL), exp3),
                      pl.BlockSpec((1, 1, 1, D_MODEL), exp3)],
            out_specs=pl.BlockSpec(tile_blk, out_blk),
            scratch_shapes=[pltpu.VMEM((D_MODEL, 2 * D_FF), BF16),
                            pltpu.VMEM((D_FF, D_MODEL), BF16)]),
        out_shape=jax.ShapeDtypeStruct(xs.shape, F32),
        compiler_params=_params(("arbitrary",), 56),
        name="moe_experts",
    )(block_e, nused, xs, w_up, b_up.reshape(depth, N_EXPERTS, 1, 2 * D_FF), w_down,
      b_down.reshape(depth, N_EXPERTS, 1, D_MODEL))


def _combine_kernel(dest_ref, yb_ref, x1_ref, gate_ref, g2_ref, b2_ref, x2_ref, buf, sems,
                    *, n, tm, alpha):
    i = pl.program_id(0)
    slot = i % 2

    def copies(step, buf_slot, act, unroll=1):
        base = step * tm

        def body(r, c):
            for k in range(TOP_K):
                act(_tile_copy(yb_ref, dest_ref[k * n + base + r], buf.at[buf_slot, k], r,
                               sems.at[buf_slot]), k)
            return c
        lax.fori_loop(0, tm, body, 0, unroll=unroll)

    @pl.when(i == 0)
    def _():
        copies(0, 0, _start_copy, ISSUE_UNROLL)

    @pl.when(i + 1 < pl.num_programs(0))
    def _():
        copies(i + 1, 1 - slot, _start_copy, ISSUE_UNROLL)

    copies(i, slot, _wait_copy)
    gate = gate_ref[...]
    x1 = _load_token_tiles(x1_ref, tm)
    parts = [_load_token_tiles(buf, tm, lead=(slot, k)) for k in range(TOP_K)]
    z = []
    for c in range(LANE_GROUPS):
        y = parts[0][c] * gate[:, 0:1]
        for k in range(1, TOP_K):
            y = y + parts[k][c] * gate[:, k:k + 1]
        z.append(alpha * x1[c] + y)
    x2_ref[...] = _layer_norm(jnp.concatenate(z, axis=1), g2_ref[...], b2_ref[...])


def _combine(yb, x1_tiles, dest, gate_rows, ln_g, ln_b, alpha):
    n = x1_tiles.shape[0] // LANE_GROUPS
    tm = min(GATHER_TILE, n)
    return pl.pallas_call(
        functools.partial(_combine_kernel, n=n, tm=tm, alpha=alpha),
        grid_spec=pltpu.PrefetchScalarGridSpec(
            num_scalar_prefetch=1,
            grid=(n // tm,),
            in_specs=[pl.BlockSpec(memory_space=pl.ANY),
                      pl.BlockSpec((tm * LANE_GROUPS, LANES), lambda i, d: (i, 0)),
                      pl.BlockSpec((tm, TOP_K), lambda i, d: (i, 0)),
                      pl.BlockSpec((1, D_MODEL), lambda i, d: (0, 0)),
                      pl.BlockSpec((1, D_MODEL), lambda i, d: (0, 0))],
            out_specs=pl.BlockSpec((tm, D_MODEL), lambda i, d: (i, 0)),
            scratch_shapes=[pltpu.VMEM((2, TOP_K, tm * LANE_GROUPS, LANES), F32),
                            pltpu.SemaphoreType.DMA((2,))]),
        out_shape=jax.ShapeDtypeStruct((n, D_MODEL), F32),
        compiler_params=_params(("arbitrary",), 32),
        name="moe_combine",
    )(dest, yb, x1_tiles, gate_rows, ln_g, ln_b)


def _moe_layout(counts, groups):
    n_assign = sum(g[0].shape[1] for g in groups) * TOP_K
    nblk = -(-(n_assign + N_EXPERTS * (MOE_BLOCK - 1)) // MOE_BLOCK)
    padded = (counts + MOE_BLOCK - 1) // MOE_BLOCK * MOE_BLOCK
    pad_end = jnp.cumsum(padded)
    pad_start = pad_end - padded
    nused = pad_end[-1] // MOE_BLOCK

    def expert_of(row):
        ended = (pad_end[None, :] <= row[:, None]).astype(jnp.int32)
        return jnp.minimum(jnp.sum(ended, axis=1), N_EXPERTS - 1)

    blk = jnp.arange(nblk, dtype=jnp.int32)
    block_e = expert_of(jnp.minimum(blk, nused - 1) * MOE_BLOCK)
    eids = jnp.arange(N_EXPERTS, dtype=jnp.int32)
    dests = [(jnp.sum(jnp.where(top_e[..., None] == eids, pad_start, 0), axis=-1) + rank)
             .reshape(-1).astype(jnp.int32) for top_e, rank in groups]
    return nblk, block_e, nused.astype(jnp.int32).reshape(1), dests


def _temporal(x, p, attend, conv0, h0, nb, cnt0, alpha):
    q, kv, xr, yr, ga, gb = _project(x, p['w_in'])
    o_attn = attend(q, kv)
    y_rnn, conv_new, h_new = _rnn_branch(xr, yr, p, conv0, h0, nb)
    x1, top_e, gate, rank, cnt = _mix(o_attn, y_rnn, ga, gb, x, p, cnt0, alpha)
    return x1, top_e, gate, rank, cnt, kv, conv_new, h_new


def kernel(x_prompt, x_sample, cache_k, cache_v, state_conv, state_h, w_in, attn_sinks, conv_w,
           conv_b, rg_wa, rg_ba, rg_wx, rg_bx, rg_lambda, w_attn_out, w_rnn_out, w_out, ln1_g,
           ln1_b, w_router, b_router, w_up, b_up, w_down, b_down, ln2_g, ln2_b):
    depth = w_in.shape[0]
    alpha = float((2 * depth) ** 0.25)
    bp, seq, _ = x_prompt.shape
    bs = x_sample.shape[0]
    cw = cache_k.shape[2]
    halo = CONV_W - 1

    xp = x_prompt.transpose(1, 0, 2).reshape(seq * bp, D_MODEL)
    xs = x_sample.reshape(bs, D_MODEL)
    outs = [[] for _ in range(8)]
    zeros_conv = jnp.zeros((halo * bp, D_RNN), F32)
    zeros_h = jnp.zeros((bp, D_RNN), F32)
    zeros_cnt = jnp.zeros((N_EXPERTS, 1), F32)

    for l in range(depth):
        p = {
            'w_in': w_in[l].astype(BF16),
            'conv_w': conv_w[l], 'conv_b': conv_b[l].reshape(1, D_RNN),
            'rg_wa': rg_wa[l].astype(BF16), 'rg_ba': rg_ba[l].reshape(1, D_RNN),
            'rg_wx': rg_wx[l].astype(BF16), 'rg_bx': rg_bx[l].reshape(1, D_RNN),
            'rg_lambda': rg_lambda[l].reshape(1, D_RNN),
            'w_attn_out': w_attn_out[l].astype(BF16), 'w_rnn_out': w_rnn_out[l].astype(BF16),
            'w_out': w_out[l].astype(BF16),
            'ln1_g': ln1_g[l].reshape(1, D_MODEL), 'ln1_b': ln1_b[l].reshape(1, D_MODEL),
            'w_router_t': w_router[l].T.astype(BF16),
            'b_router': b_router[l].reshape(N_EXPERTS, 1),
        }
        sinks = attn_sinks[l]
        g2, b2 = ln2_g[l].reshape(1, D_MODEL), ln2_b[l].reshape(1, D_MODEL)

        x1p, te_p, gt_p, rk_p, cnt_p, kv_p, conv_p, h_p = _temporal(
            xp, p, lambda q, kv: _prompt_attention(q, kv, sinks, bp, seq),
            zeros_conv, zeros_h, bp, zeros_cnt, alpha)
        conv0_s = state_conv[l].transpose(1, 0, 2).reshape(halo * bs, D_RNN)
        x1s, te_s, gt_s, rk_s, cnt_s, kv_s, conv_s, h_s = _temporal(
            xs, p, lambda q, kv: _sample_attention(q, kv, sinks, cache_k[l], cache_v[l]),
            conv0_s, state_h[l], bs, cnt_p, alpha)

        counts = cnt_s.reshape(N_EXPERTS).astype(jnp.int32)
        nblk, block_e, nused, (dest_p, dest_s) = _moe_layout(counts, [(te_p, rk_p), (te_s, rk_s)])
        sorted_x = jnp.zeros((nblk * MOE_BLOCK * LANE_GROUPS, LANES), F32)
        sorted_x = _dispatch(x1p, dest_p, sorted_x)
        sorted_x = _dispatch(x1s, dest_s, sorted_x)
        yb = _experts(sorted_x, block_e, nused, l, w_up, b_up, w_down, b_down)
        xp = _combine(yb, x1p, dest_p, gt_p.T, g2, b2, alpha)
        xs = _combine(yb, x1s, dest_s, gt_s.T, g2, b2, alpha)

        kv_p4 = kv_p.reshape(seq, bp, 2, N_KV_HEADS, HEAD_DIM)[seq - cw:]
        outs[0].append(kv_p4[:, :, 0].transpose(1, 0, 2, 3))
        outs[1].append(kv_p4[:, :, 1].transpose(1, 0, 2, 3))
        outs[2].append(conv_p.reshape(halo, bp, D_RNN).transpose(1, 0, 2))
        outs[3].append(h_p)
        kv_s4 = kv_s.reshape(bs, 1, 2, N_KV_HEADS, HEAD_DIM)
        outs[4].append(jnp.concatenate([cache_k[l], kv_s4[:, :, 0]], axis=1)[:, -cw:])
        outs[5].append(jnp.concatenate([cache_v[l], kv_s4[:, :, 1]], axis=1)[:, -cw:])
        outs[6].append(conv_s.reshape(halo, bs, D_RNN).transpose(1, 0, 2))
        outs[7].append(h_s)

    y_prompt = xp.reshape(seq, bp, D_MODEL).transpose(1, 0, 2)
    y_sample = xs.reshape(bs, 1, D_MODEL)
    return (y_prompt, y_sample) + tuple(jnp.stack(o) for o in outs)
```

```python
import functools

import numpy as np
import jax
import jax.numpy as jnp
from jax import lax
from jax.experimental import pallas as pl
from jax.experimental.pallas import tpu as pltpu

F32 = jnp.float32
BF16 = jnp.bfloat16

D_MODEL = 1024
N_HEADS = 16
N_KV_HEADS = 2
HEAD_DIM = 64
GROUP = N_HEADS // N_KV_HEADS
WINDOW = 128
ATTN_W = N_HEADS * HEAD_DIM
KV_W = N_KV_HEADS * HEAD_DIM
D_RNN = 1280
RNN_BLOCK = 128
N_RNN_BLOCKS = D_RNN // RNN_BLOCK
CONV_W = 4
LRU_C = 8.0
N_EXPERTS = 32
TOP_K = 4
D_FF = 1024
SWIGLU_LIMIT = 7.0
SWIGLU_ALPHA = 1.702
LN_EPS = 1e-5
PAST_LEN = 8192
PROJ_OFFS = (0, ATTN_W, ATTN_W + 2 * KV_W, ATTN_W + 2 * KV_W + D_RNN,
             ATTN_W + 2 * KV_W + 2 * D_RNN, ATTN_W + 2 * KV_W + 2 * D_RNN + D_MODEL,
             ATTN_W + 2 * KV_W + 2 * D_RNN + 2 * D_MODEL)
PROJ_W = PROJ_OFFS[-1]
NEG_BIG = -1e30
LANES = 128
LANE_GROUPS = D_MODEL // LANES

ROW_TILE = 256
RNN_TIME_TILE = 64
MOE_BLOCK = 512
GATHER_TILE = 128
ISSUE_UNROLL = 4
SAMPLE_ATTN_TILE = 32
MIB = 1 << 20


def _alibi_slopes():
    h = np.arange(1, N_HEADS + 1, dtype=np.float32)
    return (np.float32(2.0) ** (np.float32(-8.0) * h / np.float32(N_HEADS))).astype(np.float32)


def _params(semantics, vmem_mib):
    return pltpu.CompilerParams(dimension_semantics=semantics, vmem_limit_bytes=vmem_mib * MIB)


def _const_spec(shape):
    nd = len(shape)
    return pl.BlockSpec(shape, lambda *_: (0,) * nd)


def _proj_kernel(x_ref, w_ref, q_ref, kv_ref, xr_ref, yr_ref, ga_ref, gb_ref):
    xb = x_ref[...].astype(BF16)
    outs = (q_ref, kv_ref, xr_ref, yr_ref, ga_ref, gb_ref)
    for n, o_ref in enumerate(outs):
        w = w_ref[:, PROJ_OFFS[n]:PROJ_OFFS[n + 1]]
        res = jnp.dot(xb, w, preferred_element_type=F32)
        if len(o_ref.shape) == 3:
            for c in range(o_ref.shape[0]):
                o_ref[c] = res[:, c * LANES:(c + 1) * LANES]
        else:
            o_ref[...] = res


def _project(x, w_in_bf):
    n = x.shape[0]
    tm = min(ROW_TILE, n)
    widths = [PROJ_OFFS[i + 1] - PROJ_OFFS[i] for i in range(6)]
    row = lambda w: (pl.BlockSpec((tm, w), lambda i: (i, 0)), jax.ShapeDtypeStruct((n, w), F32))
    stack = lambda w: (pl.BlockSpec((w // LANES, tm, LANES), lambda i: (0, i, 0)),
                       jax.ShapeDtypeStruct((w // LANES, n, LANES), F32))
    outs = [stack(widths[0]), stack(widths[1])] + [row(w) for w in widths[2:]]
    return pl.pallas_call(
        _proj_kernel,
        grid=(n // tm,),
        in_specs=[pl.BlockSpec((tm, D_MODEL), lambda i: (i, 0)),
                  _const_spec((D_MODEL, PROJ_W))],
        out_specs=[o[0] for o in outs],
        out_shape=[o[1] for o in outs],
        compiler_params=_params(("parallel",), 56),
        name="in_proj",
    )(x, w_in_bf)


def _prompt_attn_kernel(sink_ref, q_ref, kvp_ref, kvc_ref, o_ref, *, nb):
    tb = pl.program_id(0)
    seq_rows = pl.ds(pl.program_id(1), WINDOW, stride=nb)
    kcat = jnp.concatenate([kvp_ref[0, seq_rows, :], kvc_ref[0, seq_rows, :]], axis=0).astype(BF16)
    vcat = jnp.concatenate([kvp_ref[1, seq_rows, :], kvc_ref[1, seq_rows, :]], axis=0).astype(BF16)
    qi = lax.broadcasted_iota(jnp.int32, (WINDOW, 2 * WINDOW), 0)
    kj = lax.broadcasted_iota(jnp.int32, (WINDOW, 2 * WINDOW), 1)
    dist = qi - kj + WINDOW
    first_ok = jnp.where(kj >= WINDOW, 1, jnp.where(tb > 0, 1, 0))
    ok = jnp.where(dist >= 0, jnp.where(dist < WINDOW, first_ok, 0), 0)
    neg_dist = jnp.where(ok > 0, -dist.astype(F32), NEG_BIG)
    slopes = _alibi_slopes()
    scale = HEAD_DIM ** -0.5
    heads_per_chunk = LANES // HEAD_DIM
    for c in range(ATTN_W // LANES):
        qc = q_ref[c, seq_rows, :].astype(BF16)
        outs = []
        for j in range(heads_per_chunk):
            h = c * heads_per_chunk + j
            g = h // GROUP
            qh = qc[:, j * HEAD_DIM:(j + 1) * HEAD_DIM]
            kg = kcat[:, g * HEAD_DIM:(g + 1) * HEAD_DIM]
            vg = vcat[:, g * HEAD_DIM:(g + 1) * HEAD_DIM]
            s = lax.dot_general(qh, kg, (((1,), (1,)), ((), ())), preferred_element_type=F32)
            logits = s * scale + float(slopes[h]) * neg_dist
            sink = sink_ref[h]
            m = jnp.maximum(jnp.max(logits, axis=1, keepdims=True), sink)
            p = jnp.exp(logits - m)
            denom = jnp.sum(p, axis=1, keepdims=True) + jnp.exp(sink - m)
            outs.append(jnp.dot(p.astype(BF16), vg, preferred_element_type=F32) / denom)
        o_ref[c, seq_rows, :] = jnp.concatenate(outs, axis=1)


def _prompt_attention(q, kv, sinks, batch, seq):
    nblk = seq // WINDOW
    rows = WINDOW * batch
    cur = lambda t, b: (0, t, 0)
    prev = lambda t, b: (0, jnp.maximum(t - 1, 0), 0)
    return pl.pallas_call(
        functools.partial(_prompt_attn_kernel, nb=batch),
        grid=(nblk, batch),
        in_specs=[pl.BlockSpec(memory_space=pltpu.SMEM),
                  pl.BlockSpec((q.shape[0], rows, LANES), cur),
                  pl.BlockSpec((kv.shape[0], rows, LANES), prev),
                  pl.BlockSpec((kv.shape[0], rows, LANES), cur)],
        out_specs=pl.BlockSpec((q.shape[0], rows, LANES), cur),
        out_shape=jax.ShapeDtypeStruct(q.shape, F32),
        compiler_params=_params(("parallel", "arbitrary"), 40),
        name="prompt_attn",
    )(sinks, q, kv, kv)


def _sample_attn_kernel(qz_ref, ck_ref, cv_ref, kn_ref, vn_ref, bias_ref, sink_ref, o_ref):
    qz = qz_ref[...]
    s_c = jnp.einsum('bhd,bjd->bhj', qz, ck_ref[...].astype(BF16),
                     preferred_element_type=F32) * (HEAD_DIM ** -0.5)
    s_n = jnp.sum(qz.astype(F32) * kn_ref[...], axis=-1, keepdims=True) * (HEAD_DIM ** -0.5)
    logits = s_c + bias_ref[...][None]
    sink = sink_ref[...][None]
    m = jnp.maximum(jnp.maximum(jnp.max(logits, axis=-1, keepdims=True), s_n), sink)
    p_c = jnp.exp(logits - m)
    p_n = jnp.exp(s_n - m)
    denom = jnp.sum(p_c, axis=-1, keepdims=True) + p_n + jnp.exp(sink - m)
    o = jnp.einsum('bhj,bjd->bhd', p_c.astype(BF16), cv_ref[...].astype(BF16),
                   preferred_element_type=F32)
    o_ref[...] = (o + p_n * vn_ref[...]) / denom


def _sample_attention(q, kv, sinks, ck, cv):
    b, cw = ck.shape[0], ck.shape[1]
    bt = min(SAMPLE_ATTN_TILE, b)
    q = q.transpose(1, 0, 2).reshape(b, ATTN_W).astype(BF16)
    kv = kv.transpose(1, 0, 2).reshape(b, 2 * KV_W)
    q4 = q.reshape(b, N_KV_HEADS, GROUP, HEAD_DIM)
    eye = jnp.eye(N_KV_HEADS, dtype=q.dtype)
    qz = (q4[:, :, :, None, :] * eye[None, :, None, :, None]).reshape(b, N_HEADS, KV_W)
    kn = kv[:, :KV_W].reshape(b, 1, KV_W)
    vn = kv[:, KV_W:].reshape(b, 1, KV_W)
    dist = (cw - np.arange(cw)).astype(np.float32)
    bias = np.where(dist[None, :] < WINDOW, -_alibi_slopes()[:, None] * dist[None, :], NEG_BIG)
    o = pl.pallas_call(
        _sample_attn_kernel,
        grid=(b // bt,),
        in_specs=[pl.BlockSpec((bt, N_HEADS, KV_W), lambda i: (i, 0, 0)),
                  pl.BlockSpec((bt, cw, KV_W), lambda i: (i, 0, 0)),
                  pl.BlockSpec((bt, cw, KV_W), lambda i: (i, 0, 0)),
                  pl.BlockSpec((bt, 1, KV_W), lambda i: (i, 0, 0)),
                  pl.BlockSpec((bt, 1, KV_W), lambda i: (i, 0, 0)),
                  _const_spec((N_HEADS, cw)),
                  _const_spec((N_HEADS, 1))],
        out_specs=pl.BlockSpec((bt, N_HEADS, KV_W), lambda i: (i, 0, 0)),
        out_shape=jax.ShapeDtypeStruct((b, N_HEADS, KV_W), F32),
        compiler_params=_params(("parallel",), 32),
        name="sample_attn",
    )(qz, ck.reshape(b, cw, KV_W), cv.reshape(b, cw, KV_W), kn, vn,
      jnp.asarray(bias, F32), sinks.reshape(N_HEADS, 1))
    o4 = o.reshape(b, N_KV_HEADS, GROUP, N_KV_HEADS, HEAD_DIM)
    o_sel = jnp.stack([o4[:, g, :, g, :] for g in range(N_KV_HEADS)], axis=1)
    return o_sel.reshape(b, ATTN_W // LANES, LANES).transpose(1, 0, 2)


def _rnn_kernel(xr_ref, yr_ref, cw_ref, cb_ref, wa_ref, ba_ref, wx_ref, bx_ref, lam_ref,
                conv0_ref, h0_ref, y_ref, convo_ref, ho_ref, xp_s, a_s, b_s, h_s, *, nb, tt):
    rows = tt * nb
    halo = (CONV_W - 1) * nb

    @pl.when(pl.program_id(0) == 0)
    def _():
        xp_s[0:halo] = conv0_ref[...]
        h_s[...] = h0_ref[...]

    xp_s[halo:halo + rows] = xr_ref[...]
    nl = -lam_ref[...]
    coef = -LRU_C * (jnp.maximum(nl, 0.0) + jnp.log1p(jnp.exp(-jnp.abs(nl))))
    for n in range(N_RNN_BLOCKS):
        sl = slice(n * RNN_BLOCK, (n + 1) * RNN_BLOCK)
        xc = cb_ref[:, sl] + xp_s[0:rows, sl] * cw_ref[0:1, sl]
        for j in range(1, CONV_W):
            xc = xc + xp_s[j * nb:j * nb + rows, sl] * cw_ref[j:j + 1, sl]
        xcb = xc.astype(BF16)
        r = jax.nn.sigmoid(jnp.dot(xcb, wa_ref[n], preferred_element_type=F32) + ba_ref[:, sl])
        ig = jax.nn.sigmoid(jnp.dot(xcb, wx_ref[n], preferred_element_type=F32) + bx_ref[:, sl])
        log_a = coef[:, sl] * r
        a = jnp.exp(log_a)
        a_s[:, sl] = a
        b_s[:, sl] = jnp.sqrt(1.0 - a * a) * (ig * xc)

    def step(t, h):
        off = pl.multiple_of(t * nb, nb)
        h = a_s[pl.ds(off, nb), :] * h + b_s[pl.ds(off, nb), :]
        a_s[pl.ds(off, nb), :] = h
        return h

    h = lax.fori_loop(0, tt, step, h_s[...])
    h_s[...] = h
    yr = yr_ref[...]
    gelu = 0.5 * yr * (1.0 + jnp.tanh(np.float32(np.sqrt(2.0 / np.pi))
                                      * (yr + np.float32(0.044715) * (yr * yr * yr))))
    y_ref[...] = (a_s[...] * gelu).astype(y_ref.dtype)
    tail = xp_s[rows:rows + halo]
    convo_ref[...] = tail
    ho_ref[...] = h
    xp_s[0:halo] = tail


def _rnn_branch(xr, yr, p, conv0, h0, nb):
    rows_total = xr.shape[0]
    t_total = rows_total // nb
    tt = min(RNN_TIME_TILE, t_total)
    rows = tt * nb
    halo = (CONV_W - 1) * nb
    row_spec = pl.BlockSpec((rows, D_RNN), lambda i: (i, 0))
    vec = _const_spec((1, D_RNN))
    return pl.pallas_call(
        functools.partial(_rnn_kernel, nb=nb, tt=tt),
        grid=(t_total // tt,),
        in_specs=[row_spec, row_spec, _const_spec((CONV_W, D_RNN)), vec,
                  _const_spec((N_RNN_BLOCKS, RNN_BLOCK, RNN_BLOCK)), vec,
                  _const_spec((N_RNN_BLOCKS, RNN_BLOCK, RNN_BLOCK)), vec, vec,
                  _const_spec((halo, D_RNN)), _const_spec((nb, D_RNN))],
        out_specs=[row_spec, _const_spec((halo, D_RNN)), _const_spec((nb, D_RNN))],
        out_shape=[jax.ShapeDtypeStruct((rows_total, D_RNN), BF16),
                   jax.ShapeDtypeStruct((halo, D_RNN), F32),
                   jax.ShapeDtypeStruct((nb, D_RNN), F32)],
        scratch_shapes=[pltpu.VMEM((rows + halo, D_RNN), F32),
                        pltpu.VMEM((rows, D_RNN), F32),
                        pltpu.VMEM((rows, D_RNN), F32),
                        pltpu.VMEM((nb, D_RNN), F32)],
        compiler_params=_params(("arbitrary",), 48),
        name="rnn_branch",
    )(xr, yr, p['conv_w'], p['conv_b'], p['rg_wa'], p['rg_ba'], p['rg_wx'], p['rg_bx'],
      p['rg_lambda'], conv0, h0)


def _load_token_tiles(ref, n, lead=()):
    return [ref[lead + (pl.ds(c, n, stride=LANE_GROUPS), slice(None))] for c in range(LANE_GROUPS)]


def _store_token_tiles(ref, x, n):
    for c in range(LANE_GROUPS):
        ref[pl.ds(c, n, stride=LANE_GROUPS), :] = x[:, c * LANES:(c + 1) * LANES]


def _layer_norm(z, g, b):
    mu = jnp.mean(z, axis=-1, keepdims=True)
    zc = z - mu
    var = jnp.mean(zc * zc, axis=-1, keepdims=True)
    return zc * lax.rsqrt(var + LN_EPS) * g + b


def _mix_kernel(o_ref, y_ref, ga_ref, gb_ref, x_ref, wao_ref, wro_ref, wo_ref, g1_ref, b1_ref,
                wrt_ref, br_ref, cnt0_ref, x1_ref, tope_ref, gate_ref, rank_ref, cnt_ref, run_s,
                *, alpha, tm):
    @pl.when(pl.program_id(0) == 0)
    def _():
        run_s[...] = cnt0_ref[...]

    o_attn = jnp.concatenate([o_ref[c].astype(BF16) for c in range(ATTN_W // LANES)], axis=1)
    att = jnp.dot(o_attn, wao_ref[...], preferred_element_type=F32)
    rn = jnp.dot(y_ref[...], wro_ref[...], preferred_element_type=F32)
    merged = jax.nn.sigmoid(ga_ref[...]) * att + jax.nn.sigmoid(gb_ref[...]) * rn
    mixed = jnp.dot(merged.astype(BF16), wo_ref[...], preferred_element_type=F32)
    x1 = _layer_norm(alpha * x_ref[...] + mixed, g1_ref[...], b1_ref[...])
    _store_token_tiles(x1_ref, x1, tm)

    logits = lax.dot_general(wrt_ref[...], x1.astype(BF16), (((1,), (1,)), ((), ())),
                             preferred_element_type=F32) + br_ref[...]
    eidx = lax.broadcasted_iota(jnp.int32, (N_EXPERTS, tm), 0)
    vals, idxs, hots = [], [], []
    for _ in range(TOP_K):
        m = jnp.max(logits, axis=0, keepdims=True)
        idx = jnp.min(jnp.where(logits == m, eidx, N_EXPERTS), axis=0, keepdims=True)
        hot = eidx == idx
        logits = jnp.where(hot, -jnp.inf, logits)
        vals.append(m)
        idxs.append(idx)
        hots.append(hot)
    exps = [jnp.exp(v - vals[0]) for v in vals]
    total = exps[0] + exps[1] + exps[2] + exps[3]
    gate_ref[...] = jnp.concatenate([e / total for e in exps], axis=0)
    tope_ref[...] = jnp.concatenate(idxs, axis=0)

    assigned = sum(h.astype(F32) for h in hots)
    earlier = (lax.broadcasted_iota(jnp.int32, (tm, tm), 0)
               < lax.broadcasted_iota(jnp.int32, (tm, tm), 1)).astype(BF16)
    before = jnp.dot(assigned.astype(BF16), earlier, preferred_element_type=F32) + run_s[...]
    ranks = [jnp.sum(jnp.where(h, before, 0.0), axis=0, keepdims=True) for h in hots]
    rank_ref[...] = jnp.concatenate(ranks, axis=0).astype(jnp.int32)
    run_s[...] = run_s[...] + jnp.sum(assigned, axis=1, keepdims=True)
    cnt_ref[...] = run_s[...]


def _mix(o_attn, y_rnn, ga, gb, x, p, cnt0, alpha):
    n = x.shape[0]
    tm = min(ROW_TILE, n)
    row = lambda w: pl.BlockSpec((tm, w), lambda i: (i, 0))
    tok = pl.BlockSpec((TOP_K, tm), lambda i: (0, i))
    vec = _const_spec((1, D_MODEL))
    cnt = _const_spec((N_EXPERTS, 1))
    return pl.pallas_call(
        functools.partial(_mix_kernel, alpha=alpha, tm=tm),
        grid=(n // tm,),
        in_specs=[pl.BlockSpec((ATTN_W // LANES, tm, LANES), lambda i: (0, i, 0)),
                  row(D_RNN), row(D_MODEL), row(D_MODEL), row(D_MODEL),
                  _const_spec((ATTN_W, D_MODEL)), _const_spec((D_RNN, D_MODEL)),
                  _const_spec((D_MODEL, D_MODEL)), vec, vec,
                  _const_spec((N_EXPERTS, D_MODEL)), cnt, cnt],
        out_specs=[pl.BlockSpec((tm * LANE_GROUPS, LANES), lambda i: (i, 0)), tok, tok, tok, cnt],
        out_shape=[jax.ShapeDtypeStruct((n * LANE_GROUPS, LANES), F32),
                   jax.ShapeDtypeStruct((TOP_K, n), jnp.int32),
                   jax.ShapeDtypeStruct((TOP_K, n), F32),
                   jax.ShapeDtypeStruct((TOP_K, n), jnp.int32),
                   jax.ShapeDtypeStruct((N_EXPERTS, 1), F32)],
        scratch_shapes=[pltpu.VMEM((N_EXPERTS, 1), F32)],
        compiler_params=_params(("arbitrary",), 48),
        name="mix_norm_router",
    )(o_attn, y_rnn, ga, gb, x, p['w_attn_out'], p['w_rnn_out'], p['w_out'], p['ln1_g'],
      p['ln1_b'], p['w_router_t'], p['b_router'], cnt0)


def _tile_copy(src, src_tok, dst, dst_tok, sem):
    rows = lambda t: pl.ds(pl.multiple_of(t * LANE_GROUPS, LANE_GROUPS), LANE_GROUPS)
    return pltpu.make_async_copy(src.at[rows(src_tok)], dst.at[rows(dst_tok)], sem)


_start_copy = lambda cp, k: cp.start(priority=k % 2)
_wait_copy = lambda cp, k: cp.wait()


def _dispatch_kernel(dest_ref, x_ref, xs_in_ref, xs_ref, stage, sems, *, n, tm):
    del xs_in_ref
    i = pl.program_id(0)
    last = pl.num_programs(0) - 1
    slot = i % 2

    def copies(step, s, act, unroll=1):
        base = step * tm

        def body(r, c):
            for k in range(TOP_K):
                act(_tile_copy(stage.at[s], r, xs_ref, dest_ref[k * n + base + r], sems.at[s]), k)
            return c
        lax.fori_loop(0, tm, body, 0, unroll=unroll)

    @pl.when(i >= 2)
    def _():
        copies(i - 2, slot, _wait_copy)

    stage[slot] = x_ref[...]
    copies(i, slot, _start_copy, ISSUE_UNROLL)

    @pl.when(i == last)
    def _():
        @pl.when(i >= 1)
        def _():
            copies(i - 1, 1 - slot, _wait_copy)
        copies(i, slot, _wait_copy)


def _dispatch(x1_tiles, dest, xs):
    n = x1_tiles.shape[0] // LANE_GROUPS
    tm = min(GATHER_TILE, n)
    return pl.pallas_call(
        functools.partial(_dispatch_kernel, n=n, tm=tm),
        grid_spec=pltpu.PrefetchScalarGridSpec(
            num_scalar_prefetch=1,
            grid=(n // tm,),
            in_specs=[pl.BlockSpec((tm * LANE_GROUPS, LANES), lambda i, d: (i, 0)),
                      pl.BlockSpec(memory_space=pl.ANY)],
            out_specs=pl.BlockSpec(memory_space=pl.ANY),
            scratch_shapes=[pltpu.VMEM((2, tm * LANE_GROUPS, LANES), F32),
                            pltpu.SemaphoreType.DMA((2,))]),
        out_shape=jax.ShapeDtypeStruct(xs.shape, xs.dtype),
        input_output_aliases={2: 0},
        compiler_params=_params(("arbitrary",), 32),
        name="moe_dispatch",
    )(dest, x1_tiles, xs)


def _expert_kernel(be_ref, next_ref, nused_ref, x_ref, wu_hbm, bu_ref, wd_hbm, bd_ref, y_ref,
                   wu_f, wd_f, wu_s, wd_s, slot_s, sems, *, layer):
    i = pl.program_id(0)

    def weight_copies(e, slot):
        return (pltpu.make_async_copy(wu_hbm.at[layer, e], wu_f.at[slot], sems.at[slot, 0]),
                pltpu.make_async_copy(wd_hbm.at[layer, e], wd_f.at[slot], sems.at[slot, 1]))

    @pl.when(i < nused_ref[0])
    def _():
        e = be_ref[i]

        @pl.when(i == 0)
        def _():
            slot_s[0] = 0
            for cp in weight_copies(e, 0):
                cp.start()

        @pl.when(jnp.logical_or(i == 0, e != be_ref[jnp.maximum(i - 1, 0)]))
        def _():
            @pl.when(i > 0)
            def _():
                slot_s[0] = 1 - slot_s[0]
            slot = slot_s[0]
            for cp in weight_copies(e, slot):
                cp.wait()
            nxt = next_ref[i]

            @pl.when(nxt >= 0)
            def _():
                for cp in weight_copies(nxt, 1 - slot):
                    cp.start()
            wu_s[...] = wu_f[slot].astype(BF16)
            wd_s[...] = wd_f[slot].astype(BF16)

        xb = jnp.concatenate([c.astype(BF16) for c in _load_token_tiles(x_ref, MOE_BLOCK)], axis=1)
        h = jnp.dot(xb, wu_s[...], preferred_element_type=F32) + bu_ref[0, 0]
        glu = jnp.minimum(h[:, :D_FF], SWIGLU_LIMIT)
        lin = jnp.clip(h[:, D_FF:], -SWIGLU_LIMIT, SWIGLU_LIMIT)
        act = glu * jax.nn.sigmoid(SWIGLU_ALPHA * glu) * (lin + 1.0)
        y = jnp.dot(act.astype(BF16), wd_s[...], preferred_element_type=F32) + bd_ref[0, 0]
        _store_token_tiles(y_ref, y, MOE_BLOCK)

    @pl.when(i >= nused_ref[0])
    def _():
        y_ref[...] = jnp.zeros_like(y_ref)


def _experts(xs, block_e, next_e, nused, layer, w_up, b_up, w_down, b_down):
    rows = xs.shape[0] // LANE_GROUPS
    nblk = rows // MOE_BLOCK
    depth = w_up.shape[0]
    tile_blk = (MOE_BLOCK * LANE_GROUPS, LANES)
    blk = lambda i, be, nx, nu: (jnp.minimum(i, nu[0] - 1), 0)
    out_blk = lambda i, be, nx, nu: (i, 0)
    exp3 = lambda i, be, nx, nu: (layer, be[i], 0, 0)
    return pl.pallas_call(
        functools.partial(_expert_kernel, layer=layer),
        grid_spec=pltpu.PrefetchScalarGridSpec(
            num_scalar_prefetch=3,
            grid=(nblk,),
            in_specs=[pl.BlockSpec(tile_blk, blk),
                      pl.BlockSpec(memory_space=pl.ANY),
                      pl.BlockSpec((1, 1, 1, 2 * D_FF), exp3),
                      pl.BlockSpec(memory_space=pl.ANY),
                      pl.BlockSpec((1, 1, 1, D_MODEL), exp3)],
            out_specs=pl.BlockSpec(tile_blk, out_blk),
            scratch_shapes=[pltpu.VMEM((2, D_MODEL, 2 * D_FF), F32),
                            pltpu.VMEM((2, D_FF, D_MODEL), F32),
                            pltpu.VMEM((D_MODEL, 2 * D_FF), BF16),
                            pltpu.VMEM((D_FF, D_MODEL), BF16),
                            pltpu.SMEM((1,), jnp.int32),
                            pltpu.SemaphoreType.DMA((2, 2))]),
        out_shape=jax.ShapeDtypeStruct(xs.shape, F32),
        compiler_params=_params(("arbitrary",), 56),
        name="moe_experts",
    )(block_e, next_e, nused, xs, w_up, b_up.reshape(depth, N_EXPERTS, 1, 2 * D_FF), w_down,
      b_down.reshape(depth, N_EXPERTS, 1, D_MODEL))


def _combine_kernel(dest_ref, yb_ref, x1_ref, gate_ref, g2_ref, b2_ref, x2_ref, buf, sems,
                    *, n, tm, alpha):
    i = pl.program_id(0)
    slot = i % 2

    def copies(step, buf_slot, act, unroll=1):
        base = step * tm

        def body(r, c):
            for k in range(TOP_K):
                act(_tile_copy(yb_ref, dest_ref[k * n + base + r], buf.at[buf_slot, k], r,
                               sems.at[buf_slot]), k)
            return c
        lax.fori_loop(0, tm, body, 0, unroll=unroll)

    @pl.when(i == 0)
    def _():
        copies(0, 0, _start_copy, ISSUE_UNROLL)

    @pl.when(i + 1 < pl.num_programs(0))
    def _():
        copies(i + 1, 1 - slot, _start_copy, ISSUE_UNROLL)

    copies(i, slot, _wait_copy)
    gate = gate_ref[...]
    x1 = _load_token_tiles(x1_ref, tm)
    parts = [_load_token_tiles(buf, tm, lead=(slot, k)) for k in range(TOP_K)]
    z = []
    for c in range(LANE_GROUPS):
        y = parts[0][c] * gate[:, 0:1]
        for k in range(1, TOP_K):
            y = y + parts[k][c] * gate[:, k:k + 1]
        z.append(alpha * x1[c] + y)
    x2_ref[...] = _layer_norm(jnp.concatenate(z, axis=1), g2_ref[...], b2_ref[...])


def _combine(yb, x1_tiles, dest, gate_rows, ln_g, ln_b, alpha):
    n = x1_tiles.shape[0] // LANE_GROUPS
    tm = min(GATHER_TILE, n)
    return pl.pallas_call(
        functools.partial(_combine_kernel, n=n, tm=tm, alpha=alpha),
        grid_spec=pltpu.PrefetchScalarGridSpec(
            num_scalar_prefetch=1,
            grid=(n // tm,),
            in_specs=[pl.BlockSpec(memory_space=pl.ANY),
                      pl.BlockSpec((tm * LANE_GROUPS, LANES), lambda i, d: (i, 0)),
                      pl.BlockSpec((tm, TOP_K), lambda i, d: (i, 0)),
                      pl.BlockSpec((1, D_MODEL), lambda i, d: (0, 0)),
                      pl.BlockSpec((1, D_MODEL), lambda i, d: (0, 0))],
            out_specs=pl.BlockSpec((tm, D_MODEL), lambda i, d: (i, 0)),
            scratch_shapes=[pltpu.VMEM((2, TOP_K, tm * LANE_GROUPS, LANES), F32),
                            pltpu.SemaphoreType.DMA((2,))]),
        out_shape=jax.ShapeDtypeStruct((n, D_MODEL), F32),
        compiler_params=_params(("arbitrary",), 32),
        name="moe_combine",
    )(dest, yb, x1_tiles, gate_rows, ln_g, ln_b)


def _moe_layout(counts, groups):
    n_assign = sum(g[0].shape[1] for g in groups) * TOP_K
    nblk = -(-(n_assign + N_EXPERTS * (MOE_BLOCK - 1)) // MOE_BLOCK)
    padded = (counts + MOE_BLOCK - 1) // MOE_BLOCK * MOE_BLOCK
    pad_end = jnp.cumsum(padded)
    pad_start = pad_end - padded
    nused = pad_end[-1] // MOE_BLOCK

    def expert_of(row):
        ended = (pad_end[None, :] <= row[:, None]).astype(jnp.int32)
        return jnp.minimum(jnp.sum(ended, axis=1), N_EXPERTS - 1)

    blk = jnp.arange(nblk, dtype=jnp.int32)
    block_e = expert_of(jnp.minimum(blk, nused - 1) * MOE_BLOCK)
    eids = jnp.arange(N_EXPERTS, dtype=jnp.int32)
    lookup = lambda table, idx: jnp.sum(jnp.where(idx[..., None] == eids, table, 0), axis=-1)
    after = lookup(pad_end, block_e)
    next_e = jnp.where(after < pad_end[-1], expert_of(after), -1).astype(jnp.int32)
    dests = [(lookup(pad_start, top_e) + rank).reshape(-1).astype(jnp.int32)
             for top_e, rank in groups]
    return nblk, block_e.astype(jnp.int32), next_e, nused.astype(jnp.int32).reshape(1), dests


def _temporal(x, p, attend, conv0, h0, nb, cnt0, alpha):
    q, kv, xr, yr, ga, gb = _project(x, p['w_in'])
    o_attn = attend(q, kv)
    y_rnn, conv_new, h_new = _rnn_branch(xr, yr, p, conv0, h0, nb)
    x1, top_e, gate, rank, cnt = _mix(o_attn, y_rnn, ga, gb, x, p, cnt0, alpha)
    return x1, top_e, gate, rank, cnt, kv, conv_new, h_new


def kernel(x_prompt, x_sample, cache_k, cache_v, state_conv, state_h, w_in, attn_sinks, conv_w,
           conv_b, rg_wa, rg_ba, rg_wx, rg_bx, rg_lambda, w_attn_out, w_rnn_out, w_out, ln1_g,
           ln1_b, w_router, b_router, w_up, b_up, w_down, b_down, ln2_g, ln2_b):
    depth = w_in.shape[0]
    alpha = float((2 * depth) ** 0.25)
    bp, seq, _ = x_prompt.shape
    bs = x_sample.shape[0]
    cw = cache_k.shape[2]
    halo = CONV_W - 1

    xp = x_prompt.transpose(1, 0, 2).reshape(seq * bp, D_MODEL)
    xs = x_sample.reshape(bs, D_MODEL)
    outs = [[] for _ in range(8)]
    zeros_conv = jnp.zeros((halo * bp, D_RNN), F32)
    zeros_h = jnp.zeros((bp, D_RNN), F32)
    zeros_cnt = jnp.zeros((N_EXPERTS, 1), F32)

    for l in range(depth):
        p = {
            'w_in': w_in[l].astype(BF16),
            'conv_w': conv_w[l], 'conv_b': conv_b[l].reshape(1, D_RNN),
            'rg_wa': rg_wa[l].astype(BF16), 'rg_ba': rg_ba[l].reshape(1, D_RNN),
            'rg_wx': rg_wx[l].astype(BF16), 'rg_bx': rg_bx[l].reshape(1, D_RNN),
            'rg_lambda': rg_lambda[l].reshape(1, D_RNN),
            'w_attn_out': w_attn_out[l].astype(BF16), 'w_rnn_out': w_rnn_out[l].astype(BF16),
            'w_out': w_out[l].astype(BF16),
            'ln1_g': ln1_g[l].reshape(1, D_MODEL), 'ln1_b': ln1_b[l].reshape(1, D_MODEL),
            'w_router_t': w_router[l].T.astype(BF16),
            'b_router': b_router[l].reshape(N_EXPERTS, 1),
        }
        sinks = attn_sinks[l]
        g2, b2 = ln2_g[l].reshape(1, D_MODEL), ln2_b[l].reshape(1, D_MODEL)

        x1p, te_p, gt_p, rk_p, cnt_p, kv_p, conv_p, h_p = _temporal(
            xp, p, lambda q, kv: _prompt_attention(q, kv, sinks, bp, seq),
            zeros_conv, zeros_h, bp, zeros_cnt, alpha)
        conv0_s = state_conv[l].transpose(1, 0, 2).reshape(halo * bs, D_RNN)
        x1s, te_s, gt_s, rk_s, cnt_s, kv_s, conv_s, h_s = _temporal(
            xs, p, lambda q, kv: _sample_attention(q, kv, sinks, cache_k[l], cache_v[l]),
            conv0_s, state_h[l], bs, cnt_p, alpha)

        counts = cnt_s.reshape(N_EXPERTS).astype(jnp.int32)
        nblk, block_e, next_e, nused, (dest_p, dest_s) = _moe_layout(
            counts, [(te_p, rk_p), (te_s, rk_s)])
        sorted_x = jnp.zeros((nblk * MOE_BLOCK * LANE_GROUPS, LANES), F32)
        sorted_x = _dispatch(x1p, dest_p, sorted_x)
        sorted_x = _dispatch(x1s, dest_s, sorted_x)
        yb = _experts(sorted_x, block_e, next_e, nused, l, w_up, b_up, w_down, b_down)
        xp = _combine(yb, x1p, dest_p, gt_p.T, g2, b2, alpha)
        xs = _combine(yb, x1s, dest_s, gt_s.T, g2, b2, alpha)

        kv_p4 = kv_p.reshape(2, seq, bp, N_KV_HEADS, HEAD_DIM)[:, seq - cw:]
        outs[0].append(kv_p4[0].transpose(1, 0, 2, 3))
        outs[1].append(kv_p4[1].transpose(1, 0, 2, 3))
        outs[2].append(conv_p.reshape(halo, bp, D_RNN).transpose(1, 0, 2))
        outs[3].append(h_p)
        kv_s4 = kv_s.reshape(2, bs, 1, N_KV_HEADS, HEAD_DIM)
        outs[4].append(jnp.concatenate([cache_k[l], kv_s4[0]], axis=1)[:, -cw:])
        outs[5].append(jnp.concatenate([cache_v[l], kv_s4[1]], axis=1)[:, -cw:])
        outs[6].append(conv_s.reshape(halo, bs, D_RNN).transpose(1, 0, 2))
        outs[7].append(h_s)

    y_prompt = xp.reshape(seq, bp, D_MODEL).transpose(1, 0, 2)
    y_sample = xs.reshape(bs, 1, D_MODEL)
    return (y_prompt, y_sample) + tuple(jnp.stack(o) for o in outs)
```

```python
import functools

import numpy as np
import jax
import jax.numpy as jnp
from jax import lax
from jax.experimental import pallas as pl
from jax.experimental.pallas import tpu as pltpu

F32 = jnp.float32
BF16 = jnp.bfloat16

D_MODEL = 1024
N_HEADS = 16
N_KV_HEADS = 2
HEAD_DIM = 64
GROUP = N_HEADS // N_KV_HEADS
WINDOW = 128
ATTN_W = N_HEADS * HEAD_DIM
KV_W = N_KV_HEADS * HEAD_DIM
D_RNN = 1280
RNN_BLOCK = 128
N_RNN_BLOCKS = D_RNN // RNN_BLOCK
CONV_W = 4
LRU_C = 8.0
N_EXPERTS = 32
TOP_K = 4
D_FF = 1024
SWIGLU_LIMIT = 7.0
SWIGLU_ALPHA = 1.702
LN_EPS = 1e-5
PAST_LEN = 8192
PROJ_OFFS = (0, ATTN_W, ATTN_W + 2 * KV_W, ATTN_W + 2 * KV_W + D_RNN,
             ATTN_W + 2 * KV_W + 2 * D_RNN, ATTN_W + 2 * KV_W + 2 * D_RNN + D_MODEL,
             ATTN_W + 2 * KV_W + 2 * D_RNN + 2 * D_MODEL)
PROJ_W = PROJ_OFFS[-1]
NEG_BIG = -1e30
LANES = 128
LANE_GROUPS = D_MODEL // LANES

ROW_TILE = 256
RNN_TIME_TILE = 64
MOE_BLOCK = 512
GATHER_TILE = 128
ISSUE_UNROLL = 4
RUN_CHUNK = 8
SAMPLE_ATTN_TILE = 32
MIB = 1 << 20


def _alibi_slopes():
    h = np.arange(1, N_HEADS + 1, dtype=np.float32)
    return (np.float32(2.0) ** (np.float32(-8.0) * h / np.float32(N_HEADS))).astype(np.float32)


def _params(semantics, vmem_mib):
    return pltpu.CompilerParams(dimension_semantics=semantics, vmem_limit_bytes=vmem_mib * MIB)


def _const_spec(shape):
    nd = len(shape)
    return pl.BlockSpec(shape, lambda *_: (0,) * nd)


def _proj_kernel(x_ref, w_ref, q_ref, kv_ref, xr_ref, yr_ref, ga_ref, gb_ref):
    xb = x_ref[...].astype(BF16)
    outs = (q_ref, kv_ref, xr_ref, yr_ref, ga_ref, gb_ref)
    for n, o_ref in enumerate(outs):
        w = w_ref[:, PROJ_OFFS[n]:PROJ_OFFS[n + 1]]
        res = jnp.dot(xb, w, preferred_element_type=F32)
        if len(o_ref.shape) == 3:
            for c in range(o_ref.shape[0]):
                o_ref[c] = res[:, c * LANES:(c + 1) * LANES]
        else:
            o_ref[...] = res


def _project(x, w_in_bf):
    n = x.shape[0]
    tm = min(ROW_TILE, n)
    widths = [PROJ_OFFS[i + 1] - PROJ_OFFS[i] for i in range(6)]
    row = lambda w: (pl.BlockSpec((tm, w), lambda i: (i, 0)), jax.ShapeDtypeStruct((n, w), F32))
    stack = lambda w: (pl.BlockSpec((w // LANES, tm, LANES), lambda i: (0, i, 0)),
                       jax.ShapeDtypeStruct((w // LANES, n, LANES), F32))
    outs = [stack(widths[0]), stack(widths[1])] + [row(w) for w in widths[2:]]
    return pl.pallas_call(
        _proj_kernel,
        grid=(n // tm,),
        in_specs=[pl.BlockSpec((tm, D_MODEL), lambda i: (i, 0)),
                  _const_spec((D_MODEL, PROJ_W))],
        out_specs=[o[0] for o in outs],
        out_shape=[o[1] for o in outs],
        compiler_params=_params(("parallel",), 56),
        name="in_proj",
    )(x, w_in_bf)


def _prompt_attn_kernel(sink_ref, q_ref, kvp_ref, kvc_ref, o_ref, *, nb):
    tb = pl.program_id(0)
    seq_rows = pl.ds(pl.program_id(1), WINDOW, stride=nb)
    kcat = jnp.concatenate([kvp_ref[0, seq_rows, :], kvc_ref[0, seq_rows, :]], axis=0).astype(BF16)
    vcat = jnp.concatenate([kvp_ref[1, seq_rows, :], kvc_ref[1, seq_rows, :]], axis=0).astype(BF16)
    qi = lax.broadcasted_iota(jnp.int32, (WINDOW, 2 * WINDOW), 0)
    kj = lax.broadcasted_iota(jnp.int32, (WINDOW, 2 * WINDOW), 1)
    dist = qi - kj + WINDOW
    first_ok = jnp.where(kj >= WINDOW, 1, jnp.where(tb > 0, 1, 0))
    ok = jnp.where(dist >= 0, jnp.where(dist < WINDOW, first_ok, 0), 0)
    neg_dist = jnp.where(ok > 0, -dist.astype(F32), NEG_BIG)
    slopes = _alibi_slopes()
    scale = HEAD_DIM ** -0.5
    heads_per_chunk = LANES // HEAD_DIM
    for c in range(ATTN_W // LANES):
        qc = q_ref[c, seq_rows, :].astype(BF16)
        outs = []
        for j in range(heads_per_chunk):
            h = c * heads_per_chunk + j
            g = h // GROUP
            qh = qc[:, j * HEAD_DIM:(j + 1) * HEAD_DIM]
            kg = kcat[:, g * HEAD_DIM:(g + 1) * HEAD_DIM]
            vg = vcat[:, g * HEAD_DIM:(g + 1) * HEAD_DIM]
            s = lax.dot_general(qh, kg, (((1,), (1,)), ((), ())), preferred_element_type=F32)
            logits = s * scale + float(slopes[h]) * neg_dist
            sink = sink_ref[h]
            m = jnp.maximum(jnp.max(logits, axis=1, keepdims=True), sink)
            p = jnp.exp(logits - m)
            denom = jnp.sum(p, axis=1, keepdims=True) + jnp.exp(sink - m)
            outs.append(jnp.dot(p.astype(BF16), vg, preferred_element_type=F32) / denom)
        o_ref[c, seq_rows, :] = jnp.concatenate(outs, axis=1)


def _prompt_attention(q, kv, sinks, batch, seq):
    nblk = seq // WINDOW
    rows = WINDOW * batch
    cur = lambda t, b: (0, t, 0)
    prev = lambda t, b: (0, jnp.maximum(t - 1, 0), 0)
    return pl.pallas_call(
        functools.partial(_prompt_attn_kernel, nb=batch),
        grid=(nblk, batch),
        in_specs=[pl.BlockSpec(memory_space=pltpu.SMEM),
                  pl.BlockSpec((q.shape[0], rows, LANES), cur),
                  pl.BlockSpec((kv.shape[0], rows, LANES), prev),
                  pl.BlockSpec((kv.shape[0], rows, LANES), cur)],
        out_specs=pl.BlockSpec((q.shape[0], rows, LANES), cur),
        out_shape=jax.ShapeDtypeStruct(q.shape, F32),
        compiler_params=_params(("parallel", "arbitrary"), 40),
        name="prompt_attn",
    )(sinks, q, kv, kv)


def _sample_attn_kernel(qz_ref, ck_ref, cv_ref, kn_ref, vn_ref, bias_ref, sink_ref, o_ref):
    qz = qz_ref[...]
    s_c = jnp.einsum('bhd,bjd->bhj', qz, ck_ref[...].astype(BF16),
                     preferred_element_type=F32) * (HEAD_DIM ** -0.5)
    s_n = jnp.sum(qz.astype(F32) * kn_ref[...], axis=-1, keepdims=True) * (HEAD_DIM ** -0.5)
    logits = s_c + bias_ref[...][None]
    sink = sink_ref[...][None]
    m = jnp.maximum(jnp.maximum(jnp.max(logits, axis=-1, keepdims=True), s_n), sink)
    p_c = jnp.exp(logits - m)
    p_n = jnp.exp(s_n - m)
    denom = jnp.sum(p_c, axis=-1, keepdims=True) + p_n + jnp.exp(sink - m)
    o = jnp.einsum('bhj,bjd->bhd', p_c.astype(BF16), cv_ref[...].astype(BF16),
                   preferred_element_type=F32)
    o_ref[...] = (o + p_n * vn_ref[...]) / denom


def _sample_attention(q, kv, sinks, ck, cv):
    b, cw = ck.shape[0], ck.shape[1]
    bt = min(SAMPLE_ATTN_TILE, b)
    q = q.transpose(1, 0, 2).reshape(b, ATTN_W).astype(BF16)
    kv = kv.transpose(1, 0, 2).reshape(b, 2 * KV_W)
    q4 = q.reshape(b, N_KV_HEADS, GROUP, HEAD_DIM)
    eye = jnp.eye(N_KV_HEADS, dtype=q.dtype)
    qz = (q4[:, :, :, None, :] * eye[None, :, None, :, None]).reshape(b, N_HEADS, KV_W)
    kn = kv[:, :KV_W].reshape(b, 1, KV_W)
    vn = kv[:, KV_W:].reshape(b, 1, KV_W)
    dist = (cw - np.arange(cw)).astype(np.float32)
    bias = np.where(dist[None, :] < WINDOW, -_alibi_slopes()[:, None] * dist[None, :], NEG_BIG)
    o = pl.pallas_call(
        _sample_attn_kernel,
        grid=(b // bt,),
        in_specs=[pl.BlockSpec((bt, N_HEADS, KV_W), lambda i: (i, 0, 0)),
                  pl.BlockSpec((bt, cw, KV_W), lambda i: (i, 0, 0)),
                  pl.BlockSpec((bt, cw, KV_W), lambda i: (i, 0, 0)),
                  pl.BlockSpec((bt, 1, KV_W), lambda i: (i, 0, 0)),
                  pl.BlockSpec((bt, 1, KV_W), lambda i: (i, 0, 0)),
                  _const_spec((N_HEADS, cw)),
                  _const_spec((N_HEADS, 1))],
        out_specs=pl.BlockSpec((bt, N_HEADS, KV_W), lambda i: (i, 0, 0)),
        out_shape=jax.ShapeDtypeStruct((b, N_HEADS, KV_W), F32),
        compiler_params=_params(("parallel",), 32),
        name="sample_attn",
    )(qz, ck.reshape(b, cw, KV_W), cv.reshape(b, cw, KV_W), kn, vn,
      jnp.asarray(bias, F32), sinks.reshape(N_HEADS, 1))
    o4 = o.reshape(b, N_KV_HEADS, GROUP, N_KV_HEADS, HEAD_DIM)
    o_sel = jnp.stack([o4[:, g, :, g, :] for g in range(N_KV_HEADS)], axis=1)
    return o_sel.reshape(b, ATTN_W // LANES, LANES).transpose(1, 0, 2)


def _rnn_kernel(xr_ref, yr_ref, cw_ref, cb_ref, wa_ref, ba_ref, wx_ref, bx_ref, lam_ref,
                conv0_ref, h0_ref, y_ref, convo_ref, ho_ref, xp_s, a_s, b_s, h_s, *, nb, tt):
    rows = tt * nb
    halo = (CONV_W - 1) * nb

    @pl.when(pl.program_id(0) == 0)
    def _():
        xp_s[0:halo] = conv0_ref[...]
        h_s[...] = h0_ref[...]

    xp_s[halo:halo + rows] = xr_ref[...]
    nl = -lam_ref[...]
    coef = -LRU_C * (jnp.maximum(nl, 0.0) + jnp.log1p(jnp.exp(-jnp.abs(nl))))
    for n in range(N_RNN_BLOCKS):
        sl = slice(n * RNN_BLOCK, (n + 1) * RNN_BLOCK)
        xc = cb_ref[:, sl] + xp_s[0:rows, sl] * cw_ref[0:1, sl]
        for j in range(1, CONV_W):
            xc = xc + xp_s[j * nb:j * nb + rows, sl] * cw_ref[j:j + 1, sl]
        xcb = xc.astype(BF16)
        r = jax.nn.sigmoid(jnp.dot(xcb, wa_ref[n], preferred_element_type=F32) + ba_ref[:, sl])
        ig = jax.nn.sigmoid(jnp.dot(xcb, wx_ref[n], preferred_element_type=F32) + bx_ref[:, sl])
        log_a = coef[:, sl] * r
        a = jnp.exp(log_a)
        a_s[:, sl] = a
        b_s[:, sl] = jnp.sqrt(1.0 - a * a) * (ig * xc)

    def step(t, h):
        off = pl.multiple_of(t * nb, nb)
        h = a_s[pl.ds(off, nb), :] * h + b_s[pl.ds(off, nb), :]
        a_s[pl.ds(off, nb), :] = h
        return h

    h = lax.fori_loop(0, tt, step, h_s[...])
    h_s[...] = h
    yr = yr_ref[...]
    gelu = 0.5 * yr * (1.0 + jnp.tanh(np.float32(np.sqrt(2.0 / np.pi))
                                      * (yr + np.float32(0.044715) * (yr * yr * yr))))
    y_ref[...] = (a_s[...] * gelu).astype(y_ref.dtype)
    tail = xp_s[rows:rows + halo]
    convo_ref[...] = tail
    ho_ref[...] = h
    xp_s[0:halo] = tail


def _rnn_branch(xr, yr, p, conv0, h0, nb):
    rows_total = xr.shape[0]
    t_total = rows_total // nb
    tt = min(RNN_TIME_TILE, t_total)
    rows = tt * nb
    halo = (CONV_W - 1) * nb
    row_spec = pl.BlockSpec((rows, D_RNN), lambda i: (i, 0))
    vec = _const_spec((1, D_RNN))
    return pl.pallas_call(
        functools.partial(_rnn_kernel, nb=nb, tt=tt),
        grid=(t_total // tt,),
        in_specs=[row_spec, row_spec, _const_spec((CONV_W, D_RNN)), vec,
                  _const_spec((N_RNN_BLOCKS, RNN_BLOCK, RNN_BLOCK)), vec,
                  _const_spec((N_RNN_BLOCKS, RNN_BLOCK, RNN_BLOCK)), vec, vec,
                  _const_spec((halo, D_RNN)), _const_spec((nb, D_RNN))],
        out_specs=[row_spec, _const_spec((halo, D_RNN)), _const_spec((nb, D_RNN))],
        out_shape=[jax.ShapeDtypeStruct((rows_total, D_RNN), BF16),
                   jax.ShapeDtypeStruct((halo, D_RNN), F32),
                   jax.ShapeDtypeStruct((nb, D_RNN), F32)],
        scratch_shapes=[pltpu.VMEM((rows + halo, D_RNN), F32),
                        pltpu.VMEM((rows, D_RNN), F32),
                        pltpu.VMEM((rows, D_RNN), F32),
                        pltpu.VMEM((nb, D_RNN), F32)],
        compiler_params=_params(("arbitrary",), 48),
        name="rnn_branch",
    )(xr, yr, p['conv_w'], p['conv_b'], p['rg_wa'], p['rg_ba'], p['rg_wx'], p['rg_bx'],
      p['rg_lambda'], conv0, h0)


def _load_token_tiles(ref, n, lead=()):
    return [ref[lead + (pl.ds(c, n, stride=LANE_GROUPS), slice(None))] for c in range(LANE_GROUPS)]


def _store_token_tiles(ref, x, n):
    for c in range(LANE_GROUPS):
        ref[pl.ds(c, n, stride=LANE_GROUPS), :] = x[:, c * LANES:(c + 1) * LANES]


def _layer_norm(z, g, b):
    mu = jnp.mean(z, axis=-1, keepdims=True)
    zc = z - mu
    var = jnp.mean(zc * zc, axis=-1, keepdims=True)
    return zc * lax.rsqrt(var + LN_EPS) * g + b


def _mix_kernel(o_ref, y_ref, ga_ref, gb_ref, x_ref, wao_ref, wro_ref, wo_ref, g1_ref, b1_ref,
                wrt_ref, br_ref, cnt0_ref, x1_ref, tope_ref, gate_ref, rank_ref, cnt_ref, runs_ref,
                run_s, *, alpha, tm):
    @pl.when(pl.program_id(0) == 0)
    def _():
        run_s[...] = cnt0_ref[...]

    o_attn = jnp.concatenate([o_ref[c].astype(BF16) for c in range(ATTN_W // LANES)], axis=1)
    att = jnp.dot(o_attn, wao_ref[...], preferred_element_type=F32)
    rn = jnp.dot(y_ref[...], wro_ref[...], preferred_element_type=F32)
    merged = jax.nn.sigmoid(ga_ref[...]) * att + jax.nn.sigmoid(gb_ref[...]) * rn
    mixed = jnp.dot(merged.astype(BF16), wo_ref[...], preferred_element_type=F32)
    x1 = _layer_norm(alpha * x_ref[...] + mixed, g1_ref[...], b1_ref[...])
    _store_token_tiles(x1_ref, x1, tm)

    logits = lax.dot_general(wrt_ref[...], x1.astype(BF16), (((1,), (1,)), ((), ())),
                             preferred_element_type=F32) + br_ref[...]
    eidx = lax.broadcasted_iota(jnp.int32, (N_EXPERTS, tm), 0)
    vals, idxs, hots = [], [], []
    for _ in range(TOP_K):
        m = jnp.max(logits, axis=0, keepdims=True)
        idx = jnp.min(jnp.where(logits == m, eidx, N_EXPERTS), axis=0, keepdims=True)
        hot = eidx == idx
        logits = jnp.where(hot, -jnp.inf, logits)
        vals.append(m)
        idxs.append(idx)
        hots.append(hot)
    exps = [jnp.exp(v - vals[0]) for v in vals]
    total = exps[0] + exps[1] + exps[2] + exps[3]
    gate_ref[...] = jnp.concatenate([e / total for e in exps], axis=0)
    tope_ref[...] = jnp.concatenate(idxs, axis=0)

    assigned = sum(h.astype(F32) for h in hots)
    earlier = (lax.broadcasted_iota(jnp.int32, (tm, tm), 0)
               < lax.broadcasted_iota(jnp.int32, (tm, tm), 1)).astype(BF16)
    before = jnp.dot(assigned.astype(BF16), earlier, preferred_element_type=F32) + run_s[...]
    ranks = [jnp.sum(jnp.where(h, before, 0.0), axis=0, keepdims=True) for h in hots]
    rank_ref[...] = jnp.concatenate(ranks, axis=0).astype(jnp.int32)
    runs_ref[0] = run_s[...]
    run_s[...] = run_s[...] + jnp.sum(assigned, axis=1, keepdims=True)
    cnt_ref[...] = run_s[...]


def _mix(o_attn, y_rnn, ga, gb, x, p, cnt0, alpha):
    n = x.shape[0]
    tm = min(ROW_TILE, n)
    row = lambda w: pl.BlockSpec((tm, w), lambda i: (i, 0))
    tok = pl.BlockSpec((TOP_K, tm), lambda i: (0, i))
    vec = _const_spec((1, D_MODEL))
    cnt = _const_spec((N_EXPERTS, 1))
    return pl.pallas_call(
        functools.partial(_mix_kernel, alpha=alpha, tm=tm),
        grid=(n // tm,),
        in_specs=[pl.BlockSpec((ATTN_W // LANES, tm, LANES), lambda i: (0, i, 0)),
                  row(D_RNN), row(D_MODEL), row(D_MODEL), row(D_MODEL),
                  _const_spec((ATTN_W, D_MODEL)), _const_spec((D_RNN, D_MODEL)),
                  _const_spec((D_MODEL, D_MODEL)), vec, vec,
                  _const_spec((N_EXPERTS, D_MODEL)), cnt, cnt],
        out_specs=[pl.BlockSpec((tm * LANE_GROUPS, LANES), lambda i: (i, 0)), tok, tok, tok, cnt,
                   pl.BlockSpec((1, N_EXPERTS, 1), lambda i: (i, 0, 0))],
        out_shape=[jax.ShapeDtypeStruct((n * LANE_GROUPS, LANES), F32),
                   jax.ShapeDtypeStruct((TOP_K, n), jnp.int32),
                   jax.ShapeDtypeStruct((TOP_K, n), F32),
                   jax.ShapeDtypeStruct((TOP_K, n), jnp.int32),
                   jax.ShapeDtypeStruct((N_EXPERTS, 1), F32),
                   jax.ShapeDtypeStruct((n // tm, N_EXPERTS, 1), F32)],
        scratch_shapes=[pltpu.VMEM((N_EXPERTS, 1), F32)],
        compiler_params=_params(("arbitrary",), 48),
        name="mix_norm_router",
    )(o_attn, y_rnn, ga, gb, x, p['w_attn_out'], p['w_rnn_out'], p['w_out'], p['ln1_g'],
      p['ln1_b'], p['w_router_t'], p['b_router'], cnt0)


def _tile_copy(src, src_tok, dst, dst_tok, sem):
    rows = lambda t: pl.ds(pl.multiple_of(t * LANE_GROUPS, LANE_GROUPS), LANE_GROUPS)
    return pltpu.make_async_copy(src.at[rows(src_tok)], dst.at[rows(dst_tok)], sem)


_start_copy = lambda cp, k: cp.start(priority=k % 2)
_wait_copy = lambda cp, k: cp.wait()


def _dispatch_kernel(dest_ref, x_ref, xs_in_ref, xs_ref, stage, sems, *, n, tm):
    del xs_in_ref
    i = pl.program_id(0)
    last = pl.num_programs(0) - 1
    slot = i % 2

    def copies(step, s, act, unroll=1):
        base = step * tm

        def body(r, c):
            for k in range(TOP_K):
                act(_tile_copy(stage.at[s], r, xs_ref, dest_ref[k * n + base + r], sems.at[s]), k)
            return c
        lax.fori_loop(0, tm, body, 0, unroll=unroll)

    @pl.when(i >= 2)
    def _():
        copies(i - 2, slot, _wait_copy)

    stage[slot] = x_ref[...]
    copies(i, slot, _start_copy, ISSUE_UNROLL)

    @pl.when(i == last)
    def _():
        @pl.when(i >= 1)
        def _():
            copies(i - 1, 1 - slot, _wait_copy)
        copies(i, slot, _wait_copy)


def _dispatch(x1_tiles, dest, xs):
    n = x1_tiles.shape[0] // LANE_GROUPS
    tm = min(GATHER_TILE, n)
    return pl.pallas_call(
        functools.partial(_dispatch_kernel, n=n, tm=tm),
        grid_spec=pltpu.PrefetchScalarGridSpec(
            num_scalar_prefetch=1,
            grid=(n // tm,),
            in_specs=[pl.BlockSpec((tm * LANE_GROUPS, LANES), lambda i, d: (i, 0)),
                      pl.BlockSpec(memory_space=pl.ANY)],
            out_specs=pl.BlockSpec(memory_space=pl.ANY),
            scratch_shapes=[pltpu.VMEM((2, tm * LANE_GROUPS, LANES), F32),
                            pltpu.SemaphoreType.DMA((2,))]),
        out_shape=jax.ShapeDtypeStruct(xs.shape, xs.dtype),
        input_output_aliases={2: 0},
        compiler_params=_params(("arbitrary",), 32),
        name="moe_dispatch",
    )(dest, x1_tiles, xs)


def _expert_kernel(be_ref, next_ref, nused_ref, x_ref, wu_hbm, bu_ref, wd_hbm, bd_ref, y_ref,
                   wu_f, wd_f, wu_s, wd_s, slot_s, sems, *, layer):
    i = pl.program_id(0)

    def weight_copies(e, slot):
        return (pltpu.make_async_copy(wu_hbm.at[layer, e], wu_f.at[slot], sems.at[slot, 0]),
                pltpu.make_async_copy(wd_hbm.at[layer, e], wd_f.at[slot], sems.at[slot, 1]))

    @pl.when(i < nused_ref[0])
    def _():
        e = be_ref[i]

        @pl.when(i == 0)
        def _():
            slot_s[0] = 0
            for cp in weight_copies(e, 0):
                cp.start()

        @pl.when(jnp.logical_or(i == 0, e != be_ref[jnp.maximum(i - 1, 0)]))
        def _():
            @pl.when(i > 0)
            def _():
                slot_s[0] = 1 - slot_s[0]
            slot = slot_s[0]
            for cp in weight_copies(e, slot):
                cp.wait()
            nxt = next_ref[i]

            @pl.when(nxt >= 0)
            def _():
                for cp in weight_copies(nxt, 1 - slot):
                    cp.start()
            wu_s[...] = wu_f[slot].astype(BF16)
            wd_s[...] = wd_f[slot].astype(BF16)

        xb = jnp.concatenate([c.astype(BF16) for c in _load_token_tiles(x_ref, MOE_BLOCK)], axis=1)
        h = jnp.dot(xb, wu_s[...], preferred_element_type=F32) + bu_ref[0, 0]
        glu = jnp.minimum(h[:, :D_FF], SWIGLU_LIMIT)
        lin = jnp.clip(h[:, D_FF:], -SWIGLU_LIMIT, SWIGLU_LIMIT)
        act = glu * jax.nn.sigmoid(SWIGLU_ALPHA * glu) * (lin + 1.0)
        y = jnp.dot(act.astype(BF16), wd_s[...], preferred_element_type=F32) + bd_ref[0, 0]
        _store_token_tiles(y_ref, y, MOE_BLOCK)

    @pl.when(i >= nused_ref[0])
    def _():
        y_ref[...] = jnp.zeros_like(y_ref)


def _experts(xs, block_e, next_e, nused, layer, w_up, b_up, w_down, b_down):
    rows = xs.shape[0] // LANE_GROUPS
    nblk = rows // MOE_BLOCK
    depth = w_up.shape[0]
    tile_blk = (MOE_BLOCK * LANE_GROUPS, LANES)
    blk = lambda i, be, nx, nu: (jnp.minimum(i, nu[0] - 1), 0)
    out_blk = lambda i, be, nx, nu: (i, 0)
    exp3 = lambda i, be, nx, nu: (layer, be[i], 0, 0)
    return pl.pallas_call(
        functools.partial(_expert_kernel, layer=layer),
        grid_spec=pltpu.PrefetchScalarGridSpec(
            num_scalar_prefetch=3,
            grid=(nblk,),
            in_specs=[pl.BlockSpec(tile_blk, blk),
                      pl.BlockSpec(memory_space=pl.ANY),
                      pl.BlockSpec((1, 1, 1, 2 * D_FF), exp3),
                      pl.BlockSpec(memory_space=pl.ANY),
                      pl.BlockSpec((1, 1, 1, D_MODEL), exp3)],
            out_specs=pl.BlockSpec(tile_blk, out_blk),
            scratch_shapes=[pltpu.VMEM((2, D_MODEL, 2 * D_FF), F32),
                            pltpu.VMEM((2, D_FF, D_MODEL), F32),
                            pltpu.VMEM((D_MODEL, 2 * D_FF), BF16),
                            pltpu.VMEM((D_FF, D_MODEL), BF16),
                            pltpu.SMEM((1,), jnp.int32),
                            pltpu.SemaphoreType.DMA((2, 2))]),
        out_shape=jax.ShapeDtypeStruct(xs.shape, F32),
        compiler_params=_params(("arbitrary",), 56),
        name="moe_experts",
    )(block_e, next_e, nused, xs, w_up, b_up.reshape(depth, N_EXPERTS, 1, 2 * D_FF), w_down,
      b_down.reshape(depth, N_EXPERTS, 1, D_MODEL))


def _combine_kernel(src_ref, nch_ref, loc_ref, yb_ref, x1_ref, gate_ref, col_ref, g2_ref, b2_ref,
                    x2_ref, buf, sems, *, tm, stage, alpha):
    i = pl.program_id(0)
    slot = i % 2
    piece = RUN_CHUNK * LANE_GROUPS

    def runs(step, buf_slot, act):
        def one_expert(e, prio):
            idx = step * N_EXPERTS + e
            src0, loc0 = src_ref[idx], loc_ref[idx]

            def body(j, c):
                s = pl.multiple_of((src0 + j * RUN_CHUNK) * LANE_GROUPS, LANE_GROUPS)
                d = pl.multiple_of((loc0 + j * RUN_CHUNK) * LANE_GROUPS, piece)
                act(pltpu.make_async_copy(yb_ref.at[pl.ds(s, piece)],
                                          buf.at[buf_slot, pl.ds(d, piece)], sems.at[buf_slot]), prio)
                return c
            lax.fori_loop(0, nch_ref[idx], body, 0)

        def pair(m, c):
            one_expert(2 * m, 0)
            one_expert(2 * m + 1, 1)
            return c
        lax.fori_loop(0, N_EXPERTS // 2, pair, 0)

    @pl.when(i == 0)
    def _():
        buf[...] = jnp.zeros_like(buf)
        runs(0, 0, _start_copy)

    @pl.when(i + 1 < pl.num_programs(0))
    def _():
        runs(i + 1, 1 - slot, _start_copy)

    runs(i, slot, _wait_copy)
    staged = jnp.concatenate([c.astype(BF16) for c in _load_token_tiles(buf, stage, lead=(slot,))],
                             axis=1)
    gate, col = gate_ref[...], col_ref[...]
    pos = lax.broadcasted_iota(jnp.int32, (tm, stage), 1)
    weights = jnp.where(col[:, 0:1] == pos, gate[:, 0:1], 0.0)
    for k in range(1, TOP_K):
        weights = weights + jnp.where(col[:, k:k + 1] == pos, gate[:, k:k + 1], 0.0)
    y = jnp.dot(weights.astype(BF16), staged, preferred_element_type=F32)
    x1 = jnp.concatenate(_load_token_tiles(x1_ref, tm), axis=1)
    x2_ref[...] = _layer_norm(alpha * x1 + y, g2_ref[...], b2_ref[...])


def _combine(yb, x1_tiles, runs_plan, gate_rows, col_rows, tm, ln_g, ln_b, alpha):
    n = x1_tiles.shape[0] // LANE_GROUPS
    stage = -(-(TOP_K * tm + N_EXPERTS * (RUN_CHUNK - 1)) // LANES) * LANES
    vec = pl.BlockSpec((1, D_MODEL), lambda i, *_: (0, 0))
    tok = pl.BlockSpec((tm, TOP_K), lambda i, *_: (i, 0))
    return pl.pallas_call(
        functools.partial(_combine_kernel, tm=tm, stage=stage, alpha=alpha),
        grid_spec=pltpu.PrefetchScalarGridSpec(
            num_scalar_prefetch=3,
            grid=(n // tm,),
            in_specs=[pl.BlockSpec(memory_space=pl.ANY),
                      pl.BlockSpec((tm * LANE_GROUPS, LANES), lambda i, *_: (i, 0)),
                      tok, tok, vec, vec],
            out_specs=pl.BlockSpec((tm, D_MODEL), lambda i, *_: (i, 0)),
            scratch_shapes=[pltpu.VMEM((2, stage * LANE_GROUPS, LANES), F32),
                            pltpu.SemaphoreType.DMA((2,))]),
        out_shape=jax.ShapeDtypeStruct((n, D_MODEL), F32),
        compiler_params=_params(("arbitrary",), 48),
        name="moe_combine",
    )(*runs_plan, yb, x1_tiles, gate_rows, col_rows, ln_g, ln_b)


def _combine_plan(top_e, rank, runs, counts_after, pad_start, tm):
    nt = runs.shape[0]
    tile_cnt = jnp.concatenate([runs[1:], counts_after[None]], axis=0) - runs
    nch = (tile_cnt + RUN_CHUNK - 1) // RUN_CHUNK
    room = nch * RUN_CHUNK
    loc = jnp.cumsum(room, axis=1) - room
    src = pad_start[None, :] + runs
    eids = jnp.arange(N_EXPERTS, dtype=jnp.int32)
    base = (loc - runs)[None, :, None, :]
    hit = top_e.reshape(TOP_K, nt, tm)[..., None] == eids
    col = jnp.sum(jnp.where(hit, base, 0), axis=-1).reshape(TOP_K, nt * tm) + rank
    flat = lambda a: a.reshape(-1).astype(jnp.int32)
    return (flat(src), flat(nch), flat(loc)), col.T.astype(jnp.int32)


def _moe_layout(counts, groups):
    n_assign = sum(g[0].shape[1] for g in groups) * TOP_K
    nblk = -(-(n_assign + N_EXPERTS * (MOE_BLOCK - 1)) // MOE_BLOCK) + 1
    padded = (counts + MOE_BLOCK - 1) // MOE_BLOCK * MOE_BLOCK
    pad_end = jnp.cumsum(padded)
    pad_start = pad_end - padded
    nused = pad_end[-1] // MOE_BLOCK

    def expert_of(row):
        ended = (pad_end[None, :] <= row[:, None]).astype(jnp.int32)
        return jnp.minimum(jnp.sum(ended, axis=1), N_EXPERTS - 1)

    blk = jnp.arange(nblk, dtype=jnp.int32)
    block_e = expert_of(jnp.minimum(blk, nused - 1) * MOE_BLOCK)
    eids = jnp.arange(N_EXPERTS, dtype=jnp.int32)
    lookup = lambda table, idx: jnp.sum(jnp.where(idx[..., None] == eids, table, 0), axis=-1)
    after = lookup(pad_end, block_e)
    next_e = jnp.where(after < pad_end[-1], expert_of(after), -1).astype(jnp.int32)
    dests = [(lookup(pad_start, top_e) + rank).reshape(-1).astype(jnp.int32)
             for top_e, rank in groups]
    return (nblk, block_e.astype(jnp.int32), next_e, nused.astype(jnp.int32).reshape(1), dests,
            pad_start)


def _temporal(x, p, attend, conv0, h0, nb, cnt0, alpha):
    q, kv, xr, yr, ga, gb = _project(x, p['w_in'])
    o_attn = attend(q, kv)
    y_rnn, conv_new, h_new = _rnn_branch(xr, yr, p, conv0, h0, nb)
    x1, top_e, gate, rank, cnt, runs = _mix(o_attn, y_rnn, ga, gb, x, p, cnt0, alpha)
    route = (top_e, gate, rank, runs.reshape(-1, N_EXPERTS).astype(jnp.int32))
    return x1, route, cnt, kv, conv_new, h_new


def kernel(x_prompt, x_sample, cache_k, cache_v, state_conv, state_h, w_in, attn_sinks, conv_w,
           conv_b, rg_wa, rg_ba, rg_wx, rg_bx, rg_lambda, w_attn_out, w_rnn_out, w_out, ln1_g,
           ln1_b, w_router, b_router, w_up, b_up, w_down, b_down, ln2_g, ln2_b):
    depth = w_in.shape[0]
    alpha = float((2 * depth) ** 0.25)
    bp, seq, _ = x_prompt.shape
    bs = x_sample.shape[0]
    cw = cache_k.shape[2]
    halo = CONV_W - 1

    xp = x_prompt.transpose(1, 0, 2).reshape(seq * bp, D_MODEL)
    xs = x_sample.reshape(bs, D_MODEL)
    outs = [[] for _ in range(8)]
    zeros_conv = jnp.zeros((halo * bp, D_RNN), F32)
    zeros_h = jnp.zeros((bp, D_RNN), F32)
    zeros_cnt = jnp.zeros((N_EXPERTS, 1), F32)

    for l in range(depth):
        p = {
            'w_in': w_in[l].astype(BF16),
            'conv_w': conv_w[l], 'conv_b': conv_b[l].reshape(1, D_RNN),
            'rg_wa': rg_wa[l].astype(BF16), 'rg_ba': rg_ba[l].reshape(1, D_RNN),
            'rg_wx': rg_wx[l].astype(BF16), 'rg_bx': rg_bx[l].reshape(1, D_RNN),
            'rg_lambda': rg_lambda[l].reshape(1, D_RNN),
            'w_attn_out': w_attn_out[l].astype(BF16), 'w_rnn_out': w_rnn_out[l].astype(BF16),
            'w_out': w_out[l].astype(BF16),
            'ln1_g': ln1_g[l].reshape(1, D_MODEL), 'ln1_b': ln1_b[l].reshape(1, D_MODEL),
            'w_router_t': w_router[l].T.astype(BF16),
            'b_router': b_router[l].reshape(N_EXPERTS, 1),
        }
        sinks = attn_sinks[l]
        g2, b2 = ln2_g[l].reshape(1, D_MODEL), ln2_b[l].reshape(1, D_MODEL)

        x1p, route_p, cnt_p, kv_p, conv_p, h_p = _temporal(
            xp, p, lambda q, kv: _prompt_attention(q, kv, sinks, bp, seq),
            zeros_conv, zeros_h, bp, zeros_cnt, alpha)
        conv0_s = state_conv[l].transpose(1, 0, 2).reshape(halo * bs, D_RNN)
        x1s, route_s, cnt_s, kv_s, conv_s, h_s = _temporal(
            xs, p, lambda q, kv: _sample_attention(q, kv, sinks, cache_k[l], cache_v[l]),
            conv0_s, state_h[l], bs, cnt_p, alpha)

        (te_p, gt_p, rk_p, runs_p), (te_s, gt_s, rk_s, runs_s) = route_p, route_s
        counts_p = cnt_p.reshape(N_EXPERTS).astype(jnp.int32)
        counts = cnt_s.reshape(N_EXPERTS).astype(jnp.int32)
        nblk, block_e, next_e, nused, (dest_p, dest_s), pad_start = _moe_layout(
            counts, [(te_p, rk_p), (te_s, rk_s)])
        sorted_x = jnp.zeros((nblk * MOE_BLOCK * LANE_GROUPS, LANES), F32)
        sorted_x = _dispatch(x1p, dest_p, sorted_x)
        sorted_x = _dispatch(x1s, dest_s, sorted_x)
        yb = _experts(sorted_x, block_e, next_e, nused, l, w_up, b_up, w_down, b_down)
        tm_p, tm_s = min(ROW_TILE, seq * bp), min(ROW_TILE, bs)
        plan_p, col_p = _combine_plan(te_p, rk_p, runs_p, counts_p, pad_start, tm_p)
        plan_s, col_s = _combine_plan(te_s, rk_s, runs_s, counts, pad_start, tm_s)
        xp = _combine(yb, x1p, plan_p, gt_p.T, col_p, tm_p, g2, b2, alpha)
        xs = _combine(yb, x1s, plan_s, gt_s.T, col_s, tm_s, g2, b2, alpha)

        kv_p4 = kv_p[:, (seq - cw) * bp:].reshape(2, cw, bp, N_KV_HEADS, HEAD_DIM)
        outs[0].append(kv_p4[0].transpose(1, 0, 2, 3))
        outs[1].append(kv_p4[1].transpose(1, 0, 2, 3))
        outs[2].append(conv_p.reshape(halo, bp, D_RNN).transpose(1, 0, 2))
        outs[3].append(h_p)
        kv_s4 = kv_s.reshape(2, bs, 1, N_KV_HEADS, HEAD_DIM)
        outs[4].append(jnp.concatenate([cache_k[l], kv_s4[0]], axis=1)[:, -cw:])
        outs[5].append(jnp.concatenate([cache_v[l], kv_s4[1]], axis=1)[:, -cw:])
        outs[6].append(conv_s.reshape(halo, bs, D_RNN).transpose(1, 0, 2))
        outs[7].append(h_s)

    y_prompt = xp.reshape(seq, bp, D_MODEL).transpose(1, 0, 2)
    y_sample = xs.reshape(bs, 1, D_MODEL)
    return (y_prompt, y_sample) + tuple(jnp.stack(o) for o in outs)
```

```python
import functools

import numpy as np
import jax
import jax.numpy as jnp
from jax import lax
from jax.experimental import pallas as pl
from jax.experimental.pallas import tpu as pltpu

F32 = jnp.float32
BF16 = jnp.bfloat16

D_MODEL = 1024
N_HEADS = 16
N_KV_HEADS = 2
HEAD_DIM = 64
GROUP = N_HEADS // N_KV_HEADS
WINDOW = 128
ATTN_W = N_HEADS * HEAD_DIM
KV_W = N_KV_HEADS * HEAD_DIM
D_RNN = 1280
RNN_BLOCK = 128
N_RNN_BLOCKS = D_RNN // RNN_BLOCK
CONV_W = 4
LRU_C = 8.0
N_EXPERTS = 32
TOP_K = 4
D_FF = 1024
SWIGLU_LIMIT = 7.0
SWIGLU_ALPHA = 1.702
LN_EPS = 1e-5
PAST_LEN = 8192
PROJ_OFFS = (0, ATTN_W, ATTN_W + 2 * KV_W, ATTN_W + 2 * KV_W + D_RNN,
             ATTN_W + 2 * KV_W + 2 * D_RNN, ATTN_W + 2 * KV_W + 2 * D_RNN + D_MODEL,
             ATTN_W + 2 * KV_W + 2 * D_RNN + 2 * D_MODEL)
PROJ_W = PROJ_OFFS[-1]
NEG_BIG = -1e30
LANES = 128
LANE_GROUPS = D_MODEL // LANES

ROW_TILE = 256
RNN_TIME_TILE = 64
MOE_BLOCK = 512
GATHER_TILE = 128
ISSUE_UNROLL = 4
RUN_CHUNK = 8
QUERY_SPLIT = 128
SAMPLE_ATTN_TILE = 32
MIB = 1 << 20


def _alibi_slopes():
    h = np.arange(1, N_HEADS + 1, dtype=np.float32)
    return (np.float32(2.0) ** (np.float32(-8.0) * h / np.float32(N_HEADS))).astype(np.float32)


def _params(semantics, vmem_mib):
    return pltpu.CompilerParams(dimension_semantics=semantics, vmem_limit_bytes=vmem_mib * MIB)


def _const_spec(shape):
    nd = len(shape)
    return pl.BlockSpec(shape, lambda *_: (0,) * nd)


def _proj_kernel(x_ref, w_ref, q_ref, kv_ref, xr_ref, yr_ref, ga_ref, gb_ref):
    xb = x_ref[...].astype(BF16)
    outs = (q_ref, kv_ref, xr_ref, yr_ref, ga_ref, gb_ref)
    for n, o_ref in enumerate(outs):
        w = w_ref[:, PROJ_OFFS[n]:PROJ_OFFS[n + 1]]
        res = jnp.dot(xb, w, preferred_element_type=F32)
        if len(o_ref.shape) == 3:
            for c in range(o_ref.shape[0]):
                o_ref[c] = res[:, c * LANES:(c + 1) * LANES]
        else:
            o_ref[...] = res


def _project(x, w_in_bf):
    n = x.shape[0]
    tm = min(ROW_TILE, n)
    widths = [PROJ_OFFS[i + 1] - PROJ_OFFS[i] for i in range(6)]
    row = lambda w: (pl.BlockSpec((tm, w), lambda i: (i, 0)), jax.ShapeDtypeStruct((n, w), F32))
    stack = lambda w: (pl.BlockSpec((w // LANES, tm, LANES), lambda i: (0, i, 0)),
                       jax.ShapeDtypeStruct((w // LANES, n, LANES), F32))
    outs = [stack(widths[0]), stack(widths[1])] + [row(w) for w in widths[2:]]
    return pl.pallas_call(
        _proj_kernel,
        grid=(n // tm,),
        in_specs=[pl.BlockSpec((tm, D_MODEL), lambda i: (i, 0)),
                  _const_spec((D_MODEL, PROJ_W))],
        out_specs=[o[0] for o in outs],
        out_shape=[o[1] for o in outs],
        compiler_params=_params(("parallel",), 56),
        name="in_proj",
    )(x, w_in_bf)


def _prompt_attn_kernel(sink_ref, q_ref, kvp_ref, kvc_ref, o_ref, *, nb):
    tb = pl.program_id(0)
    seq_rows = pl.ds(pl.program_id(1), WINDOW, stride=nb)
    kcat = jnp.concatenate([kvp_ref[0, seq_rows, :], kvc_ref[0, seq_rows, :]], axis=0).astype(BF16)
    vcat = jnp.concatenate([kvp_ref[1, seq_rows, :], kvc_ref[1, seq_rows, :]], axis=0).astype(BF16)
    band = QUERY_SPLIT + WINDOW
    qi = lax.broadcasted_iota(jnp.int32, (QUERY_SPLIT, band), 0)
    kj = lax.broadcasted_iota(jnp.int32, (QUERY_SPLIT, band), 1)
    dist = qi - kj + WINDOW
    in_window = jnp.where(dist >= 0, jnp.where(dist < WINDOW, 1, 0), 0)
    neg_dists = []
    for r0 in range(0, WINDOW, QUERY_SPLIT):
        exists = jnp.where(kj + r0 >= WINDOW, 1, jnp.where(tb > 0, 1, 0))
        neg_dists.append(jnp.where(in_window * exists > 0, -dist.astype(F32), NEG_BIG))
    slopes = _alibi_slopes()
    scale = HEAD_DIM ** -0.5
    heads_per_chunk = LANES // HEAD_DIM
    for c in range(ATTN_W // LANES):
        qc = q_ref[c, seq_rows, :].astype(BF16)
        outs = []
        for j in range(heads_per_chunk):
            h = c * heads_per_chunk + j
            g = h // GROUP
            sink = sink_ref[h]
            parts = []
            for s_idx, r0 in enumerate(range(0, WINDOW, QUERY_SPLIT)):
                qh = qc[r0:r0 + QUERY_SPLIT, j * HEAD_DIM:(j + 1) * HEAD_DIM]
                kg = kcat[r0:r0 + band, g * HEAD_DIM:(g + 1) * HEAD_DIM]
                vg = vcat[r0:r0 + band, g * HEAD_DIM:(g + 1) * HEAD_DIM]
                s = lax.dot_general(qh, kg, (((1,), (1,)), ((), ())), preferred_element_type=F32)
                logits = s * scale + float(slopes[h]) * neg_dists[s_idx]
                m = jnp.maximum(jnp.max(logits, axis=1, keepdims=True), sink)
                p = jnp.exp(logits - m)
                denom = jnp.sum(p, axis=1, keepdims=True) + jnp.exp(sink - m)
                parts.append(jnp.dot(p.astype(BF16), vg, preferred_element_type=F32) / denom)
            outs.append(jnp.concatenate(parts, axis=0))
        o_ref[c, seq_rows, :] = jnp.concatenate(outs, axis=1)


def _prompt_attention(q, kv, sinks, batch, seq):
    nblk = seq // WINDOW
    rows = WINDOW * batch
    cur = lambda t, b: (0, t, 0)
    prev = lambda t, b: (0, jnp.maximum(t - 1, 0), 0)
    return pl.pallas_call(
        functools.partial(_prompt_attn_kernel, nb=batch),
        grid=(nblk, batch),
        in_specs=[pl.BlockSpec(memory_space=pltpu.SMEM),
                  pl.BlockSpec((q.shape[0], rows, LANES), cur),
                  pl.BlockSpec((kv.shape[0], rows, LANES), prev),
                  pl.BlockSpec((kv.shape[0], rows, LANES), cur)],
        out_specs=pl.BlockSpec((q.shape[0], rows, LANES), cur),
        out_shape=jax.ShapeDtypeStruct(q.shape, F32),
        compiler_params=_params(("parallel", "arbitrary"), 40),
        name="prompt_attn",
    )(sinks, q, kv, kv)


def _sample_attn_kernel(qz_ref, ck_ref, cv_ref, kn_ref, vn_ref, bias_ref, sink_ref, o_ref):
    qz = qz_ref[...]
    s_c = jnp.einsum('bhd,bjd->bhj', qz, ck_ref[...].astype(BF16),
                     preferred_element_type=F32) * (HEAD_DIM ** -0.5)
    s_n = jnp.sum(qz.astype(F32) * kn_ref[...], axis=-1, keepdims=True) * (HEAD_DIM ** -0.5)
    logits = s_c + bias_ref[...][None]
    sink = sink_ref[...][None]
    m = jnp.maximum(jnp.maximum(jnp.max(logits, axis=-1, keepdims=True), s_n), sink)
    p_c = jnp.exp(logits - m)
    p_n = jnp.exp(s_n - m)
    denom = jnp.sum(p_c, axis=-1, keepdims=True) + p_n + jnp.exp(sink - m)
    o = jnp.einsum('bhj,bjd->bhd', p_c.astype(BF16), cv_ref[...].astype(BF16),
                   preferred_element_type=F32)
    o_ref[...] = (o + p_n * vn_ref[...]) / denom


def _sample_attention(q, kv, sinks, ck, cv):
    b, cw = ck.shape[0], ck.shape[1]
    bt = min(SAMPLE_ATTN_TILE, b)
    q = q.transpose(1, 0, 2).reshape(b, ATTN_W).astype(BF16)
    kv = kv.transpose(1, 0, 2).reshape(b, 2 * KV_W)
    q4 = q.reshape(b, N_KV_HEADS, GROUP, HEAD_DIM)
    eye = jnp.eye(N_KV_HEADS, dtype=q.dtype)
    qz = (q4[:, :, :, None, :] * eye[None, :, None, :, None]).reshape(b, N_HEADS, KV_W)
    kn = kv[:, :KV_W].reshape(b, 1, KV_W)
    vn = kv[:, KV_W:].reshape(b, 1, KV_W)
    dist = (cw - np.arange(cw)).astype(np.float32)
    bias = np.where(dist[None, :] < WINDOW, -_alibi_slopes()[:, None] * dist[None, :], NEG_BIG)
    o = pl.pallas_call(
        _sample_attn_kernel,
        grid=(b // bt,),
        in_specs=[pl.BlockSpec((bt, N_HEADS, KV_W), lambda i: (i, 0, 0)),
                  pl.BlockSpec((bt, cw, KV_W), lambda i: (i, 0, 0)),
                  pl.BlockSpec((bt, cw, KV_W), lambda i: (i, 0, 0)),
                  pl.BlockSpec((bt, 1, KV_W), lambda i: (i, 0, 0)),
                  pl.BlockSpec((bt, 1, KV_W), lambda i: (i, 0, 0)),
                  _const_spec((N_HEADS, cw)),
                  _const_spec((N_HEADS, 1))],
        out_specs=pl.BlockSpec((bt, N_HEADS, KV_W), lambda i: (i, 0, 0)),
        out_shape=jax.ShapeDtypeStruct((b, N_HEADS, KV_W), F32),
        compiler_params=_params(("parallel",), 32),
        name="sample_attn",
    )(qz, ck.reshape(b, cw, KV_W), cv.reshape(b, cw, KV_W), kn, vn,
      jnp.asarray(bias, F32), sinks.reshape(N_HEADS, 1))
    o4 = o.reshape(b, N_KV_HEADS, GROUP, N_KV_HEADS, HEAD_DIM)
    o_sel = jnp.stack([o4[:, g, :, g, :] for g in range(N_KV_HEADS)], axis=1)
    return o_sel.reshape(b, ATTN_W // LANES, LANES).transpose(1, 0, 2)


def _rnn_kernel(xr_ref, yr_ref, cw_ref, cb_ref, wa_ref, ba_ref, wx_ref, bx_ref, lam_ref,
                conv0_ref, h0_ref, y_ref, convo_ref, ho_ref, xp_s, a_s, b_s, h_s, *, nb, tt):
    rows = tt * nb
    halo = (CONV_W - 1) * nb

    @pl.when(pl.program_id(0) == 0)
    def _():
        xp_s[0:halo] = conv0_ref[...]
        h_s[...] = h0_ref[...]

    xp_s[halo:halo + rows] = xr_ref[...]
    nl = -lam_ref[...]
    coef = -LRU_C * (jnp.maximum(nl, 0.0) + jnp.log1p(jnp.exp(-jnp.abs(nl))))
    for n in range(N_RNN_BLOCKS):
        sl = slice(n * RNN_BLOCK, (n + 1) * RNN_BLOCK)
        xc = cb_ref[:, sl] + xp_s[0:rows, sl] * cw_ref[0:1, sl]
        for j in range(1, CONV_W):
            xc = xc + xp_s[j * nb:j * nb + rows, sl] * cw_ref[j:j + 1, sl]
        xcb = xc.astype(BF16)
        r = jax.nn.sigmoid(jnp.dot(xcb, wa_ref[n], preferred_element_type=F32) + ba_ref[:, sl])
        ig = jax.nn.sigmoid(jnp.dot(xcb, wx_ref[n], preferred_element_type=F32) + bx_ref[:, sl])
        log_a = coef[:, sl] * r
        a = jnp.exp(log_a)
        a_s[:, sl] = a
        b_s[:, sl] = jnp.sqrt(1.0 - a * a) * (ig * xc)

    def step(t, h):
        off = pl.multiple_of(t * nb, nb)
        h = a_s[pl.ds(off, nb), :] * h + b_s[pl.ds(off, nb), :]
        a_s[pl.ds(off, nb), :] = h
        return h

    h = lax.fori_loop(0, tt, step, h_s[...])
    h_s[...] = h
    yr = yr_ref[...]
    gelu = 0.5 * yr * (1.0 + jnp.tanh(np.float32(np.sqrt(2.0 / np.pi))
                                      * (yr + np.float32(0.044715) * (yr * yr * yr))))
    y_ref[...] = (a_s[...] * gelu).astype(y_ref.dtype)
    tail = xp_s[rows:rows + halo]
    convo_ref[...] = tail
    ho_ref[...] = h
    xp_s[0:halo] = tail


def _rnn_branch(xr, yr, p, conv0, h0, nb):
    rows_total = xr.shape[0]
    t_total = rows_total // nb
    tt = min(RNN_TIME_TILE, t_total)
    rows = tt * nb
    halo = (CONV_W - 1) * nb
    row_spec = pl.BlockSpec((rows, D_RNN), lambda i: (i, 0))
    vec = _const_spec((1, D_RNN))
    return pl.pallas_call(
        functools.partial(_rnn_kernel, nb=nb, tt=tt),
        grid=(t_total // tt,),
        in_specs=[row_spec, row_spec, _const_spec((CONV_W, D_RNN)), vec,
                  _const_spec((N_RNN_BLOCKS, RNN_BLOCK, RNN_BLOCK)), vec,
                  _const_spec((N_RNN_BLOCKS, RNN_BLOCK, RNN_BLOCK)), vec, vec,
                  _const_spec((halo, D_RNN)), _const_spec((nb, D_RNN))],
        out_specs=[row_spec, _const_spec((halo, D_RNN)), _const_spec((nb, D_RNN))],
        out_shape=[jax.ShapeDtypeStruct((rows_total, D_RNN), BF16),
                   jax.ShapeDtypeStruct((halo, D_RNN), F32),
                   jax.ShapeDtypeStruct((nb, D_RNN), F32)],
        scratch_shapes=[pltpu.VMEM((rows + halo, D_RNN), F32),
                        pltpu.VMEM((rows, D_RNN), F32),
                        pltpu.VMEM((rows, D_RNN), F32),
                        pltpu.VMEM((nb, D_RNN), F32)],
        compiler_params=_params(("arbitrary",), 48),
        name="rnn_branch",
    )(xr, yr, p['conv_w'], p['conv_b'], p['rg_wa'], p['rg_ba'], p['rg_wx'], p['rg_bx'],
      p['rg_lambda'], conv0, h0)


def _load_token_tiles(ref, n, lead=()):
    return [ref[lead + (pl.ds(c, n, stride=LANE_GROUPS), slice(None))] for c in range(LANE_GROUPS)]


def _store_token_tiles(ref, x, n):
    for c in range(LANE_GROUPS):
        ref[pl.ds(c, n, stride=LANE_GROUPS), :] = x[:, c * LANES:(c + 1) * LANES]


def _layer_norm(z, g, b):
    mu = jnp.mean(z, axis=-1, keepdims=True)
    zc = z - mu
    var = jnp.mean(zc * zc, axis=-1, keepdims=True)
    return zc * lax.rsqrt(var + LN_EPS) * g + b


def _mix_kernel(o_ref, y_ref, ga_ref, gb_ref, x_ref, wao_ref, wro_ref, wo_ref, g1_ref, b1_ref,
                wrt_ref, br_ref, cnt0_ref, x1_ref, tope_ref, gate_ref, rank_ref, cnt_ref, runs_ref,
                run_s, *, alpha, tm):
    @pl.when(pl.program_id(0) == 0)
    def _():
        run_s[...] = cnt0_ref[...]

    o_attn = jnp.concatenate([o_ref[c].astype(BF16) for c in range(ATTN_W // LANES)], axis=1)
    att = jnp.dot(o_attn, wao_ref[...], preferred_element_type=F32)
    rn = jnp.dot(y_ref[...], wro_ref[...], preferred_element_type=F32)
    merged = jax.nn.sigmoid(ga_ref[...]) * att + jax.nn.sigmoid(gb_ref[...]) * rn
    mixed = jnp.dot(merged.astype(BF16), wo_ref[...], preferred_element_type=F32)
    x1 = _layer_norm(alpha * x_ref[...] + mixed, g1_ref[...], b1_ref[...])
    _store_token_tiles(x1_ref, x1, tm)

    logits = lax.dot_general(wrt_ref[...], x1.astype(BF16), (((1,), (1,)), ((), ())),
                             preferred_element_type=F32) + br_ref[...]
    eidx = lax.broadcasted_iota(jnp.int32, (N_EXPERTS, tm), 0)
    vals, idxs, hots = [], [], []
    for _ in range(TOP_K):
        m = jnp.max(logits, axis=0, keepdims=True)
        idx = jnp.min(jnp.where(logits == m, eidx, N_EXPERTS), axis=0, keepdims=True)
        hot = eidx == idx
        logits = jnp.where(hot, -jnp.inf, logits)
        vals.append(m)
        idxs.append(idx)
        hots.append(hot)
    exps = [jnp.exp(v - vals[0]) for v in vals]
    total = exps[0] + exps[1] + exps[2] + exps[3]
    gate_ref[...] = jnp.concatenate([e / total for e in exps], axis=0)
    tope_ref[...] = jnp.concatenate(idxs, axis=0)

    assigned = sum(h.astype(F32) for h in hots)
    earlier = (lax.broadcasted_iota(jnp.int32, (tm, tm), 0)
               < lax.broadcasted_iota(jnp.int32, (tm, tm), 1)).astype(BF16)
    before = jnp.dot(assigned.astype(BF16), earlier, preferred_element_type=F32) + run_s[...]
    ranks = [jnp.sum(jnp.where(h, before, 0.0), axis=0, keepdims=True) for h in hots]
    rank_ref[...] = jnp.concatenate(ranks, axis=0).astype(jnp.int32)
    runs_ref[0] = run_s[...]
    run_s[...] = run_s[...] + jnp.sum(assigned, axis=1, keepdims=True)
    cnt_ref[...] = run_s[...]


def _mix(o_attn, y_rnn, ga, gb, x, p, cnt0, alpha):
    n = x.shape[0]
    tm = min(ROW_TILE, n)
    row = lambda w: pl.BlockSpec((tm, w), lambda i: (i, 0))
    tok = pl.BlockSpec((TOP_K, tm), lambda i: (0, i))
    vec = _const_spec((1, D_MODEL))
    cnt = _const_spec((N_EXPERTS, 1))
    return pl.pallas_call(
        functools.partial(_mix_kernel, alpha=alpha, tm=tm),
        grid=(n // tm,),
        in_specs=[pl.BlockSpec((ATTN_W // LANES, tm, LANES), lambda i: (0, i, 0)),
                  row(D_RNN), row(D_MODEL), row(D_MODEL), row(D_MODEL),
                  _const_spec((ATTN_W, D_MODEL)), _const_spec((D_RNN, D_MODEL)),
                  _const_spec((D_MODEL, D_MODEL)), vec, vec,
                  _const_spec((N_EXPERTS, D_MODEL)), cnt, cnt],
        out_specs=[pl.BlockSpec((tm * LANE_GROUPS, LANES), lambda i: (i, 0)), tok, tok, tok, cnt,
                   pl.BlockSpec((1, N_EXPERTS, 1), lambda i: (i, 0, 0))],
        out_shape=[jax.ShapeDtypeStruct((n * LANE_GROUPS, LANES), F32),
                   jax.ShapeDtypeStruct((TOP_K, n), jnp.int32),
                   jax.ShapeDtypeStruct((TOP_K, n), F32),
                   jax.ShapeDtypeStruct((TOP_K, n), jnp.int32),
                   jax.ShapeDtypeStruct((N_EXPERTS, 1), F32),
                   jax.ShapeDtypeStruct((n // tm, N_EXPERTS, 1), F32)],
        scratch_shapes=[pltpu.VMEM((N_EXPERTS, 1), F32)],
        compiler_params=_params(("arbitrary",), 48),
        name="mix_norm_router",
    )(o_attn, y_rnn, ga, gb, x, p['w_attn_out'], p['w_rnn_out'], p['w_out'], p['ln1_g'],
      p['ln1_b'], p['w_router_t'], p['b_router'], cnt0)


def _tile_copy(src, src_tok, dst, dst_tok, sem):
    rows = lambda t: pl.ds(t * LANE_GROUPS if isinstance(t, int)
                           else pl.multiple_of(t * LANE_GROUPS, LANE_GROUPS), LANE_GROUPS)
    return pltpu.make_async_copy(src.at[rows(src_tok)], dst.at[rows(dst_tok)], sem)


_start_copy = lambda cp, k: cp.start(priority=k % 2)
_wait_copy = lambda cp, k: cp.wait()


def _dispatch_kernel(desta_ref, destb_ref, zstart_ref, zlen_ref, xa_ref, xb_ref, xs_ref, stage, sems,
                     zeros, zsem, *, na, nb, tm, nseg):
    i = pl.program_id(0)
    last = pl.num_programs(0) - 1
    slot = i % 2
    steps_a = na // tm
    piece = RUN_CHUNK * LANE_GROUPS

    def start_tile(dest_ref, n, step):
        base = step * tm

        def body(r, c):
            for k in range(TOP_K):
                _start_copy(_tile_copy(stage.at[slot], r, xs_ref, dest_ref[k * n + base + r],
                                       sems.at[slot]), k)
            return c
        lax.fori_loop(0, tm, body, 0, unroll=ISSUE_UNROLL)

    def wait_tile(s):
        def body(r, c):
            for k in range(TOP_K):
                _tile_copy(stage.at[s], 0, xs_ref, 0, sems.at[s]).wait()
            return c
        lax.fori_loop(0, tm, body, 0)

    def zero_fill(act):
        def segment(g, c):
            start, length = zstart_ref[g], zlen_ref[g]
            whole = length // RUN_CHUNK

            def pieces(j, c2):
                d = pl.multiple_of((start + j * RUN_CHUNK) * LANE_GROUPS, LANE_GROUPS)
                act(pltpu.make_async_copy(zeros, xs_ref.at[pl.ds(d, piece)], zsem), 0)
                return c2

            def singles(j, c2):
                d = pl.multiple_of((start + whole * RUN_CHUNK + j) * LANE_GROUPS, LANE_GROUPS)
                act(pltpu.make_async_copy(zeros.at[pl.ds(0, LANE_GROUPS)],
                                          xs_ref.at[pl.ds(d, LANE_GROUPS)], zsem), 1)
                return c2
            lax.fori_loop(0, whole, pieces, 0)
            lax.fori_loop(0, length - whole * RUN_CHUNK, singles, 0)
            return c
        lax.fori_loop(0, nseg, segment, 0)

    @pl.when(i == 0)
    def _():
        zeros[...] = jnp.zeros_like(zeros)
        zero_fill(_start_copy)

    @pl.when(i >= 2)
    def _():
        wait_tile(slot)

    @pl.when(i < steps_a)
    def _():
        stage[slot] = xa_ref[...]
        start_tile(desta_ref, na, i)

    @pl.when(i >= steps_a)
    def _():
        stage[slot] = xb_ref[...]
        start_tile(destb_ref, nb, i - steps_a)

    @pl.when(i == last)
    def _():
        @pl.when(i >= 1)
        def _():
            wait_tile(1 - slot)
        wait_tile(slot)
        zero_fill(_wait_copy)


def _dispatch(xa_tiles, dest_a, xb_tiles, dest_b, rows, zero_plan):
    na, nb = xa_tiles.shape[0] // LANE_GROUPS, xb_tiles.shape[0] // LANE_GROUPS
    tm = min(GATHER_TILE, na, nb)
    steps_a, steps_b = na // tm, nb // tm
    zstart, zlen = zero_plan
    blk = (tm * LANE_GROUPS, LANES)
    return pl.pallas_call(
        functools.partial(_dispatch_kernel, na=na, nb=nb, tm=tm, nseg=zstart.shape[0]),
        grid_spec=pltpu.PrefetchScalarGridSpec(
            num_scalar_prefetch=4,
            grid=(steps_a + steps_b,),
            in_specs=[pl.BlockSpec(blk, lambda i, *_: (jnp.minimum(i, steps_a - 1), 0)),
                      pl.BlockSpec(blk, lambda i, *_: (jnp.maximum(i - steps_a, 0), 0))],
            out_specs=pl.BlockSpec(memory_space=pl.ANY),
            scratch_shapes=[pltpu.VMEM((2,) + blk, F32),
                            pltpu.SemaphoreType.DMA((2,)),
                            pltpu.VMEM((RUN_CHUNK * LANE_GROUPS, LANES), F32),
                            pltpu.SemaphoreType.DMA(())]),
        out_shape=jax.ShapeDtypeStruct((rows * LANE_GROUPS, LANES), F32),
        compiler_params=_params(("arbitrary",), 32),
        name="moe_dispatch",
    )(dest_a, dest_b, zstart, zlen, xa_tiles, xb_tiles)


def _expert_kernel(be_ref, next_ref, nused_ref, x_ref, wu_hbm, bu_ref, wd_hbm, bd_ref, y_ref,
                   wu_f, wd_f, wu_s, wd_s, slot_s, sems, *, layer):
    i = pl.program_id(0)

    def weight_copies(e, slot):
        return (pltpu.make_async_copy(wu_hbm.at[layer, e], wu_f.at[slot], sems.at[slot, 0]),
                pltpu.make_async_copy(wd_hbm.at[layer, e], wd_f.at[slot], sems.at[slot, 1]))

    @pl.when(i < nused_ref[0])
    def _():
        e = be_ref[i]

        @pl.when(i == 0)
        def _():
            slot_s[0] = 0
            for cp in weight_copies(e, 0):
                cp.start()

        @pl.when(jnp.logical_or(i == 0, e != be_ref[jnp.maximum(i - 1, 0)]))
        def _():
            @pl.when(i > 0)
            def _():
                slot_s[0] = 1 - slot_s[0]
            slot = slot_s[0]
            for cp in weight_copies(e, slot):
                cp.wait()
            nxt = next_ref[i]

            @pl.when(nxt >= 0)
            def _():
                for cp in weight_copies(nxt, 1 - slot):
                    cp.start()
            wu_s[...] = wu_f[slot].astype(BF16)
            wd_s[...] = wd_f[slot].astype(BF16)

        xb = jnp.concatenate([c.astype(BF16) for c in _load_token_tiles(x_ref, MOE_BLOCK)], axis=1)
        h = jnp.dot(xb, wu_s[...], preferred_element_type=F32) + bu_ref[0, 0]
        glu = jnp.minimum(h[:, :D_FF], SWIGLU_LIMIT)
        lin = jnp.clip(h[:, D_FF:], -SWIGLU_LIMIT, SWIGLU_LIMIT)
        act = glu * jax.nn.sigmoid(SWIGLU_ALPHA * glu) * (lin + 1.0)
        y = jnp.dot(act.astype(BF16), wd_s[...], preferred_element_type=F32) + bd_ref[0, 0]
        _store_token_tiles(y_ref, y, MOE_BLOCK)

    @pl.when(i >= nused_ref[0])
    def _():
        y_ref[...] = jnp.zeros_like(y_ref)


def _experts(xs, block_e, next_e, nused, layer, w_up, b_up, w_down, b_down):
    rows = xs.shape[0] // LANE_GROUPS
    nblk = rows // MOE_BLOCK
    depth = w_up.shape[0]
    tile_blk = (MOE_BLOCK * LANE_GROUPS, LANES)
    blk = lambda i, be, nx, nu: (jnp.minimum(i, nu[0] - 1), 0)
    out_blk = lambda i, be, nx, nu: (i, 0)
    exp3 = lambda i, be, nx, nu: (layer, be[i], 0, 0)
    return pl.pallas_call(
        functools.partial(_expert_kernel, layer=layer),
        grid_spec=pltpu.PrefetchScalarGridSpec(
            num_scalar_prefetch=3,
            grid=(nblk,),
            in_specs=[pl.BlockSpec(tile_blk, blk),
                      pl.BlockSpec(memory_space=pl.ANY),
                      pl.BlockSpec((1, 1, 1, 2 * D_FF), exp3),
                      pl.BlockSpec(memory_space=pl.ANY),
                      pl.BlockSpec((1, 1, 1, D_MODEL), exp3)],
            out_specs=pl.BlockSpec(tile_blk, out_blk),
            scratch_shapes=[pltpu.VMEM((2, D_MODEL, 2 * D_FF), F32),
                            pltpu.VMEM((2, D_FF, D_MODEL), F32),
                            pltpu.VMEM((D_MODEL, 2 * D_FF), BF16),
                            pltpu.VMEM((D_FF, D_MODEL), BF16),
                            pltpu.SMEM((1,), jnp.int32),
                            pltpu.SemaphoreType.DMA((2, 2))]),
        out_shape=jax.ShapeDtypeStruct(xs.shape, F32),
        compiler_params=_params(("arbitrary",), 56),
        name="moe_experts",
    )(block_e, next_e, nused, xs, w_up, b_up.reshape(depth, N_EXPERTS, 1, 2 * D_FF), w_down,
      b_down.reshape(depth, N_EXPERTS, 1, D_MODEL))


def _combine_kernel(src_ref, nch_ref, loc_ref, yb_ref, x1_ref, gate_ref, col_ref, g2_ref, b2_ref,
                    x2_ref, buf, sems, *, tm, stage, alpha):
    i = pl.program_id(0)
    slot = i % 2
    piece = RUN_CHUNK * LANE_GROUPS

    def runs(step, buf_slot, act):
        def one_expert(e, prio):
            idx = step * N_EXPERTS + e
            src0, loc0 = src_ref[idx], loc_ref[idx]

            def body(j, c):
                s = pl.multiple_of((src0 + j * RUN_CHUNK) * LANE_GROUPS, LANE_GROUPS)
                d = pl.multiple_of((loc0 + j * RUN_CHUNK) * LANE_GROUPS, piece)
                act(pltpu.make_async_copy(yb_ref.at[pl.ds(s, piece)],
                                          buf.at[buf_slot, pl.ds(d, piece)], sems.at[buf_slot]), prio)
                return c
            lax.fori_loop(0, nch_ref[idx], body, 0)

        def pair(m, c):
            one_expert(2 * m, 0)
            one_expert(2 * m + 1, 1)
            return c
        lax.fori_loop(0, N_EXPERTS // 2, pair, 0)

    @pl.when(i == 0)
    def _():
        buf[...] = jnp.zeros_like(buf)
        runs(0, 0, _start_copy)

    @pl.when(i + 1 < pl.num_programs(0))
    def _():
        runs(i + 1, 1 - slot, _start_copy)

    runs(i, slot, _wait_copy)
    staged = jnp.concatenate([c.astype(BF16) for c in _load_token_tiles(buf, stage, lead=(slot,))],
                             axis=1)
    gate, col = gate_ref[...], col_ref[...]
    pos = lax.broadcasted_iota(jnp.int32, (tm, stage), 1)
    weights = jnp.where(col[:, 0:1] == pos, gate[:, 0:1], 0.0)
    for k in range(1, TOP_K):
        weights = weights + jnp.where(col[:, k:k + 1] == pos, gate[:, k:k + 1], 0.0)
    y = jnp.dot(weights.astype(BF16), staged, preferred_element_type=F32)
    x1 = jnp.concatenate(_load_token_tiles(x1_ref, tm), axis=1)
    x2_ref[...] = _layer_norm(alpha * x1 + y, g2_ref[...], b2_ref[...])


def _combine(yb, x1_tiles, runs_plan, gate_rows, col_rows, tm, ln_g, ln_b, alpha):
    n = x1_tiles.shape[0] // LANE_GROUPS
    stage = -(-(TOP_K * tm + N_EXPERTS * (RUN_CHUNK - 1)) // LANES) * LANES
    vec = pl.BlockSpec((1, D_MODEL), lambda i, *_: (0, 0))
    tok = pl.BlockSpec((tm, TOP_K), lambda i, *_: (i, 0))
    return pl.pallas_call(
        functools.partial(_combine_kernel, tm=tm, stage=stage, alpha=alpha),
        grid_spec=pltpu.PrefetchScalarGridSpec(
            num_scalar_prefetch=3,
            grid=(n // tm,),
            in_specs=[pl.BlockSpec(memory_space=pl.ANY),
                      pl.BlockSpec((tm * LANE_GROUPS, LANES), lambda i, *_: (i, 0)),
                      tok, tok, vec, vec],
            out_specs=pl.BlockSpec((tm, D_MODEL), lambda i, *_: (i, 0)),
            scratch_shapes=[pltpu.VMEM((2, stage * LANE_GROUPS, LANES), F32),
                            pltpu.SemaphoreType.DMA((2,))]),
        out_shape=jax.ShapeDtypeStruct((n, D_MODEL), F32),
        compiler_params=_params(("arbitrary",), 48),
        name="moe_combine",
    )(*runs_plan, yb, x1_tiles, gate_rows, col_rows, ln_g, ln_b)


def _combine_plan(top_e, rank, runs, counts_after, pad_start, tm):
    nt = runs.shape[0]
    tile_cnt = jnp.concatenate([runs[1:], counts_after[None]], axis=0) - runs
    nch = (tile_cnt + RUN_CHUNK - 1) // RUN_CHUNK
    room = nch * RUN_CHUNK
    loc = jnp.cumsum(room, axis=1) - room
    src = pad_start[None, :] + runs
    eids = jnp.arange(N_EXPERTS, dtype=jnp.int32)
    base = (loc - runs)[None, :, None, :]
    hit = top_e.reshape(TOP_K, nt, tm)[..., None] == eids
    col = jnp.sum(jnp.where(hit, base, 0), axis=-1).reshape(TOP_K, nt * tm) + rank
    flat = lambda a: a.reshape(-1).astype(jnp.int32)
    return (flat(src), flat(nch), flat(loc)), col.T.astype(jnp.int32)


def _moe_layout(counts, groups):
    n_assign = sum(g[0].shape[1] for g in groups) * TOP_K
    nblk = -(-(n_assign + N_EXPERTS * (MOE_BLOCK - 1)) // MOE_BLOCK) + 1
    padded = (counts + MOE_BLOCK - 1) // MOE_BLOCK * MOE_BLOCK
    pad_end = jnp.cumsum(padded)
    pad_start = pad_end - padded
    nused = pad_end[-1] // MOE_BLOCK

    def expert_of(row):
        ended = (pad_end[None, :] <= row[:, None]).astype(jnp.int32)
        return jnp.minimum(jnp.sum(ended, axis=1), N_EXPERTS - 1)

    blk = jnp.arange(nblk, dtype=jnp.int32)
    block_e = expert_of(jnp.minimum(blk, nused - 1) * MOE_BLOCK)
    eids = jnp.arange(N_EXPERTS, dtype=jnp.int32)
    lookup = lambda table, idx: jnp.sum(jnp.where(idx[..., None] == eids, table, 0), axis=-1)
    after = lookup(pad_end, block_e)
    next_e = jnp.where(after < pad_end[-1], expert_of(after), -1).astype(jnp.int32)
    dests = [(lookup(pad_start, top_e) + rank).reshape(-1).astype(jnp.int32)
             for top_e, rank in groups]
    zero_plan = (jnp.concatenate([pad_start + counts, pad_end[-1:]]).astype(jnp.int32),
                 jnp.concatenate([padded - counts, nblk * MOE_BLOCK - pad_end[-1:]]).astype(jnp.int32))
    return (nblk, block_e.astype(jnp.int32), next_e, nused.astype(jnp.int32).reshape(1), dests,
            pad_start, zero_plan)


def _temporal(x, p, attend, conv0, h0, nb, cnt0, alpha):
    q, kv, xr, yr, ga, gb = _project(x, p['w_in'])
    o_attn = attend(q, kv)
    y_rnn, conv_new, h_new = _rnn_branch(xr, yr, p, conv0, h0, nb)
    x1, top_e, gate, rank, cnt, runs = _mix(o_attn, y_rnn, ga, gb, x, p, cnt0, alpha)
    route = (top_e, gate, rank, runs.reshape(-1, N_EXPERTS).astype(jnp.int32))
    return x1, route, cnt, kv, conv_new, h_new


def kernel(x_prompt, x_sample, cache_k, cache_v, state_conv, state_h, w_in, attn_sinks, conv_w,
           conv_b, rg_wa, rg_ba, rg_wx, rg_bx, rg_lambda, w_attn_out, w_rnn_out, w_out, ln1_g,
           ln1_b, w_router, b_router, w_up, b_up, w_down, b_down, ln2_g, ln2_b):
    depth = w_in.shape[0]
    alpha = float((2 * depth) ** 0.25)
    bp, seq, _ = x_prompt.shape
    bs = x_sample.shape[0]
    cw = cache_k.shape[2]
    halo = CONV_W - 1

    xp = x_prompt.transpose(1, 0, 2).reshape(seq * bp, D_MODEL)
    xs = x_sample.reshape(bs, D_MODEL)
    outs = [[] for _ in range(8)]
    zeros_conv = jnp.zeros((halo * bp, D_RNN), F32)
    zeros_h = jnp.zeros((bp, D_RNN), F32)
    zeros_cnt = jnp.zeros((N_EXPERTS, 1), F32)

    for l in range(depth):
        p = {
            'w_in': w_in[l].astype(BF16),
            'conv_w': conv_w[l], 'conv_b': conv_b[l].reshape(1, D_RNN),
            'rg_wa': rg_wa[l].astype(BF16), 'rg_ba': rg_ba[l].reshape(1, D_RNN),
            'rg_wx': rg_wx[l].astype(BF16), 'rg_bx': rg_bx[l].reshape(1, D_RNN),
            'rg_lambda': rg_lambda[l].reshape(1, D_RNN),
            'w_attn_out': w_attn_out[l].astype(BF16), 'w_rnn_out': w_rnn_out[l].astype(BF16),
            'w_out': w_out[l].astype(BF16),
            'ln1_g': ln1_g[l].reshape(1, D_MODEL), 'ln1_b': ln1_b[l].reshape(1, D_MODEL),
            'w_router_t': w_router[l].T.astype(BF16),
            'b_router': b_router[l].reshape(N_EXPERTS, 1),
        }
        sinks = attn_sinks[l]
        g2, b2 = ln2_g[l].reshape(1, D_MODEL), ln2_b[l].reshape(1, D_MODEL)

        x1p, route_p, cnt_p, kv_p, conv_p, h_p = _temporal(
            xp, p, lambda q, kv: _prompt_attention(q, kv, sinks, bp, seq),
            zeros_conv, zeros_h, bp, zeros_cnt, alpha)
        conv0_s = state_conv[l].transpose(1, 0, 2).reshape(halo * bs, D_RNN)
        x1s, route_s, cnt_s, kv_s, conv_s, h_s = _temporal(
            xs, p, lambda q, kv: _sample_attention(q, kv, sinks, cache_k[l], cache_v[l]),
            conv0_s, state_h[l], bs, cnt_p, alpha)

        (te_p, gt_p, rk_p, runs_p), (te_s, gt_s, rk_s, runs_s) = route_p, route_s
        counts_p = cnt_p.reshape(N_EXPERTS).astype(jnp.int32)
        counts = cnt_s.reshape(N_EXPERTS).astype(jnp.int32)
        nblk, block_e, next_e, nused, (dest_p, dest_s), pad_start, zero_plan = _moe_layout(
            counts, [(te_p, rk_p), (te_s, rk_s)])
        sorted_x = _dispatch(x1p, dest_p, x1s, dest_s, nblk * MOE_BLOCK, zero_plan)
        yb = _experts(sorted_x, block_e, next_e, nused, l, w_up, b_up, w_down, b_down)
        tm_p, tm_s = min(ROW_TILE, seq * bp), min(ROW_TILE, bs)
        plan_p, col_p = _combine_plan(te_p, rk_p, runs_p, counts_p, pad_start, tm_p)
        plan_s, col_s = _combine_plan(te_s, rk_s, runs_s, counts, pad_start, tm_s)
        xp = _combine(yb, x1p, plan_p, gt_p.T, col_p, tm_p, g2, b2, alpha)
        xs = _combine(yb, x1s, plan_s, gt_s.T, col_s, tm_s, g2, b2, alpha)

        kv_p4 = kv_p[:, (seq - cw) * bp:].reshape(2, cw, bp, N_KV_HEADS, HEAD_DIM)
        outs[0].append(kv_p4[0].transpose(1, 0, 2, 3))
        outs[1].append(kv_p4[1].transpose(1, 0, 2, 3))
        outs[2].append(conv_p.reshape(halo, bp, D_RNN).transpose(1, 0, 2))
        outs[3].append(h_p)
        kv_s4 = kv_s.reshape(2, bs, 1, N_KV_HEADS, HEAD_DIM)
        outs[4].append(jnp.concatenate([cache_k[l], kv_s4[0]], axis=1)[:, -cw:])
        outs[5].append(jnp.concatenate([cache_v[l], kv_s4[1]], axis=1)[:, -cw:])
        outs[6].append(conv_s.reshape(halo, bs, D_RNN).transpose(1, 0, 2))
        outs[7].append(h_s)

    y_prompt = xp.reshape(seq, bp, D_MODEL).transpose(1, 0, 2)
    y_sample = xs.reshape(bs, 1, D_MODEL)
    return (y_prompt, y_sample) + tuple(jnp.stack(o) for o in outs)
```

```python
import functools

import numpy as np
import jax
import jax.numpy as jnp
from jax import lax
from jax.experimental import pallas as pl
from jax.experimental.pallas import tpu as pltpu

F32 = jnp.float32
BF16 = jnp.bfloat16

D_MODEL = 1024
N_HEADS = 16
N_KV_HEADS = 2
HEAD_DIM = 64
GROUP = N_HEADS // N_KV_HEADS
WINDOW = 128
ATTN_W = N_HEADS * HEAD_DIM
KV_W = N_KV_HEADS * HEAD_DIM
D_RNN = 1280
RNN_BLOCK = 128
N_RNN_BLOCKS = D_RNN // RNN_BLOCK
CONV_W = 4
LRU_C = 8.0
N_EXPERTS = 32
TOP_K = 4
D_FF = 1024
SWIGLU_LIMIT = 7.0
SWIGLU_ALPHA = 1.702
LN_EPS = 1e-5
PAST_LEN = 8192
PROJ_OFFS = (0, ATTN_W, ATTN_W + 2 * KV_W, ATTN_W + 2 * KV_W + D_RNN,
             ATTN_W + 2 * KV_W + 2 * D_RNN, ATTN_W + 2 * KV_W + 2 * D_RNN + D_MODEL,
             ATTN_W + 2 * KV_W + 2 * D_RNN + 2 * D_MODEL)
PROJ_W = PROJ_OFFS[-1]
NEG_BIG = -1e30
LANES = 128
LANE_GROUPS = D_MODEL // LANES

ROW_TILE = 256
RNN_TIME_TILE = 64
MOE_BLOCK = 512
GATHER_TILE = 128
ISSUE_UNROLL = 4
RUN_CHUNK = 8
QUERY_SPLIT = 128
SAMPLE_ATTN_TILE = 32
MIB = 1 << 20


def _alibi_slopes():
    h = np.arange(1, N_HEADS + 1, dtype=np.float32)
    return (np.float32(2.0) ** (np.float32(-8.0) * h / np.float32(N_HEADS))).astype(np.float32)


def _params(semantics, vmem_mib):
    return pltpu.CompilerParams(dimension_semantics=semantics, vmem_limit_bytes=vmem_mib * MIB)


def _const_spec(shape):
    nd = len(shape)
    return pl.BlockSpec(shape, lambda *_: (0,) * nd)


def _proj_kernel(x_ref, w_ref, q_ref, kv_ref, xr_ref, yr_ref, ga_ref, gb_ref):
    xb = x_ref[...].astype(BF16)
    outs = (q_ref, kv_ref, xr_ref, yr_ref, ga_ref, gb_ref)
    for n, o_ref in enumerate(outs):
        w = w_ref[0, :, PROJ_OFFS[n]:PROJ_OFFS[n + 1]]
        res = jnp.dot(xb, w, preferred_element_type=F32)
        if len(o_ref.shape) == 3:
            for c in range(o_ref.shape[0]):
                o_ref[c] = res[:, c * LANES:(c + 1) * LANES]
        else:
            o_ref[...] = res


def _project(x, w_in_bf, layer):
    n = x.shape[0]
    tm = min(ROW_TILE, n)
    widths = [PROJ_OFFS[i + 1] - PROJ_OFFS[i] for i in range(6)]
    row = lambda w: (pl.BlockSpec((tm, w), lambda i: (i, 0)), jax.ShapeDtypeStruct((n, w), F32))
    stack = lambda w: (pl.BlockSpec((w // LANES, tm, LANES), lambda i: (0, i, 0)),
                       jax.ShapeDtypeStruct((w // LANES, n, LANES), F32))
    outs = [stack(widths[0]), stack(widths[1])] + [row(w) for w in widths[2:]]
    return pl.pallas_call(
        _proj_kernel,
        grid=(n // tm,),
        in_specs=[pl.BlockSpec((tm, D_MODEL), lambda i: (i, 0)),
                  pl.BlockSpec((1, D_MODEL, PROJ_W), lambda i: (layer, 0, 0))],
        out_specs=[o[0] for o in outs],
        out_shape=[o[1] for o in outs],
        compiler_params=_params(("parallel",), 56),
        name="in_proj",
    )(x, w_in_bf)


def _prompt_attn_kernel(sink_ref, q_ref, kvp_ref, kvc_ref, o_ref, *, nb):
    tb = pl.program_id(0)
    seq_rows = pl.ds(pl.program_id(1), WINDOW, stride=nb)
    kcat = jnp.concatenate([kvp_ref[0, seq_rows, :], kvc_ref[0, seq_rows, :]], axis=0).astype(BF16)
    vcat = jnp.concatenate([kvp_ref[1, seq_rows, :], kvc_ref[1, seq_rows, :]], axis=0).astype(BF16)
    band = QUERY_SPLIT + WINDOW
    qi = lax.broadcasted_iota(jnp.int32, (QUERY_SPLIT, band), 0)
    kj = lax.broadcasted_iota(jnp.int32, (QUERY_SPLIT, band), 1)
    dist = qi - kj + WINDOW
    in_window = jnp.where(dist >= 0, jnp.where(dist < WINDOW, 1, 0), 0)
    neg_dists = []
    for r0 in range(0, WINDOW, QUERY_SPLIT):
        exists = jnp.where(kj + r0 >= WINDOW, 1, jnp.where(tb > 0, 1, 0))
        neg_dists.append(jnp.where(in_window * exists > 0, -dist.astype(F32), NEG_BIG))
    slopes = _alibi_slopes()
    scale = HEAD_DIM ** -0.5
    heads_per_chunk = LANES // HEAD_DIM
    for c in range(ATTN_W // LANES):
        qc = (q_ref[c, seq_rows, :] * scale).astype(BF16)
        outs = []
        for j in range(heads_per_chunk):
            h = c * heads_per_chunk + j
            g = h // GROUP
            sink = sink_ref[h]
            parts = []
            for s_idx, r0 in enumerate(range(0, WINDOW, QUERY_SPLIT)):
                qh = qc[r0:r0 + QUERY_SPLIT, j * HEAD_DIM:(j + 1) * HEAD_DIM]
                kg = kcat[r0:r0 + band, g * HEAD_DIM:(g + 1) * HEAD_DIM]
                vg = vcat[r0:r0 + band, g * HEAD_DIM:(g + 1) * HEAD_DIM]
                s = lax.dot_general(qh, kg, (((1,), (1,)), ((), ())), preferred_element_type=F32)
                logits = s + float(slopes[h]) * neg_dists[s_idx]
                m = jnp.maximum(jnp.max(logits, axis=1, keepdims=True), sink)
                p = jnp.exp(logits - m)
                denom = jnp.sum(p, axis=1, keepdims=True) + jnp.exp(sink - m)
                parts.append(jnp.dot(p.astype(BF16), vg, preferred_element_type=F32) / denom)
            outs.append(jnp.concatenate(parts, axis=0))
        o_ref[c, seq_rows, :] = jnp.concatenate(outs, axis=1)


def _prompt_attention(q, kv, sinks, batch, seq):
    nblk = seq // WINDOW
    rows = WINDOW * batch
    cur = lambda t, b: (0, t, 0)
    prev = lambda t, b: (0, jnp.maximum(t - 1, 0), 0)
    return pl.pallas_call(
        functools.partial(_prompt_attn_kernel, nb=batch),
        grid=(nblk, batch),
        in_specs=[pl.BlockSpec(memory_space=pltpu.SMEM),
                  pl.BlockSpec((q.shape[0], rows, LANES), cur),
                  pl.BlockSpec((kv.shape[0], rows, LANES), prev),
                  pl.BlockSpec((kv.shape[0], rows, LANES), cur)],
        out_specs=pl.BlockSpec((q.shape[0], rows, LANES), cur),
        out_shape=jax.ShapeDtypeStruct(q.shape, F32),
        compiler_params=_params(("parallel", "arbitrary"), 40),
        name="prompt_attn",
    )(sinks, q, kv, kv)


def _sample_attn_kernel(qz_ref, ck_ref, cv_ref, kn_ref, vn_ref, bias_ref, sink_ref, o_ref):
    qz = qz_ref[...]
    s_c = jnp.einsum('bhd,bjd->bhj', qz, ck_ref[...].astype(BF16),
                     preferred_element_type=F32) * (HEAD_DIM ** -0.5)
    s_n = jnp.sum(qz.astype(F32) * kn_ref[...], axis=-1, keepdims=True) * (HEAD_DIM ** -0.5)
    logits = s_c + bias_ref[...][None]
    sink = sink_ref[...][None]
    m = jnp.maximum(jnp.maximum(jnp.max(logits, axis=-1, keepdims=True), s_n), sink)
    p_c = jnp.exp(logits - m)
    p_n = jnp.exp(s_n - m)
    denom = jnp.sum(p_c, axis=-1, keepdims=True) + p_n + jnp.exp(sink - m)
    o = jnp.einsum('bhj,bjd->bhd', p_c.astype(BF16), cv_ref[...].astype(BF16),
                   preferred_element_type=F32)
    o_ref[...] = (o + p_n * vn_ref[...]) / denom


def _sample_attention(q, kv, sinks, ck, cv):
    b, cw = ck.shape[0], ck.shape[1]
    bt = min(SAMPLE_ATTN_TILE, b)
    q = q.transpose(1, 0, 2).reshape(b, ATTN_W).astype(BF16)
    kv = kv.transpose(1, 0, 2).reshape(b, 2 * KV_W)
    q4 = q.reshape(b, N_KV_HEADS, GROUP, HEAD_DIM)
    eye = jnp.eye(N_KV_HEADS, dtype=q.dtype)
    qz = (q4[:, :, :, None, :] * eye[None, :, None, :, None]).reshape(b, N_HEADS, KV_W)
    kn = kv[:, :KV_W].reshape(b, 1, KV_W)
    vn = kv[:, KV_W:].reshape(b, 1, KV_W)
    dist = (cw - np.arange(cw)).astype(np.float32)
    bias = np.where(dist[None, :] < WINDOW, -_alibi_slopes()[:, None] * dist[None, :], NEG_BIG)
    o = pl.pallas_call(
        _sample_attn_kernel,
        grid=(b // bt,),
        in_specs=[pl.BlockSpec((bt, N_HEADS, KV_W), lambda i: (i, 0, 0)),
                  pl.BlockSpec((bt, cw, KV_W), lambda i: (i, 0, 0)),
                  pl.BlockSpec((bt, cw, KV_W), lambda i: (i, 0, 0)),
                  pl.BlockSpec((bt, 1, KV_W), lambda i: (i, 0, 0)),
                  pl.BlockSpec((bt, 1, KV_W), lambda i: (i, 0, 0)),
                  _const_spec((N_HEADS, cw)),
                  _const_spec((N_HEADS, 1))],
        out_specs=pl.BlockSpec((bt, N_HEADS, KV_W), lambda i: (i, 0, 0)),
        out_shape=jax.ShapeDtypeStruct((b, N_HEADS, KV_W), F32),
        compiler_params=_params(("parallel",), 32),
        name="sample_attn",
    )(qz, ck.reshape(b, cw, KV_W), cv.reshape(b, cw, KV_W), kn, vn,
      jnp.asarray(bias, F32), sinks.reshape(N_HEADS, 1))
    o4 = o.reshape(b, N_KV_HEADS, GROUP, N_KV_HEADS, HEAD_DIM)
    o_sel = jnp.stack([o4[:, g, :, g, :] for g in range(N_KV_HEADS)], axis=1)
    return o_sel.reshape(b, ATTN_W // LANES, LANES).transpose(1, 0, 2)


def _rnn_kernel(xr_ref, yr_ref, cw_ref, cb_ref, wa_ref, ba_ref, wx_ref, bx_ref, lam_ref,
                conv0_ref, h0_ref, y_ref, convo_ref, ho_ref, xp_s, a_s, b_s, h_s, *, nb, tt):
    rows = tt * nb
    halo = (CONV_W - 1) * nb

    @pl.when(pl.program_id(0) == 0)
    def _():
        xp_s[0:halo] = conv0_ref[...]
        h_s[...] = h0_ref[...]

    xp_s[halo:halo + rows] = xr_ref[...]
    nl = -lam_ref[...]
    coef = -LRU_C * (jnp.maximum(nl, 0.0) + jnp.log1p(jnp.exp(-jnp.abs(nl))))
    for n in range(N_RNN_BLOCKS):
        sl = slice(n * RNN_BLOCK, (n + 1) * RNN_BLOCK)
        xc = cb_ref[:, sl] + xp_s[0:rows, sl] * cw_ref[0:1, sl]
        for j in range(1, CONV_W):
            xc = xc + xp_s[j * nb:j * nb + rows, sl] * cw_ref[j:j + 1, sl]
        xcb = xc.astype(BF16)
        r = jax.nn.sigmoid(jnp.dot(xcb, wa_ref[n], preferred_element_type=F32) + ba_ref[:, sl])
        ig = jax.nn.sigmoid(jnp.dot(xcb, wx_ref[n], preferred_element_type=F32) + bx_ref[:, sl])
        log_a = coef[:, sl] * r
        a = jnp.exp(log_a)
        a_s[:, sl] = a
        b_s[:, sl] = jnp.sqrt(1.0 - a * a) * (ig * xc)

    def step(t, h):
        off = pl.multiple_of(t * nb, nb)
        h = a_s[pl.ds(off, nb), :] * h + b_s[pl.ds(off, nb), :]
        a_s[pl.ds(off, nb), :] = h
        return h

    h = lax.fori_loop(0, tt, step, h_s[...])
    h_s[...] = h
    yr = yr_ref[...]
    gelu = 0.5 * yr * (1.0 + jnp.tanh(np.float32(np.sqrt(2.0 / np.pi))
                                      * (yr + np.float32(0.044715) * (yr * yr * yr))))
    y_ref[...] = (a_s[...] * gelu).astype(y_ref.dtype)
    tail = xp_s[rows:rows + halo]
    convo_ref[...] = tail
    ho_ref[...] = h
    xp_s[0:halo] = tail


def _rnn_branch(xr, yr, p, conv0, h0, nb):
    rows_total = xr.shape[0]
    t_total = rows_total // nb
    tt = min(RNN_TIME_TILE, t_total)
    rows = tt * nb
    halo = (CONV_W - 1) * nb
    row_spec = pl.BlockSpec((rows, D_RNN), lambda i: (i, 0))
    vec = _const_spec((1, D_RNN))
    return pl.pallas_call(
        functools.partial(_rnn_kernel, nb=nb, tt=tt),
        grid=(t_total // tt,),
        in_specs=[row_spec, row_spec, _const_spec((CONV_W, D_RNN)), vec,
                  _const_spec((N_RNN_BLOCKS, RNN_BLOCK, RNN_BLOCK)), vec,
                  _const_spec((N_RNN_BLOCKS, RNN_BLOCK, RNN_BLOCK)), vec, vec,
                  _const_spec((halo, D_RNN)), _const_spec((nb, D_RNN))],
        out_specs=[row_spec, _const_spec((halo, D_RNN)), _const_spec((nb, D_RNN))],
        out_shape=[jax.ShapeDtypeStruct((rows_total, D_RNN), BF16),
                   jax.ShapeDtypeStruct((halo, D_RNN), F32),
                   jax.ShapeDtypeStruct((nb, D_RNN), F32)],
        scratch_shapes=[pltpu.VMEM((rows + halo, D_RNN), F32),
                        pltpu.VMEM((rows, D_RNN), F32),
                        pltpu.VMEM((rows, D_RNN), F32),
                        pltpu.VMEM((nb, D_RNN), F32)],
        compiler_params=_params(("arbitrary",), 48),
        name="rnn_branch",
    )(xr, yr, p['conv_w'], p['conv_b'], p['rg_wa'], p['rg_ba'], p['rg_wx'], p['rg_bx'],
      p['rg_lambda'], conv0, h0)


def _load_token_tiles(ref, n, lead=()):
    return [ref[lead + (pl.ds(c, n, stride=LANE_GROUPS), slice(None))] for c in range(LANE_GROUPS)]


def _store_token_tiles(ref, x, n):
    for c in range(LANE_GROUPS):
        ref[pl.ds(c, n, stride=LANE_GROUPS), :] = x[:, c * LANES:(c + 1) * LANES]


def _layer_norm(z, g, b):
    mu = jnp.mean(z, axis=-1, keepdims=True)
    zc = z - mu
    var = jnp.mean(zc * zc, axis=-1, keepdims=True)
    return zc * lax.rsqrt(var + LN_EPS) * g + b


def _mix_kernel(o_ref, y_ref, ga_ref, gb_ref, x_ref, wao_ref, wro_ref, wo_ref, g1_ref, b1_ref,
                wrt_ref, br_ref, cnt0_ref, x1_ref, tope_ref, gate_ref, rank_ref, cnt_ref, runs_ref,
                run_s, *, alpha, tm):
    @pl.when(pl.program_id(0) == 0)
    def _():
        run_s[...] = cnt0_ref[...]

    o_attn = jnp.concatenate([o_ref[c].astype(BF16) for c in range(ATTN_W // LANES)], axis=1)
    att = jnp.dot(o_attn, wao_ref[...], preferred_element_type=F32)
    rn = jnp.dot(y_ref[...], wro_ref[...], preferred_element_type=F32)
    merged = jax.nn.sigmoid(ga_ref[...]) * att + jax.nn.sigmoid(gb_ref[...]) * rn
    mixed = jnp.dot(merged.astype(BF16), wo_ref[...], preferred_element_type=F32)
    x1 = _layer_norm(alpha * x_ref[...] + mixed, g1_ref[...], b1_ref[...])
    _store_token_tiles(x1_ref, x1, tm)

    logits = lax.dot_general(wrt_ref[...], x1.astype(BF16), (((1,), (1,)), ((), ())),
                             preferred_element_type=F32) + br_ref[...]
    eidx = lax.broadcasted_iota(jnp.int32, (N_EXPERTS, tm), 0)
    vals, idxs, hots = [], [], []
    for _ in range(TOP_K):
        m = jnp.max(logits, axis=0, keepdims=True)
        idx = jnp.min(jnp.where(logits == m, eidx, N_EXPERTS), axis=0, keepdims=True)
        hot = eidx == idx
        logits = jnp.where(hot, -jnp.inf, logits)
        vals.append(m)
        idxs.append(idx)
        hots.append(hot)
    exps = [jnp.exp(v - vals[0]) for v in vals]
    total = exps[0] + exps[1] + exps[2] + exps[3]
    gate_ref[...] = jnp.concatenate([e / total for e in exps], axis=0)
    tope_ref[...] = jnp.concatenate(idxs, axis=0)

    assigned = sum(h.astype(F32) for h in hots)
    earlier = (lax.broadcasted_iota(jnp.int32, (tm, tm), 0)
               < lax.broadcasted_iota(jnp.int32, (tm, tm), 1)).astype(BF16)
    before = jnp.dot(assigned.astype(BF16), earlier, preferred_element_type=F32) + run_s[...]
    ranks = [jnp.sum(jnp.where(h, before, 0.0), axis=0, keepdims=True) for h in hots]
    rank_ref[...] = jnp.concatenate(ranks, axis=0).astype(jnp.int32)
    runs_ref[0] = run_s[...]
    run_s[...] = run_s[...] + jnp.sum(assigned, axis=1, keepdims=True)
    cnt_ref[...] = run_s[...]


def _mix(o_attn, y_rnn, ga, gb, x, p, cnt0, alpha):
    n = x.shape[0]
    tm = min(ROW_TILE, n)
    row = lambda w: pl.BlockSpec((tm, w), lambda i: (i, 0))
    tok = pl.BlockSpec((TOP_K, tm), lambda i: (0, i))
    vec = _const_spec((1, D_MODEL))
    cnt = _const_spec((N_EXPERTS, 1))
    return pl.pallas_call(
        functools.partial(_mix_kernel, alpha=alpha, tm=tm),
        grid=(n // tm,),
        in_specs=[pl.BlockSpec((ATTN_W // LANES, tm, LANES), lambda i: (0, i, 0)),
                  row(D_RNN), row(D_MODEL), row(D_MODEL), row(D_MODEL),
                  _const_spec((ATTN_W, D_MODEL)), _const_spec((D_RNN, D_MODEL)),
                  _const_spec((D_MODEL, D_MODEL)), vec, vec,
                  _const_spec((N_EXPERTS, D_MODEL)), cnt, cnt],
        out_specs=[pl.BlockSpec((tm * LANE_GROUPS, LANES), lambda i: (i, 0)), tok, tok, tok, cnt,
                   pl.BlockSpec((1, N_EXPERTS, 1), lambda i: (i, 0, 0))],
        out_shape=[jax.ShapeDtypeStruct((n * LANE_GROUPS, LANES), F32),
                   jax.ShapeDtypeStruct((TOP_K, n), jnp.int32),
                   jax.ShapeDtypeStruct((TOP_K, n), F32),
                   jax.ShapeDtypeStruct((TOP_K, n), jnp.int32),
                   jax.ShapeDtypeStruct((N_EXPERTS, 1), F32),
                   jax.ShapeDtypeStruct((n // tm, N_EXPERTS, 1), F32)],
        scratch_shapes=[pltpu.VMEM((N_EXPERTS, 1), F32)],
        compiler_params=_params(("arbitrary",), 48),
        name="mix_norm_router",
    )(o_attn, y_rnn, ga, gb, x, p['w_attn_out'], p['w_rnn_out'], p['w_out'], p['ln1_g'],
      p['ln1_b'], p['w_router_t'], p['b_router'], cnt0)


def _tile_copy(src, src_tok, dst, dst_tok, sem):
    rows = lambda t: pl.ds(t * LANE_GROUPS if isinstance(t, int)
                           else pl.multiple_of(t * LANE_GROUPS, LANE_GROUPS), LANE_GROUPS)
    return pltpu.make_async_copy(src.at[rows(src_tok)], dst.at[rows(dst_tok)], sem)


_start_copy = lambda cp, k: cp.start(priority=k % 2)
_wait_copy = lambda cp, k: cp.wait()


def _dispatch_kernel(desta_ref, destb_ref, zstart_ref, zlen_ref, xa_ref, xb_ref, xs_ref, stage, sems,
                     zeros, zsem, *, na, nb, tm, nseg):
    i = pl.program_id(0)
    last = pl.num_programs(0) - 1
    slot = i % 2
    steps_a = na // tm
    piece = RUN_CHUNK * LANE_GROUPS

    def start_tile(dest_ref, n, step):
        base = step * tm

        def body(r, c):
            for k in range(TOP_K):
                _start_copy(_tile_copy(stage.at[slot], r, xs_ref, dest_ref[k * n + base + r],
                                       sems.at[slot]), k)
            return c
        lax.fori_loop(0, tm, body, 0, unroll=ISSUE_UNROLL)

    def wait_tile(s):
        def body(r, c):
            for k in range(TOP_K):
                _tile_copy(stage.at[s], 0, xs_ref, 0, sems.at[s]).wait()
            return c
        lax.fori_loop(0, tm, body, 0)

    nsteps = pl.num_programs(0)

    def zero_fill(act, first, stride):
        def segment(q, c):
            g = first + q * stride
            start, length = zstart_ref[g], zlen_ref[g]
            whole = length // RUN_CHUNK

            def pieces(j, c2):
                d = pl.multiple_of((start + j * RUN_CHUNK) * LANE_GROUPS, LANE_GROUPS)
                act(pltpu.make_async_copy(zeros, xs_ref.at[pl.ds(d, piece)], zsem), 0)
                return c2

            def singles(j, c2):
                d = pl.multiple_of((start + whole * RUN_CHUNK + j) * LANE_GROUPS, LANE_GROUPS)
                act(pltpu.make_async_copy(zeros.at[pl.ds(0, LANE_GROUPS)],
                                          xs_ref.at[pl.ds(d, LANE_GROUPS)], zsem), 1)
                return c2
            lax.fori_loop(0, whole, pieces, 0)
            lax.fori_loop(0, length - whole * RUN_CHUNK, singles, 0)
            return c
        lax.fori_loop(0, (nseg - first + stride - 1) // stride, segment, 0)

    @pl.when(i == 0)
    def _():
        zeros[...] = jnp.zeros_like(zeros)

    zero_fill(_start_copy, i, nsteps)

    @pl.when(i >= 2)
    def _():
        wait_tile(slot)

    @pl.when(i < steps_a)
    def _():
        stage[slot] = xa_ref[...]
        start_tile(desta_ref, na, i)

    @pl.when(i >= steps_a)
    def _():
        stage[slot] = xb_ref[...]
        start_tile(destb_ref, nb, i - steps_a)

    @pl.when(i == last)
    def _():
        @pl.when(i >= 1)
        def _():
            wait_tile(1 - slot)
        wait_tile(slot)
        zero_fill(_wait_copy, 0, 1)


def _dispatch(xa_tiles, dest_a, xb_tiles, dest_b, rows, zero_plan):
    na, nb = xa_tiles.shape[0] // LANE_GROUPS, xb_tiles.shape[0] // LANE_GROUPS
    tm = min(GATHER_TILE, na, nb)
    steps_a, steps_b = na // tm, nb // tm
    zstart, zlen = zero_plan
    blk = (tm * LANE_GROUPS, LANES)
    return pl.pallas_call(
        functools.partial(_dispatch_kernel, na=na, nb=nb, tm=tm, nseg=zstart.shape[0]),
        grid_spec=pltpu.PrefetchScalarGridSpec(
            num_scalar_prefetch=4,
            grid=(steps_a + steps_b,),
            in_specs=[pl.BlockSpec(blk, lambda i, *_: (jnp.minimum(i, steps_a - 1), 0)),
                      pl.BlockSpec(blk, lambda i, *_: (jnp.maximum(i - steps_a, 0), 0))],
            out_specs=pl.BlockSpec(memory_space=pl.ANY),
            scratch_shapes=[pltpu.VMEM((2,) + blk, F32),
                            pltpu.SemaphoreType.DMA((2,)),
                            pltpu.VMEM((RUN_CHUNK * LANE_GROUPS, LANES), F32),
                            pltpu.SemaphoreType.DMA(())]),
        out_shape=jax.ShapeDtypeStruct((rows * LANE_GROUPS, LANES), F32),
        compiler_params=_params(("arbitrary",), 32),
        name="moe_dispatch",
    )(dest_a, dest_b, zstart, zlen, xa_tiles, xb_tiles)


def _expert_kernel(be_ref, next_ref, nused_ref, x_ref, wu_hbm, bu_ref, wd_hbm, bd_ref, y_ref,
                   wu_f, wd_f, wu_s, wd_s, slot_s, sems, *, layer):
    i = pl.program_id(0)

    def weight_copies(e, slot):
        return (pltpu.make_async_copy(wu_hbm.at[layer, e], wu_f.at[slot], sems.at[slot, 0]),
                pltpu.make_async_copy(wd_hbm.at[layer, e], wd_f.at[slot], sems.at[slot, 1]))

    @pl.when(i < nused_ref[0])
    def _():
        e = be_ref[i]

        @pl.when(i == 0)
        def _():
            slot_s[0] = 0
            for cp in weight_copies(e, 0):
                cp.start()

        @pl.when(jnp.logical_or(i == 0, e != be_ref[jnp.maximum(i - 1, 0)]))
        def _():
            @pl.when(i > 0)
            def _():
                slot_s[0] = 1 - slot_s[0]
            slot = slot_s[0]
            for cp in weight_copies(e, slot):
                cp.wait()
            nxt = next_ref[i]

            @pl.when(nxt >= 0)
            def _():
                for cp in weight_copies(nxt, 1 - slot):
                    cp.start()
            wu_s[...] = wu_f[slot].astype(BF16)
            wd_s[...] = wd_f[slot].astype(BF16)

        xb = jnp.concatenate([c.astype(BF16) for c in _load_token_tiles(x_ref, MOE_BLOCK)], axis=1)
        h = jnp.dot(xb, wu_s[...], preferred_element_type=F32) + bu_ref[0, 0]
        glu = jnp.minimum(h[:, :D_FF], SWIGLU_LIMIT)
        lin = jnp.clip(h[:, D_FF:], -SWIGLU_LIMIT, SWIGLU_LIMIT)
        act = glu * jax.nn.sigmoid(SWIGLU_ALPHA * glu) * (lin + 1.0)
        y = jnp.dot(act.astype(BF16), wd_s[...], preferred_element_type=F32) + bd_ref[0, 0]
        _store_token_tiles(y_ref, y, MOE_BLOCK)

    @pl.when(i >= nused_ref[0])
    def _():
        y_ref[...] = jnp.zeros_like(y_ref)


def _experts(xs, block_e, next_e, nused, layer, w_up, b_up, w_down, b_down):
    rows = xs.shape[0] // LANE_GROUPS
    nblk = rows // MOE_BLOCK
    depth = w_up.shape[0]
    tile_blk = (MOE_BLOCK * LANE_GROUPS, LANES)
    blk = lambda i, be, nx, nu: (jnp.minimum(i, nu[0] - 1), 0)
    out_blk = lambda i, be, nx, nu: (i, 0)
    exp3 = lambda i, be, nx, nu: (layer, be[i], 0, 0)
    return pl.pallas_call(
        functools.partial(_expert_kernel, layer=layer),
        grid_spec=pltpu.PrefetchScalarGridSpec(
            num_scalar_prefetch=3,
            grid=(nblk,),
            in_specs=[pl.BlockSpec(tile_blk, blk),
                      pl.BlockSpec(memory_space=pl.ANY),
                      pl.BlockSpec((1, 1, 1, 2 * D_FF), exp3),
                      pl.BlockSpec(memory_space=pl.ANY),
                      pl.BlockSpec((1, 1, 1, D_MODEL), exp3)],
            out_specs=pl.BlockSpec(tile_blk, out_blk),
            scratch_shapes=[pltpu.VMEM((2, D_MODEL, 2 * D_FF), F32),
                            pltpu.VMEM((2, D_FF, D_MODEL), F32),
                            pltpu.VMEM((D_MODEL, 2 * D_FF), BF16),
                            pltpu.VMEM((D_FF, D_MODEL), BF16),
                            pltpu.SMEM((1,), jnp.int32),
                            pltpu.SemaphoreType.DMA((2, 2))]),
        out_shape=jax.ShapeDtypeStruct(xs.shape, F32),
        compiler_params=_params(("arbitrary",), 56),
        name="moe_experts",
    )(block_e, next_e, nused, xs, w_up, b_up.reshape(depth, N_EXPERTS, 1, 2 * D_FF), w_down,
      b_down.reshape(depth, N_EXPERTS, 1, D_MODEL))


def _combine_kernel(src_ref, nch_ref, loc_ref, yb_ref, x1_ref, gate_ref, col_ref, g2_ref, b2_ref,
                    x2_ref, buf, sems, *, tm, stage, alpha):
    i = pl.program_id(0)
    slot = i % 2
    piece = RUN_CHUNK * LANE_GROUPS

    def runs(step, buf_slot, act):
        def one_expert(e, prio):
            idx = step * N_EXPERTS + e
            src0, loc0 = src_ref[idx], loc_ref[idx]

            def body(j, c):
                s = pl.multiple_of((src0 + j * RUN_CHUNK) * LANE_GROUPS, LANE_GROUPS)
                d = pl.multiple_of((loc0 + j * RUN_CHUNK) * LANE_GROUPS, piece)
                act(pltpu.make_async_copy(yb_ref.at[pl.ds(s, piece)],
                                          buf.at[buf_slot, pl.ds(d, piece)], sems.at[buf_slot]), prio)
                return c
            lax.fori_loop(0, nch_ref[idx], body, 0)

        def pair(m, c):
            one_expert(2 * m, 0)
            one_expert(2 * m + 1, 1)
            return c
        lax.fori_loop(0, N_EXPERTS // 2, pair, 0)

    @pl.when(i == 0)
    def _():
        buf[...] = jnp.zeros_like(buf)
        runs(0, 0, _start_copy)

    @pl.when(i + 1 < pl.num_programs(0))
    def _():
        runs(i + 1, 1 - slot, _start_copy)

    runs(i, slot, _wait_copy)
    staged = jnp.concatenate([c.astype(BF16) for c in _load_token_tiles(buf, stage, lead=(slot,))],
                             axis=1)
    gate, col = gate_ref[...], col_ref[...]
    pos = lax.broadcasted_iota(jnp.int32, (tm, stage), 1)
    weights = jnp.where(col[:, 0:1] == pos, gate[:, 0:1], 0.0)
    for k in range(1, TOP_K):
        weights = weights + jnp.where(col[:, k:k + 1] == pos, gate[:, k:k + 1], 0.0)
    y = jnp.dot(weights.astype(BF16), staged, preferred_element_type=F32)
    x1 = jnp.concatenate(_load_token_tiles(x1_ref, tm), axis=1)
    x2_ref[...] = _layer_norm(alpha * x1 + y, g2_ref[...], b2_ref[...])


def _combine(yb, x1_tiles, runs_plan, gate_rows, col_rows, tm, ln_g, ln_b, alpha):
    n = x1_tiles.shape[0] // LANE_GROUPS
    stage = -(-(TOP_K * tm + N_EXPERTS * (RUN_CHUNK - 1)) // LANES) * LANES
    vec = pl.BlockSpec((1, D_MODEL), lambda i, *_: (0, 0))
    tok = pl.BlockSpec((tm, TOP_K), lambda i, *_: (i, 0))
    return pl.pallas_call(
        functools.partial(_combine_kernel, tm=tm, stage=stage, alpha=alpha),
        grid_spec=pltpu.PrefetchScalarGridSpec(
            num_scalar_prefetch=3,
            grid=(n // tm,),
            in_specs=[pl.BlockSpec(memory_space=pl.ANY),
                      pl.BlockSpec((tm * LANE_GROUPS, LANES), lambda i, *_: (i, 0)),
                      tok, tok, vec, vec],
            out_specs=pl.BlockSpec((tm, D_MODEL), lambda i, *_: (i, 0)),
            scratch_shapes=[pltpu.VMEM((2, stage * LANE_GROUPS, LANES), F32),
                            pltpu.SemaphoreType.DMA((2,))]),
        out_shape=jax.ShapeDtypeStruct((n, D_MODEL), F32),
        compiler_params=_params(("arbitrary",), 48),
        name="moe_combine",
    )(*runs_plan, yb, x1_tiles, gate_rows, col_rows, ln_g, ln_b)


def _combine_plan(top_e, rank, runs, counts_after, pad_start, tm):
    nt = runs.shape[0]
    tile_cnt = jnp.concatenate([runs[1:], counts_after[None]], axis=0) - runs
    nch = (tile_cnt + RUN_CHUNK - 1) // RUN_CHUNK
    room = nch * RUN_CHUNK
    loc = jnp.cumsum(room, axis=1) - room
    src = pad_start[None, :] + runs
    eids = jnp.arange(N_EXPERTS, dtype=jnp.int32)
    base = (loc - runs)[None, :, None, :]
    hit = top_e.reshape(TOP_K, nt, tm)[..., None] == eids
    col = jnp.sum(jnp.where(hit, base, 0), axis=-1).reshape(TOP_K, nt * tm) + rank
    flat = lambda a: a.reshape(-1).astype(jnp.int32)
    return (flat(src), flat(nch), flat(loc)), col.T.astype(jnp.int32)


def _moe_layout(counts, groups):
    n_assign = sum(g[0].shape[1] for g in groups) * TOP_K
    nblk = -(-(n_assign + N_EXPERTS * (MOE_BLOCK - 1)) // MOE_BLOCK) + 1
    padded = (counts + MOE_BLOCK - 1) // MOE_BLOCK * MOE_BLOCK
    pad_end = jnp.cumsum(padded)
    pad_start = pad_end - padded
    nused = pad_end[-1] // MOE_BLOCK

    def expert_of(row):
        ended = (pad_end[None, :] <= row[:, None]).astype(jnp.int32)
        return jnp.minimum(jnp.sum(ended, axis=1), N_EXPERTS - 1)

    blk = jnp.arange(nblk, dtype=jnp.int32)
    block_e = expert_of(jnp.minimum(blk, nused - 1) * MOE_BLOCK)
    eids = jnp.arange(N_EXPERTS, dtype=jnp.int32)
    lookup = lambda table, idx: jnp.sum(jnp.where(idx[..., None] == eids, table, 0), axis=-1)
    after = lookup(pad_end, block_e)
    next_e = jnp.where(after < pad_end[-1], expert_of(after), -1).astype(jnp.int32)
    dests = [(lookup(pad_start, top_e) + rank).reshape(-1).astype(jnp.int32)
             for top_e, rank in groups]
    zero_plan = (jnp.concatenate([pad_start + counts, pad_end[-1:]]).astype(jnp.int32),
                 jnp.concatenate([padded - counts, nblk * MOE_BLOCK - pad_end[-1:]]).astype(jnp.int32))
    return (nblk, block_e.astype(jnp.int32), next_e, nused.astype(jnp.int32).reshape(1), dests,
            pad_start, zero_plan)


def _temporal(x, p, attend, conv0, h0, nb, cnt0, alpha):
    q, kv, xr, yr, ga, gb = _project(x, p['w_in'], p['layer'])
    o_attn = attend(q, kv)
    y_rnn, conv_new, h_new = _rnn_branch(xr, yr, p, conv0, h0, nb)
    x1, top_e, gate, rank, cnt, runs = _mix(o_attn, y_rnn, ga, gb, x, p, cnt0, alpha)
    route = (top_e, gate, rank, runs.reshape(-1, N_EXPERTS).astype(jnp.int32))
    return x1, route, cnt, kv, conv_new, h_new


def kernel(x_prompt, x_sample, cache_k, cache_v, state_conv, state_h, w_in, attn_sinks, conv_w,
           conv_b, rg_wa, rg_ba, rg_wx, rg_bx, rg_lambda, w_attn_out, w_rnn_out, w_out, ln1_g,
           ln1_b, w_router, b_router, w_up, b_up, w_down, b_down, ln2_g, ln2_b):
    depth = w_in.shape[0]
    alpha = float((2 * depth) ** 0.25)
    bp, seq, _ = x_prompt.shape
    bs = x_sample.shape[0]
    cw = cache_k.shape[2]
    halo = CONV_W - 1

    xp = x_prompt.transpose(1, 0, 2).reshape(seq * bp, D_MODEL)
    xs = x_sample.reshape(bs, D_MODEL)
    outs = [[] for _ in range(8)]
    zeros_conv = jnp.zeros((halo * bp, D_RNN), F32)
    zeros_h = jnp.zeros((bp, D_RNN), F32)
    zeros_cnt = jnp.zeros((N_EXPERTS, 1), F32)
    w_in_bf = w_in.astype(BF16)

    for l in range(depth):
        p = {
            'w_in': w_in_bf, 'layer': l,
            'conv_w': conv_w[l], 'conv_b': conv_b[l].reshape(1, D_RNN),
            'rg_wa': rg_wa[l].astype(BF16), 'rg_ba': rg_ba[l].reshape(1, D_RNN),
            'rg_wx': rg_wx[l].astype(BF16), 'rg_bx': rg_bx[l].reshape(1, D_RNN),
            'rg_lambda': rg_lambda[l].reshape(1, D_RNN),
            'w_attn_out': w_attn_out[l].astype(BF16), 'w_rnn_out': w_rnn_out[l].astype(BF16),
            'w_out': w_out[l].astype(BF16),
            'ln1_g': ln1_g[l].reshape(1, D_MODEL), 'ln1_b': ln1_b[l].reshape(1, D_MODEL),
            'w_router_t': w_router[l].T.astype(BF16),
            'b_router': b_router[l].reshape(N_EXPERTS, 1),
        }
        sinks = attn_sinks[l]
        g2, b2 = ln2_g[l].reshape(1, D_MODEL), ln2_b[l].reshape(1, D_MODEL)

        x1p, route_p, cnt_p, kv_p, conv_p, h_p = _temporal(
            xp, p, lambda q, kv: _prompt_attention(q, kv, sinks, bp, seq),
            zeros_conv, zeros_h, bp, zeros_cnt, alpha)
        conv0_s = state_conv[l].transpose(1, 0, 2).reshape(halo * bs, D_RNN)
        x1s, route_s, cnt_s, kv_s, conv_s, h_s = _temporal(
            xs, p, lambda q, kv: _sample_attention(q, kv, sinks, cache_k[l], cache_v[l]),
            conv0_s, state_h[l], bs, cnt_p, alpha)

        (te_p, gt_p, rk_p, runs_p), (te_s, gt_s, rk_s, runs_s) = route_p, route_s
        counts_p = cnt_p.reshape(N_EXPERTS).astype(jnp.int32)
        counts = cnt_s.reshape(N_EXPERTS).astype(jnp.int32)
        nblk, block_e, next_e, nused, (dest_p, dest_s), pad_start, zero_plan = _moe_layout(
            counts, [(te_p, rk_p), (te_s, rk_s)])
        sorted_x = _dispatch(x1p, dest_p, x1s, dest_s, nblk * MOE_BLOCK, zero_plan)
        yb = _experts(sorted_x, block_e, next_e, nused, l, w_up, b_up, w_down, b_down)
        tm_p, tm_s = min(ROW_TILE, seq * bp), min(ROW_TILE, bs)
        plan_p, col_p = _combine_plan(te_p, rk_p, runs_p, counts_p, pad_start, tm_p)
        plan_s, col_s = _combine_plan(te_s, rk_s, runs_s, counts, pad_start, tm_s)
        xp = _combine(yb, x1p, plan_p, gt_p.T, col_p, tm_p, g2, b2, alpha)
        xs = _combine(yb, x1s, plan_s, gt_s.T, col_s, tm_s, g2, b2, alpha)

        kv_p4 = kv_p[:, (seq - cw) * bp:].reshape(2, cw, bp, N_KV_HEADS, HEAD_DIM)
        outs[0].append(kv_p4[0].transpose(1, 0, 2, 3))
        outs[1].append(kv_p4[1].transpose(1, 0, 2, 3))
        outs[2].append(conv_p.reshape(halo, bp, D_RNN).transpose(1, 0, 2))
        outs[3].append(h_p)
        kv_s4 = kv_s.reshape(2, bs, 1, N_KV_HEADS, HEAD_DIM)
        outs[4].append(jnp.concatenate([cache_k[l], kv_s4[0]], axis=1)[:, -cw:])
        outs[5].append(jnp.concatenate([cache_v[l], kv_s4[1]], axis=1)[:, -cw:])
        outs[6].append(conv_s.reshape(halo, bs, D_RNN).transpose(1, 0, 2))
        outs[7].append(h_s)

    y_prompt = xp.reshape(seq, bp, D_MODEL).transpose(1, 0, 2)
    y_sample = xs.reshape(bs, 1, D_MODEL)
    return (y_prompt, y_sample) + tuple(jnp.stack(o) for o in outs)
```

```python
import functools

import numpy as np
import jax
import jax.numpy as jnp
from jax import lax
from jax.experimental import pallas as pl
from jax.experimental.pallas import tpu as pltpu

F32 = jnp.float32
BF16 = jnp.bfloat16

D_MODEL = 1024
N_HEADS = 16
N_KV_HEADS = 2
HEAD_DIM = 64
GROUP = N_HEADS // N_KV_HEADS
WINDOW = 128
ATTN_W = N_HEADS * HEAD_DIM
KV_W = N_KV_HEADS * HEAD_DIM
D_RNN = 1280
RNN_BLOCK = 128
N_RNN_BLOCKS = D_RNN // RNN_BLOCK
CONV_W = 4
LRU_C = 8.0
N_EXPERTS = 32
TOP_K = 4
D_FF = 1024
SWIGLU_LIMIT = 7.0
SWIGLU_ALPHA = 1.702
LN_EPS = 1e-5
PAST_LEN = 8192
PROJ_OFFS = (0, ATTN_W, ATTN_W + 2 * KV_W, ATTN_W + 2 * KV_W + D_RNN,
             ATTN_W + 2 * KV_W + 2 * D_RNN, ATTN_W + 2 * KV_W + 2 * D_RNN + D_MODEL,
             ATTN_W + 2 * KV_W + 2 * D_RNN + 2 * D_MODEL)
PROJ_W = PROJ_OFFS[-1]
NEG_BIG = -1e30
LANES = 128
LANE_GROUPS = D_MODEL // LANES

ROW_TILE = 256
RNN_TIME_TILE = 64
MOE_BLOCK = 512
GATHER_TILE = 128
ISSUE_UNROLL = 4
RUN_CHUNK = 8
QUERY_SPLIT = 128
SAMPLE_ATTN_TILE = 32
MIB = 1 << 20


def _alibi_slopes():
    h = np.arange(1, N_HEADS + 1, dtype=np.float32)
    return (np.float32(2.0) ** (np.float32(-8.0) * h / np.float32(N_HEADS))).astype(np.float32)


def _params(semantics, vmem_mib):
    return pltpu.CompilerParams(dimension_semantics=semantics, vmem_limit_bytes=vmem_mib * MIB)


def _const_spec(shape):
    nd = len(shape)
    return pl.BlockSpec(shape, lambda *_: (0,) * nd)


def _proj_kernel(x_ref, w_ref, q_ref, kv_ref, xr_ref, yr_ref, ga_ref, gb_ref):
    xb = x_ref[...].astype(BF16)
    outs = (q_ref, kv_ref, xr_ref, yr_ref, ga_ref, gb_ref)
    for n, o_ref in enumerate(outs):
        w = w_ref[0, :, PROJ_OFFS[n]:PROJ_OFFS[n + 1]]
        res = jnp.dot(xb, w, preferred_element_type=F32)
        if len(o_ref.shape) == 3:
            for c in range(o_ref.shape[0]):
                o_ref[c] = res[:, c * LANES:(c + 1) * LANES]
        else:
            o_ref[...] = res


def _project(x, w_in_bf, layer):
    n = x.shape[0]
    tm = min(ROW_TILE, n)
    widths = [PROJ_OFFS[i + 1] - PROJ_OFFS[i] for i in range(6)]
    row = lambda w: (pl.BlockSpec((tm, w), lambda i: (i, 0)), jax.ShapeDtypeStruct((n, w), F32))
    stack = lambda w: (pl.BlockSpec((w // LANES, tm, LANES), lambda i: (0, i, 0)),
                       jax.ShapeDtypeStruct((w // LANES, n, LANES), F32))
    outs = [stack(widths[0]), stack(widths[1])] + [row(w) for w in widths[2:]]
    return pl.pallas_call(
        _proj_kernel,
        grid=(n // tm,),
        in_specs=[pl.BlockSpec((tm, D_MODEL), lambda i: (i, 0)),
                  pl.BlockSpec((1, D_MODEL, PROJ_W), lambda i: (layer, 0, 0))],
        out_specs=[o[0] for o in outs],
        out_shape=[o[1] for o in outs],
        compiler_params=_params(("parallel",), 56),
        name="in_proj",
    )(x, w_in_bf)


def _prompt_attn_kernel(sink_ref, q_ref, kvp_ref, kvc_ref, o_ref, *, nb):
    tb = pl.program_id(0)
    seq_rows = pl.ds(pl.program_id(1), WINDOW, stride=nb)
    kcat = jnp.concatenate([kvp_ref[0, seq_rows, :], kvc_ref[0, seq_rows, :]], axis=0).astype(BF16)
    vcat = jnp.concatenate([kvp_ref[1, seq_rows, :], kvc_ref[1, seq_rows, :]], axis=0).astype(BF16)
    band = QUERY_SPLIT + WINDOW
    qi = lax.broadcasted_iota(jnp.int32, (QUERY_SPLIT, band), 0)
    kj = lax.broadcasted_iota(jnp.int32, (QUERY_SPLIT, band), 1)
    dist = qi - kj + WINDOW
    in_window = jnp.where(dist >= 0, jnp.where(dist < WINDOW, 1, 0), 0)
    neg_dists = []
    for r0 in range(0, WINDOW, QUERY_SPLIT):
        exists = jnp.where(kj + r0 >= WINDOW, 1, jnp.where(tb > 0, 1, 0))
        neg_dists.append(jnp.where(in_window * exists > 0, -dist.astype(F32), NEG_BIG))
    slopes = _alibi_slopes()
    scale = HEAD_DIM ** -0.5
    heads_per_chunk = LANES // HEAD_DIM
    for c in range(ATTN_W // LANES):
        qc = (q_ref[c, seq_rows, :] * scale).astype(BF16)
        outs = []
        for j in range(heads_per_chunk):
            h = c * heads_per_chunk + j
            g = h // GROUP
            sink = sink_ref[h]
            parts = []
            for s_idx, r0 in enumerate(range(0, WINDOW, QUERY_SPLIT)):
                qh = qc[r0:r0 + QUERY_SPLIT, j * HEAD_DIM:(j + 1) * HEAD_DIM]
                kg = kcat[r0:r0 + band, g * HEAD_DIM:(g + 1) * HEAD_DIM]
                vg = vcat[r0:r0 + band, g * HEAD_DIM:(g + 1) * HEAD_DIM]
                s = lax.dot_general(qh, kg, (((1,), (1,)), ((), ())), preferred_element_type=F32)
                logits = s + float(slopes[h]) * neg_dists[s_idx]
                m = jnp.maximum(jnp.max(logits, axis=1, keepdims=True), sink)
                p = jnp.exp(logits - m)
                denom = jnp.sum(p, axis=1, keepdims=True) + jnp.exp(sink - m)
                parts.append(jnp.dot(p.astype(BF16), vg, preferred_element_type=F32) / denom)
            outs.append(jnp.concatenate(parts, axis=0))
        o_ref[c, seq_rows, :] = jnp.concatenate(outs, axis=1)


def _prompt_attention(q, kv, sinks, batch, seq):
    nblk = seq // WINDOW
    rows = WINDOW * batch
    cur = lambda t, b: (0, t, 0)
    prev = lambda t, b: (0, jnp.maximum(t - 1, 0), 0)
    return pl.pallas_call(
        functools.partial(_prompt_attn_kernel, nb=batch),
        grid=(nblk, batch),
        in_specs=[pl.BlockSpec(memory_space=pltpu.SMEM),
                  pl.BlockSpec((q.shape[0], rows, LANES), cur),
                  pl.BlockSpec((kv.shape[0], rows, LANES), prev),
                  pl.BlockSpec((kv.shape[0], rows, LANES), cur)],
        out_specs=pl.BlockSpec((q.shape[0], rows, LANES), cur),
        out_shape=jax.ShapeDtypeStruct(q.shape, F32),
        compiler_params=_params(("parallel", "arbitrary"), 40),
        name="prompt_attn",
    )(sinks, q, kv, kv)


def _sample_attn_kernel(qz_ref, ck_ref, cv_ref, kn_ref, vn_ref, bias_ref, sink_ref, o_ref):
    qz = qz_ref[...]
    s_c = jnp.einsum('bhd,bjd->bhj', qz, ck_ref[...].astype(BF16),
                     preferred_element_type=F32) * (HEAD_DIM ** -0.5)
    s_n = jnp.sum(qz.astype(F32) * kn_ref[...], axis=-1, keepdims=True) * (HEAD_DIM ** -0.5)
    logits = s_c + bias_ref[...][None]
    sink = sink_ref[...][None]
    m = jnp.maximum(jnp.maximum(jnp.max(logits, axis=-1, keepdims=True), s_n), sink)
    p_c = jnp.exp(logits - m)
    p_n = jnp.exp(s_n - m)
    denom = jnp.sum(p_c, axis=-1, keepdims=True) + p_n + jnp.exp(sink - m)
    o = jnp.einsum('bhj,bjd->bhd', p_c.astype(BF16), cv_ref[...].astype(BF16),
                   preferred_element_type=F32)
    o_ref[...] = (o + p_n * vn_ref[...]) / denom


def _sample_attention(q, kv, sinks, ck, cv):
    b, cw = ck.shape[0], ck.shape[1]
    bt = min(SAMPLE_ATTN_TILE, b)
    q = q.transpose(1, 0, 2).reshape(b, ATTN_W).astype(BF16)
    kv = kv.transpose(1, 0, 2).reshape(b, 2 * KV_W)
    q4 = q.reshape(b, N_KV_HEADS, GROUP, HEAD_DIM)
    eye = jnp.eye(N_KV_HEADS, dtype=q.dtype)
    qz = (q4[:, :, :, None, :] * eye[None, :, None, :, None]).reshape(b, N_HEADS, KV_W)
    kn = kv[:, :KV_W].reshape(b, 1, KV_W)
    vn = kv[:, KV_W:].reshape(b, 1, KV_W)
    dist = (cw - np.arange(cw)).astype(np.float32)
    bias = np.where(dist[None, :] < WINDOW, -_alibi_slopes()[:, None] * dist[None, :], NEG_BIG)
    o = pl.pallas_call(
        _sample_attn_kernel,
        grid=(b // bt,),
        in_specs=[pl.BlockSpec((bt, N_HEADS, KV_W), lambda i: (i, 0, 0)),
                  pl.BlockSpec((bt, cw, KV_W), lambda i: (i, 0, 0)),
                  pl.BlockSpec((bt, cw, KV_W), lambda i: (i, 0, 0)),
                  pl.BlockSpec((bt, 1, KV_W), lambda i: (i, 0, 0)),
                  pl.BlockSpec((bt, 1, KV_W), lambda i: (i, 0, 0)),
                  _const_spec((N_HEADS, cw)),
                  _const_spec((N_HEADS, 1))],
        out_specs=pl.BlockSpec((bt, N_HEADS, KV_W), lambda i: (i, 0, 0)),
        out_shape=jax.ShapeDtypeStruct((b, N_HEADS, KV_W), F32),
        compiler_params=_params(("parallel",), 32),
        name="sample_attn",
    )(qz, ck.reshape(b, cw, KV_W), cv.reshape(b, cw, KV_W), kn, vn,
      jnp.asarray(bias, F32), sinks.reshape(N_HEADS, 1))
    o4 = o.reshape(b, N_KV_HEADS, GROUP, N_KV_HEADS, HEAD_DIM)
    o_sel = jnp.stack([o4[:, g, :, g, :] for g in range(N_KV_HEADS)], axis=1)
    return o_sel.reshape(b, ATTN_W // LANES, LANES).transpose(1, 0, 2)


def _rnn_kernel(xr_ref, yr_ref, cw_ref, cb_ref, wa_ref, ba_ref, wx_ref, bx_ref, lam_ref,
                conv0_ref, h0_ref, y_ref, convo_ref, ho_ref, xp_s, a_s, b_s, h_s, *, nb, tt):
    rows = tt * nb
    halo = (CONV_W - 1) * nb

    @pl.when(pl.program_id(0) == 0)
    def _():
        xp_s[0:halo] = conv0_ref[...]
        h_s[...] = h0_ref[...]

    xp_s[halo:halo + rows] = xr_ref[...]
    nl = -lam_ref[...]
    coef = -LRU_C * (jnp.maximum(nl, 0.0) + jnp.log1p(jnp.exp(-jnp.abs(nl))))
    for n in range(N_RNN_BLOCKS):
        sl = slice(n * RNN_BLOCK, (n + 1) * RNN_BLOCK)
        xc = cb_ref[:, sl] + xp_s[0:rows, sl] * cw_ref[0:1, sl]
        for j in range(1, CONV_W):
            xc = xc + xp_s[j * nb:j * nb + rows, sl] * cw_ref[j:j + 1, sl]
        xcb = xc.astype(BF16)
        r = jax.nn.sigmoid(jnp.dot(xcb, wa_ref[n], preferred_element_type=F32) + ba_ref[:, sl])
        ig = jax.nn.sigmoid(jnp.dot(xcb, wx_ref[n], preferred_element_type=F32) + bx_ref[:, sl])
        log_a = coef[:, sl] * r
        a = jnp.exp(log_a)
        a_s[:, sl] = a
        b_s[:, sl] = jnp.sqrt(1.0 - a * a) * (ig * xc)

    def step(t, h):
        off = pl.multiple_of(t * nb, nb)
        h = a_s[pl.ds(off, nb), :] * h + b_s[pl.ds(off, nb), :]
        a_s[pl.ds(off, nb), :] = h
        return h

    h = lax.fori_loop(0, tt, step, h_s[...])
    h_s[...] = h
    yr = yr_ref[...]
    gelu = 0.5 * yr * (1.0 + jnp.tanh(np.float32(np.sqrt(2.0 / np.pi))
                                      * (yr + np.float32(0.044715) * (yr * yr * yr))))
    y_ref[...] = (a_s[...] * gelu).astype(y_ref.dtype)
    tail = xp_s[rows:rows + halo]
    convo_ref[...] = tail
    ho_ref[...] = h
    xp_s[0:halo] = tail


def _rnn_branch(xr, yr, p, conv0, h0, nb):
    rows_total = xr.shape[0]
    t_total = rows_total // nb
    tt = min(RNN_TIME_TILE, t_total)
    rows = tt * nb
    halo = (CONV_W - 1) * nb
    row_spec = pl.BlockSpec((rows, D_RNN), lambda i: (i, 0))
    vec = _const_spec((1, D_RNN))
    return pl.pallas_call(
        functools.partial(_rnn_kernel, nb=nb, tt=tt),
        grid=(t_total // tt,),
        in_specs=[row_spec, row_spec, _const_spec((CONV_W, D_RNN)), vec,
                  _const_spec((N_RNN_BLOCKS, RNN_BLOCK, RNN_BLOCK)), vec,
                  _const_spec((N_RNN_BLOCKS, RNN_BLOCK, RNN_BLOCK)), vec, vec,
                  _const_spec((halo, D_RNN)), _const_spec((nb, D_RNN))],
        out_specs=[row_spec, _const_spec((halo, D_RNN)), _const_spec((nb, D_RNN))],
        out_shape=[jax.ShapeDtypeStruct((rows_total, D_RNN), BF16),
                   jax.ShapeDtypeStruct((halo, D_RNN), F32),
                   jax.ShapeDtypeStruct((nb, D_RNN), F32)],
        scratch_shapes=[pltpu.VMEM((rows + halo, D_RNN), F32),
                        pltpu.VMEM((rows, D_RNN), F32),
                        pltpu.VMEM((rows, D_RNN), F32),
                        pltpu.VMEM((nb, D_RNN), F32)],
        compiler_params=_params(("arbitrary",), 48),
        name="rnn_branch",
    )(xr, yr, p['conv_w'], p['conv_b'], p['rg_wa'], p['rg_ba'], p['rg_wx'], p['rg_bx'],
      p['rg_lambda'], conv0, h0)


def _load_token_tiles(ref, n, lead=()):
    return [ref[lead + (pl.ds(c, n, stride=LANE_GROUPS), slice(None))] for c in range(LANE_GROUPS)]


def _store_token_tiles(ref, x, n):
    for c in range(LANE_GROUPS):
        ref[pl.ds(c, n, stride=LANE_GROUPS), :] = x[:, c * LANES:(c + 1) * LANES]


def _layer_norm(z, g, b):
    mu = jnp.mean(z, axis=-1, keepdims=True)
    zc = z - mu
    var = jnp.mean(zc * zc, axis=-1, keepdims=True)
    return zc * lax.rsqrt(var + LN_EPS) * g + b


def _mix_kernel(o_ref, y_ref, ga_ref, gb_ref, x_ref, wao_ref, wro_ref, wo_ref, g1_ref, b1_ref,
                wrt_ref, br_ref, cnt0_ref, x1_ref, tope_ref, gate_ref, rank_ref, cnt_ref, runs_ref,
                run_s, *, alpha, tm):
    @pl.when(pl.program_id(0) == 0)
    def _():
        run_s[...] = cnt0_ref[...]

    o_attn = jnp.concatenate([o_ref[c].astype(BF16) for c in range(ATTN_W // LANES)], axis=1)
    att = jnp.dot(o_attn, wao_ref[...], preferred_element_type=F32)
    rn = jnp.dot(y_ref[...], wro_ref[...], preferred_element_type=F32)
    merged = jax.nn.sigmoid(ga_ref[...]) * att + jax.nn.sigmoid(gb_ref[...]) * rn
    mixed = jnp.dot(merged.astype(BF16), wo_ref[...], preferred_element_type=F32)
    x1 = _layer_norm(alpha * x_ref[...] + mixed, g1_ref[...], b1_ref[...])
    _store_token_tiles(x1_ref, x1, tm)

    logits = lax.dot_general(wrt_ref[...], x1.astype(BF16), (((1,), (1,)), ((), ())),
                             preferred_element_type=F32) + br_ref[...]
    eidx = lax.broadcasted_iota(jnp.int32, (N_EXPERTS, tm), 0)
    vals, idxs, hots = [], [], []
    for _ in range(TOP_K):
        m = jnp.max(logits, axis=0, keepdims=True)
        idx = jnp.min(jnp.where(logits == m, eidx, N_EXPERTS), axis=0, keepdims=True)
        hot = eidx == idx
        logits = jnp.where(hot, -jnp.inf, logits)
        vals.append(m)
        idxs.append(idx)
        hots.append(hot)
    exps = [jnp.exp(v - vals[0]) for v in vals]
    total = exps[0] + exps[1] + exps[2] + exps[3]
    gate_ref[...] = jnp.concatenate([e / total for e in exps], axis=0)
    tope_ref[...] = jnp.concatenate(idxs, axis=0)

    assigned = sum(h.astype(F32) for h in hots)
    earlier = (lax.broadcasted_iota(jnp.int32, (tm, tm), 0)
               < lax.broadcasted_iota(jnp.int32, (tm, tm), 1)).astype(BF16)
    before = jnp.dot(assigned.astype(BF16), earlier, preferred_element_type=F32) + run_s[...]
    ranks = [jnp.sum(jnp.where(h, before, 0.0), axis=0, keepdims=True) for h in hots]
    rank_ref[...] = jnp.concatenate(ranks, axis=0).astype(jnp.int32)
    runs_ref[0] = run_s[...]
    run_s[...] = run_s[...] + jnp.sum(assigned, axis=1, keepdims=True)
    cnt_ref[...] = run_s[...]


def _mix(o_attn, y_rnn, ga, gb, x, p, cnt0, alpha):
    n = x.shape[0]
    tm = min(ROW_TILE, n)
    row = lambda w: pl.BlockSpec((tm, w), lambda i: (i, 0))
    tok = pl.BlockSpec((TOP_K, tm), lambda i: (0, i))
    vec = _const_spec((1, D_MODEL))
    cnt = _const_spec((N_EXPERTS, 1))
    return pl.pallas_call(
        functools.partial(_mix_kernel, alpha=alpha, tm=tm),
        grid=(n // tm,),
        in_specs=[pl.BlockSpec((ATTN_W // LANES, tm, LANES), lambda i: (0, i, 0)),
                  row(D_RNN), row(D_MODEL), row(D_MODEL), row(D_MODEL),
                  _const_spec((ATTN_W, D_MODEL)), _const_spec((D_RNN, D_MODEL)),
                  _const_spec((D_MODEL, D_MODEL)), vec, vec,
                  _const_spec((N_EXPERTS, D_MODEL)), cnt, cnt],
        out_specs=[pl.BlockSpec((tm * LANE_GROUPS, LANES), lambda i: (i, 0)), tok, tok, tok, cnt,
                   pl.BlockSpec((1, N_EXPERTS, 1), lambda i: (i, 0, 0))],
        out_shape=[jax.ShapeDtypeStruct((n * LANE_GROUPS, LANES), F32),
                   jax.ShapeDtypeStruct((TOP_K, n), jnp.int32),
                   jax.ShapeDtypeStruct((TOP_K, n), F32),
                   jax.ShapeDtypeStruct((TOP_K, n), jnp.int32),
                   jax.ShapeDtypeStruct((N_EXPERTS, 1), F32),
                   jax.ShapeDtypeStruct((n // tm, N_EXPERTS, 1), F32)],
        scratch_shapes=[pltpu.VMEM((N_EXPERTS, 1), F32)],
        compiler_params=_params(("arbitrary",), 48),
        name="mix_norm_router",
    )(o_attn, y_rnn, ga, gb, x, p['w_attn_out'], p['w_rnn_out'], p['w_out'], p['ln1_g'],
      p['ln1_b'], p['w_router_t'], p['b_router'], cnt0)


def _tile_copy(src, src_tok, dst, dst_tok, sem):
    rows = lambda t: pl.ds(t * LANE_GROUPS if isinstance(t, int)
                           else pl.multiple_of(t * LANE_GROUPS, LANE_GROUPS), LANE_GROUPS)
    return pltpu.make_async_copy(src.at[rows(src_tok)], dst.at[rows(dst_tok)], sem)


_start_copy = lambda cp, k: cp.start(priority=k % 2)
_wait_copy = lambda cp, k: cp.wait()


def _dispatch_kernel(desta_ref, destb_ref, zstart_ref, zlen_ref, xa_ref, xb_ref, xs_ref, stage, sems,
                     zeros, zsem, *, na, nb, tm, nseg):
    i = pl.program_id(0)
    last = pl.num_programs(0) - 1
    slot = i % 2
    steps_a = na // tm
    piece = RUN_CHUNK * LANE_GROUPS

    def start_tile(dest_ref, n, step):
        base = step * tm

        def body(r, c):
            for k in range(TOP_K):
                _start_copy(_tile_copy(stage.at[slot], r, xs_ref, dest_ref[k * n + base + r],
                                       sems.at[slot]), k)
            return c
        lax.fori_loop(0, tm, body, 0, unroll=ISSUE_UNROLL)

    def wait_tile(s):
        def body(r, c):
            for k in range(TOP_K):
                _tile_copy(stage.at[s], 0, xs_ref, 0, sems.at[s]).wait()
            return c
        lax.fori_loop(0, tm, body, 0)

    nsteps = pl.num_programs(0)

    def zero_fill(act, first, stride):
        def segment(q, c):
            g = first + q * stride
            start, length = zstart_ref[g], zlen_ref[g]
            whole = length // RUN_CHUNK

            def pieces(j, c2):
                d = pl.multiple_of((start + j * RUN_CHUNK) * LANE_GROUPS, LANE_GROUPS)
                act(pltpu.make_async_copy(zeros, xs_ref.at[pl.ds(d, piece)], zsem), 0)
                return c2

            def singles(j, c2):
                d = pl.multiple_of((start + whole * RUN_CHUNK + j) * LANE_GROUPS, LANE_GROUPS)
                act(pltpu.make_async_copy(zeros.at[pl.ds(0, LANE_GROUPS)],
                                          xs_ref.at[pl.ds(d, LANE_GROUPS)], zsem), 1)
                return c2
            lax.fori_loop(0, whole, pieces, 0)
            lax.fori_loop(0, length - whole * RUN_CHUNK, singles, 0)
            return c
        lax.fori_loop(0, (nseg - first + stride - 1) // stride, segment, 0)

    @pl.when(i == 0)
    def _():
        zeros[...] = jnp.zeros_like(zeros)

    zero_fill(_start_copy, i, nsteps)

    @pl.when(i >= 2)
    def _():
        wait_tile(slot)

    @pl.when(i < steps_a)
    def _():
        stage[slot] = xa_ref[...]
        start_tile(desta_ref, na, i)

    @pl.when(i >= steps_a)
    def _():
        stage[slot] = xb_ref[...]
        start_tile(destb_ref, nb, i - steps_a)

    @pl.when(i == last)
    def _():
        @pl.when(i >= 1)
        def _():
            wait_tile(1 - slot)
        wait_tile(slot)
        zero_fill(_wait_copy, 0, 1)


def _dispatch(xa_tiles, dest_a, xb_tiles, dest_b, rows, zero_plan):
    na, nb = xa_tiles.shape[0] // LANE_GROUPS, xb_tiles.shape[0] // LANE_GROUPS
    tm = min(GATHER_TILE, na, nb)
    steps_a, steps_b = na // tm, nb // tm
    zstart, zlen = zero_plan
    blk = (tm * LANE_GROUPS, LANES)
    return pl.pallas_call(
        functools.partial(_dispatch_kernel, na=na, nb=nb, tm=tm, nseg=zstart.shape[0]),
        grid_spec=pltpu.PrefetchScalarGridSpec(
            num_scalar_prefetch=4,
            grid=(steps_a + steps_b,),
            in_specs=[pl.BlockSpec(blk, lambda i, *_: (jnp.minimum(i, steps_a - 1), 0)),
                      pl.BlockSpec(blk, lambda i, *_: (jnp.maximum(i - steps_a, 0), 0))],
            out_specs=pl.BlockSpec(memory_space=pl.ANY),
            scratch_shapes=[pltpu.VMEM((2,) + blk, F32),
                            pltpu.SemaphoreType.DMA((2,)),
                            pltpu.VMEM((RUN_CHUNK * LANE_GROUPS, LANES), F32),
                            pltpu.SemaphoreType.DMA(())]),
        out_shape=jax.ShapeDtypeStruct((rows * LANE_GROUPS, LANES), F32),
        compiler_params=_params(("arbitrary",), 32),
        name="moe_dispatch",
    )(dest_a, dest_b, zstart, zlen, xa_tiles, xb_tiles)


def _dispatch_runs_kernel(src_ref, nch_ref, loc_ref, tot_ref, destb_ref, zstart_ref, zlen_ref,
                          xa_ref, cola_ref, xb_ref, xs_ref, stage_a, sems_a, stage_b, sem_b,
                          zeros, zsem, *, nb, tma, tmb, sa, steps_a, nseg):
    i = pl.program_id(0)
    nsteps = pl.num_programs(0)
    slot = i % 2
    piece = RUN_CHUNK * LANE_GROUPS

    def runs(step, s, act):
        def one_expert(e, prio):
            idx = step * N_EXPERTS + e
            src0, loc0 = src_ref[idx], loc_ref[idx]

            def body(j, c):
                d = pl.multiple_of((src0 + j * RUN_CHUNK) * LANE_GROUPS, LANE_GROUPS)
                l = pl.multiple_of((loc0 + j * RUN_CHUNK) * LANE_GROUPS, piece)
                act(pltpu.make_async_copy(stage_a.at[s, pl.ds(l, piece)],
                                          xs_ref.at[pl.ds(d, piece)], sems_a.at[s]), prio)
                return c
            lax.fori_loop(0, nch_ref[idx], body, 0)

        def pair(m, c):
            one_expert(2 * m, 0)
            one_expert(2 * m + 1, 1)
            return c
        lax.fori_loop(0, N_EXPERTS // 2, pair, 0)

    def wait_runs(step, s):
        def body(j, c):
            pltpu.make_async_copy(stage_a.at[s, pl.ds(0, piece)], xs_ref.at[pl.ds(0, piece)],
                                  sems_a.at[s]).wait()
            return c
        lax.fori_loop(0, tot_ref[step], body, 0)

    def tokens_b(step, act):
        base = step * tmb

        def body(r, c):
            for k in range(TOP_K):
                act(_tile_copy(stage_b, r, xs_ref, destb_ref[k * nb + base + r], sem_b), k)
            return c
        lax.fori_loop(0, tmb, body, 0)

    def zero_fill(act, first, stride):
        def segment(q, c):
            g = first + q * stride
            start, length = zstart_ref[g], zlen_ref[g]
            whole = length // RUN_CHUNK

            def pieces(j, c2):
                d = pl.multiple_of((start + j * RUN_CHUNK) * LANE_GROUPS, LANE_GROUPS)
                act(pltpu.make_async_copy(zeros, xs_ref.at[pl.ds(d, piece)], zsem), 0)
                return c2

            def singles(j, c2):
                d = pl.multiple_of((start + whole * RUN_CHUNK + j) * LANE_GROUPS, LANE_GROUPS)
                act(pltpu.make_async_copy(zeros.at[pl.ds(0, LANE_GROUPS)],
                                          xs_ref.at[pl.ds(d, LANE_GROUPS)], zsem), 1)
                return c2
            lax.fori_loop(0, whole, pieces, 0)
            lax.fori_loop(0, length - whole * RUN_CHUNK, singles, 0)
            return c
        lax.fori_loop(0, (nseg - first + stride - 1) // stride, segment, 0)

    @pl.when(i == 0)
    def _():
        zeros[...] = jnp.zeros_like(zeros)

    zero_fill(_start_copy, i, nsteps)

    @pl.when(i < steps_a)
    def _():
        xa = jnp.concatenate([c.astype(BF16) for c in _load_token_tiles(xa_ref, tma)], axis=1)
        col = cola_ref[...]
        pos = lax.broadcasted_iota(jnp.int32, (sa, tma), 0)
        onehot = jnp.where(col[0:1, :] == pos, 1.0, 0.0)
        for k in range(1, TOP_K):
            onehot = onehot + jnp.where(col[k:k + 1, :] == pos, 1.0, 0.0)
        staged = jnp.dot(onehot.astype(BF16), xa, preferred_element_type=F32)
        for c in range(LANE_GROUPS):
            stage_a[slot, pl.ds(c, sa, stride=LANE_GROUPS), :] = staged[:, c * LANES:(c + 1) * LANES]

        @pl.when(i > 0)
        def _():
            wait_runs(i - 1, 1 - slot)
        runs(i, slot, _start_copy)

    @pl.when(i >= steps_a)
    def _():
        @pl.when(i == steps_a)
        def _():
            wait_runs(steps_a - 1, (steps_a - 1) % 2)
        stage_b[...] = xb_ref[...]
        tokens_b(i - steps_a, _start_copy)
        tokens_b(i - steps_a, _wait_copy)

    @pl.when(i == nsteps - 1)
    def _():
        zero_fill(_wait_copy, 0, 1)


def _dispatch_runs(xa_tiles, plan_a, col_a, tma, xb_tiles, dest_b, rows, zero_plan):
    na, nb = xa_tiles.shape[0] // LANE_GROUPS, xb_tiles.shape[0] // LANE_GROUPS
    tmb = min(GATHER_TILE, nb)
    steps_a, steps_b = na // tma, nb // tmb
    sa = _staging_rows(tma)
    zstart, zlen = zero_plan
    return pl.pallas_call(
        functools.partial(_dispatch_runs_kernel, nb=nb, tma=tma, tmb=tmb, sa=sa, steps_a=steps_a,
                          nseg=zstart.shape[0]),
        grid_spec=pltpu.PrefetchScalarGridSpec(
            num_scalar_prefetch=7,
            grid=(steps_a + steps_b,),
            in_specs=[pl.BlockSpec((tma * LANE_GROUPS, LANES),
                                   lambda i, *_: (jnp.minimum(i, steps_a - 1), 0)),
                      pl.BlockSpec((TOP_K, tma), lambda i, *_: (0, jnp.minimum(i, steps_a - 1))),
                      pl.BlockSpec((tmb * LANE_GROUPS, LANES),
                                   lambda i, *_: (jnp.maximum(i - steps_a, 0), 0))],
            out_specs=pl.BlockSpec(memory_space=pl.ANY),
            scratch_shapes=[pltpu.VMEM((2, sa * LANE_GROUPS, LANES), F32),
                            pltpu.SemaphoreType.DMA((2,)),
                            pltpu.VMEM((tmb * LANE_GROUPS, LANES), F32),
                            pltpu.SemaphoreType.DMA(()),
                            pltpu.VMEM((RUN_CHUNK * LANE_GROUPS, LANES), F32),
                            pltpu.SemaphoreType.DMA(())]),
        out_shape=jax.ShapeDtypeStruct((rows * LANE_GROUPS, LANES), F32),
        compiler_params=_params(("arbitrary",), 48),
        name="moe_dispatch",
    )(*plan_a, dest_b, zstart, zlen, xa_tiles, col_a, xb_tiles)


def _staging_rows(tm):
    return -(-(TOP_K * tm + N_EXPERTS * (RUN_CHUNK - 1)) // LANES) * LANES


def _expert_kernel(be_ref, next_ref, nused_ref, x_ref, wu_hbm, bu_ref, wd_hbm, bd_ref, y_ref,
                   wu_f, wd_f, wu_s, wd_s, slot_s, sems, *, layer):
    i = pl.program_id(0)

    def weight_copies(e, slot):
        return (pltpu.make_async_copy(wu_hbm.at[layer, e], wu_f.at[slot], sems.at[slot, 0]),
                pltpu.make_async_copy(wd_hbm.at[layer, e], wd_f.at[slot], sems.at[slot, 1]))

    @pl.when(i < nused_ref[0])
    def _():
        e = be_ref[i]

        @pl.when(i == 0)
        def _():
            slot_s[0] = 0
            for cp in weight_copies(e, 0):
                cp.start()

        @pl.when(jnp.logical_or(i == 0, e != be_ref[jnp.maximum(i - 1, 0)]))
        def _():
            @pl.when(i > 0)
            def _():
                slot_s[0] = 1 - slot_s[0]
            slot = slot_s[0]
            for cp in weight_copies(e, slot):
                cp.wait()
            nxt = next_ref[i]

            @pl.when(nxt >= 0)
            def _():
                for cp in weight_copies(nxt, 1 - slot):
                    cp.start()
            wu_s[...] = wu_f[slot].astype(BF16)
            wd_s[...] = wd_f[slot].astype(BF16)

        xb = jnp.concatenate([c.astype(BF16) for c in _load_token_tiles(x_ref, MOE_BLOCK)], axis=1)
        h = jnp.dot(xb, wu_s[...], preferred_element_type=F32) + bu_ref[0, 0]
        glu = jnp.minimum(h[:, :D_FF], SWIGLU_LIMIT)
        lin = jnp.clip(h[:, D_FF:], -SWIGLU_LIMIT, SWIGLU_LIMIT)
        act = glu * jax.nn.sigmoid(SWIGLU_ALPHA * glu) * (lin + 1.0)
        y = jnp.dot(act.astype(BF16), wd_s[...], preferred_element_type=F32) + bd_ref[0, 0]
        _store_token_tiles(y_ref, y, MOE_BLOCK)

    @pl.when(i >= nused_ref[0])
    def _():
        y_ref[...] = jnp.zeros_like(y_ref)


def _experts(xs, block_e, next_e, nused, layer, w_up, b_up, w_down, b_down):
    rows = xs.shape[0] // LANE_GROUPS
    nblk = rows // MOE_BLOCK
    depth = w_up.shape[0]
    tile_blk = (MOE_BLOCK * LANE_GROUPS, LANES)
    blk = lambda i, be, nx, nu: (jnp.minimum(i, nu[0] - 1), 0)
    out_blk = lambda i, be, nx, nu: (i, 0)
    exp3 = lambda i, be, nx, nu: (layer, be[i], 0, 0)
    return pl.pallas_call(
        functools.partial(_expert_kernel, layer=layer),
        grid_spec=pltpu.PrefetchScalarGridSpec(
            num_scalar_prefetch=3,
            grid=(nblk,),
            in_specs=[pl.BlockSpec(tile_blk, blk),
                      pl.BlockSpec(memory_space=pl.ANY),
                      pl.BlockSpec((1, 1, 1, 2 * D_FF), exp3),
                      pl.BlockSpec(memory_space=pl.ANY),
                      pl.BlockSpec((1, 1, 1, D_MODEL), exp3)],
            out_specs=pl.BlockSpec(tile_blk, out_blk),
            scratch_shapes=[pltpu.VMEM((2, D_MODEL, 2 * D_FF), F32),
                            pltpu.VMEM((2, D_FF, D_MODEL), F32),
                            pltpu.VMEM((D_MODEL, 2 * D_FF), BF16),
                            pltpu.VMEM((D_FF, D_MODEL), BF16),
                            pltpu.SMEM((1,), jnp.int32),
                            pltpu.SemaphoreType.DMA((2, 2))]),
        out_shape=jax.ShapeDtypeStruct(xs.shape, F32),
        compiler_params=_params(("arbitrary",), 56),
        name="moe_experts",
    )(block_e, next_e, nused, xs, w_up, b_up.reshape(depth, N_EXPERTS, 1, 2 * D_FF), w_down,
      b_down.reshape(depth, N_EXPERTS, 1, D_MODEL))


def _combine_kernel(src_ref, nch_ref, loc_ref, tot_ref, yb_ref, x1_ref, gate_ref, col_ref, g2_ref,
                    b2_ref, x2_ref, buf, sems, *, tm, stage, alpha):
    i = pl.program_id(0)
    slot = i % 2
    piece = RUN_CHUNK * LANE_GROUPS

    def runs(step, buf_slot, act):
        def one_expert(e, prio):
            idx = step * N_EXPERTS + e
            src0, loc0 = src_ref[idx], loc_ref[idx]

            def body(j, c):
                s = pl.multiple_of((src0 + j * RUN_CHUNK) * LANE_GROUPS, LANE_GROUPS)
                d = pl.multiple_of((loc0 + j * RUN_CHUNK) * LANE_GROUPS, piece)
                act(pltpu.make_async_copy(yb_ref.at[pl.ds(s, piece)],
                                          buf.at[buf_slot, pl.ds(d, piece)], sems.at[buf_slot]), prio)
                return c
            lax.fori_loop(0, nch_ref[idx], body, 0)

        def pair(m, c):
            one_expert(2 * m, 0)
            one_expert(2 * m + 1, 1)
            return c
        lax.fori_loop(0, N_EXPERTS // 2, pair, 0)

    @pl.when(i == 0)
    def _():
        buf[...] = jnp.zeros_like(buf)
        runs(0, 0, _start_copy)

    @pl.when(i + 1 < pl.num_programs(0))
    def _():
        runs(i + 1, 1 - slot, _start_copy)

    def wait_piece(j, c):
        pltpu.make_async_copy(yb_ref.at[pl.ds(0, piece)], buf.at[slot, pl.ds(0, piece)],
                              sems.at[slot]).wait()
        return c
    lax.fori_loop(0, tot_ref[i], wait_piece, 0)
    staged = jnp.concatenate([c.astype(BF16) for c in _load_token_tiles(buf, stage, lead=(slot,))],
                             axis=1)
    gate, col = gate_ref[...], col_ref[...]
    pos = lax.broadcasted_iota(jnp.int32, (tm, stage), 1)
    weights = jnp.where(col[:, 0:1] == pos, gate[:, 0:1], 0.0)
    for k in range(1, TOP_K):
        weights = weights + jnp.where(col[:, k:k + 1] == pos, gate[:, k:k + 1], 0.0)
    y = jnp.dot(weights.astype(BF16), staged, preferred_element_type=F32)
    x1 = jnp.concatenate(_load_token_tiles(x1_ref, tm), axis=1)
    x2_ref[...] = _layer_norm(alpha * x1 + y, g2_ref[...], b2_ref[...])


def _combine(yb, x1_tiles, runs_plan, gate_rows, col_rows, tm, ln_g, ln_b, alpha):
    n = x1_tiles.shape[0] // LANE_GROUPS
    stage = _staging_rows(tm)
    vec = pl.BlockSpec((1, D_MODEL), lambda i, *_: (0, 0))
    tok = pl.BlockSpec((tm, TOP_K), lambda i, *_: (i, 0))
    return pl.pallas_call(
        functools.partial(_combine_kernel, tm=tm, stage=stage, alpha=alpha),
        grid_spec=pltpu.PrefetchScalarGridSpec(
            num_scalar_prefetch=4,
            grid=(n // tm,),
            in_specs=[pl.BlockSpec(memory_space=pl.ANY),
                      pl.BlockSpec((tm * LANE_GROUPS, LANES), lambda i, *_: (i, 0)),
                      tok, tok, vec, vec],
            out_specs=pl.BlockSpec((tm, D_MODEL), lambda i, *_: (i, 0)),
            scratch_shapes=[pltpu.VMEM((2, stage * LANE_GROUPS, LANES), F32),
                            pltpu.SemaphoreType.DMA((2,))]),
        out_shape=jax.ShapeDtypeStruct((n, D_MODEL), F32),
        compiler_params=_params(("arbitrary",), 48),
        name="moe_combine",
    )(*runs_plan, yb, x1_tiles, gate_rows, col_rows, ln_g, ln_b)


def _combine_plan(top_e, rank, runs, counts_after, pad_start, tm):
    nt = runs.shape[0]
    tile_cnt = jnp.concatenate([runs[1:], counts_after[None]], axis=0) - runs
    nch = (tile_cnt + RUN_CHUNK - 1) // RUN_CHUNK
    room = nch * RUN_CHUNK
    loc = jnp.cumsum(room, axis=1) - room
    src = pad_start[None, :] + runs
    eids = jnp.arange(N_EXPERTS, dtype=jnp.int32)
    base = (loc - runs)[None, :, None, :]
    hit = top_e.reshape(TOP_K, nt, tm)[..., None] == eids
    col = jnp.sum(jnp.where(hit, base, 0), axis=-1).reshape(TOP_K, nt * tm) + rank
    flat = lambda a: a.reshape(-1).astype(jnp.int32)
    col = col.astype(jnp.int32)
    return (flat(src), flat(nch), flat(loc), flat(jnp.sum(nch, axis=1))), col, col.T


def _moe_layout(counts, groups):
    n_assign = sum(g[0].shape[1] for g in groups) * TOP_K
    spare = RUN_CHUNK - 1
    nblk = -(-(n_assign + N_EXPERTS * (MOE_BLOCK - 1 + spare)) // MOE_BLOCK) + 1
    padded = jnp.where(counts > 0, (counts + spare + MOE_BLOCK - 1) // MOE_BLOCK * MOE_BLOCK, 0)
    pad_end = jnp.cumsum(padded)
    pad_start = pad_end - padded
    nused = pad_end[-1] // MOE_BLOCK

    def expert_of(row):
        ended = (pad_end[None, :] <= row[:, None]).astype(jnp.int32)
        return jnp.minimum(jnp.sum(ended, axis=1), N_EXPERTS - 1)

    blk = jnp.arange(nblk, dtype=jnp.int32)
    block_e = expert_of(jnp.minimum(blk, nused - 1) * MOE_BLOCK)
    eids = jnp.arange(N_EXPERTS, dtype=jnp.int32)
    lookup = lambda table, idx: jnp.sum(jnp.where(idx[..., None] == eids, table, 0), axis=-1)
    after = lookup(pad_end, block_e)
    next_e = jnp.where(after < pad_end[-1], expert_of(after), -1).astype(jnp.int32)
    dests = [(lookup(pad_start, top_e) + rank).reshape(-1).astype(jnp.int32)
             for top_e, rank in groups]
    zero_plan = (jnp.concatenate([pad_start + counts, pad_end[-1:]]).astype(jnp.int32),
                 jnp.concatenate([padded - counts, nblk * MOE_BLOCK - pad_end[-1:]]).astype(jnp.int32))
    return (nblk, block_e.astype(jnp.int32), next_e, nused.astype(jnp.int32).reshape(1), dests,
            pad_start, zero_plan)


def _temporal(x, p, attend, conv0, h0, nb, cnt0, alpha):
    q, kv, xr, yr, ga, gb = _project(x, p['w_in'], p['layer'])
    o_attn = attend(q, kv)
    y_rnn, conv_new, h_new = _rnn_branch(xr, yr, p, conv0, h0, nb)
    x1, top_e, gate, rank, cnt, runs = _mix(o_attn, y_rnn, ga, gb, x, p, cnt0, alpha)
    route = (top_e, gate, rank, runs.reshape(-1, N_EXPERTS).astype(jnp.int32))
    return x1, route, cnt, kv, conv_new, h_new


def kernel(x_prompt, x_sample, cache_k, cache_v, state_conv, state_h, w_in, attn_sinks, conv_w,
           conv_b, rg_wa, rg_ba, rg_wx, rg_bx, rg_lambda, w_attn_out, w_rnn_out, w_out, ln1_g,
           ln1_b, w_router, b_router, w_up, b_up, w_down, b_down, ln2_g, ln2_b):
    depth = w_in.shape[0]
    alpha = float((2 * depth) ** 0.25)
    bp, seq, _ = x_prompt.shape
    bs = x_sample.shape[0]
    cw = cache_k.shape[2]
    halo = CONV_W - 1

    xp = x_prompt.transpose(1, 0, 2).reshape(seq * bp, D_MODEL)
    xs = x_sample.reshape(bs, D_MODEL)
    outs = [[] for _ in range(8)]
    zeros_conv = jnp.zeros((halo * bp, D_RNN), F32)
    zeros_h = jnp.zeros((bp, D_RNN), F32)
    zeros_cnt = jnp.zeros((N_EXPERTS, 1), F32)
    w_in_bf = w_in.astype(BF16)

    for l in range(depth):
        p = {
            'w_in': w_in_bf, 'layer': l,
            'conv_w': conv_w[l], 'conv_b': conv_b[l].reshape(1, D_RNN),
            'rg_wa': rg_wa[l].astype(BF16), 'rg_ba': rg_ba[l].reshape(1, D_RNN),
            'rg_wx': rg_wx[l].astype(BF16), 'rg_bx': rg_bx[l].reshape(1, D_RNN),
            'rg_lambda': rg_lambda[l].reshape(1, D_RNN),
            'w_attn_out': w_attn_out[l].astype(BF16), 'w_rnn_out': w_rnn_out[l].astype(BF16),
            'w_out': w_out[l].astype(BF16),
            'ln1_g': ln1_g[l].reshape(1, D_MODEL), 'ln1_b': ln1_b[l].reshape(1, D_MODEL),
            'w_router_t': w_router[l].T.astype(BF16),
            'b_router': b_router[l].reshape(N_EXPERTS, 1),
        }
        sinks = attn_sinks[l]
        g2, b2 = ln2_g[l].reshape(1, D_MODEL), ln2_b[l].reshape(1, D_MODEL)

        x1p, route_p, cnt_p, kv_p, conv_p, h_p = _temporal(
            xp, p, lambda q, kv: _prompt_attention(q, kv, sinks, bp, seq),
            zeros_conv, zeros_h, bp, zeros_cnt, alpha)
        conv0_s = state_conv[l].transpose(1, 0, 2).reshape(halo * bs, D_RNN)
        x1s, route_s, cnt_s, kv_s, conv_s, h_s = _temporal(
            xs, p, lambda q, kv: _sample_attention(q, kv, sinks, cache_k[l], cache_v[l]),
            conv0_s, state_h[l], bs, cnt_p, alpha)

        (te_p, gt_p, rk_p, runs_p), (te_s, gt_s, rk_s, runs_s) = route_p, route_s
        counts_p = cnt_p.reshape(N_EXPERTS).astype(jnp.int32)
        counts = cnt_s.reshape(N_EXPERTS).astype(jnp.int32)
        nblk, block_e, next_e, nused, (dest_p, dest_s), pad_start, zero_plan = _moe_layout(
            counts, [(te_p, rk_p), (te_s, rk_s)])
        tm_p, tm_s = min(ROW_TILE, seq * bp), min(ROW_TILE, bs)
        plan_p, col_p, colrows_p = _combine_plan(te_p, rk_p, runs_p, counts_p, pad_start, tm_p)
        plan_s, _, colrows_s = _combine_plan(te_s, rk_s, runs_s, counts, pad_start, tm_s)
        sorted_x = _dispatch_runs(x1p, plan_p, col_p, tm_p, x1s, dest_s, nblk * MOE_BLOCK, zero_plan)
        yb = _experts(sorted_x, block_e, next_e, nused, l, w_up, b_up, w_down, b_down)
        xp = _combine(yb, x1p, plan_p, gt_p.T, colrows_p, tm_p, g2, b2, alpha)
        xs = _combine(yb, x1s, plan_s, gt_s.T, colrows_s, tm_s, g2, b2, alpha)

        kv_p4 = kv_p[:, (seq - cw) * bp:].reshape(2, cw, bp, N_KV_HEADS, HEAD_DIM)
        outs[0].append(kv_p4[0].transpose(1, 0, 2, 3))
        outs[1].append(kv_p4[1].transpose(1, 0, 2, 3))
        outs[2].append(conv_p.reshape(halo, bp, D_RNN).transpose(1, 0, 2))
        outs[3].append(h_p)
        kv_s4 = kv_s.reshape(2, bs, 1, N_KV_HEADS, HEAD_DIM)
        outs[4].append(jnp.concatenate([cache_k[l], kv_s4[0]], axis=1)[:, -cw:])
        outs[5].append(jnp.concatenate([cache_v[l], kv_s4[1]], axis=1)[:, -cw:])
        outs[6].append(conv_s.reshape(halo, bs, D_RNN).transpose(1, 0, 2))
        outs[7].append(h_s)

    y_prompt = xp.reshape(seq, bp, D_MODEL).transpose(1, 0, 2)
    y_sample = xs.reshape(bs, 1, D_MODEL)
    return (y_prompt, y_sample) + tuple(jnp.stack(o) for o in outs)
```

```python
import functools

import numpy as np
import jax
import jax.numpy as jnp
from jax import lax
from jax.experimental import pallas as pl
from jax.experimental.pallas import tpu as pltpu

F32 = jnp.float32
BF16 = jnp.bfloat16

D_MODEL = 1024
N_HEADS = 16
N_KV_HEADS = 2
HEAD_DIM = 64
GROUP = N_HEADS // N_KV_HEADS
WINDOW = 128
ATTN_W = N_HEADS * HEAD_DIM
KV_W = N_KV_HEADS * HEAD_DIM
D_RNN = 1280
RNN_BLOCK = 128
N_RNN_BLOCKS = D_RNN // RNN_BLOCK
CONV_W = 4
LRU_C = 8.0
N_EXPERTS = 32
TOP_K = 4
D_FF = 1024
SWIGLU_LIMIT = 7.0
SWIGLU_ALPHA = 1.702
LN_EPS = 1e-5
PAST_LEN = 8192
PROJ_OFFS = (0, ATTN_W, ATTN_W + 2 * KV_W, ATTN_W + 2 * KV_W + D_RNN,
             ATTN_W + 2 * KV_W + 2 * D_RNN, ATTN_W + 2 * KV_W + 2 * D_RNN + D_MODEL,
             ATTN_W + 2 * KV_W + 2 * D_RNN + 2 * D_MODEL)
PROJ_W = PROJ_OFFS[-1]
NEG_BIG = -1e30
LANES = 128
LANE_GROUPS = D_MODEL // LANES

ROW_TILE = 256
RNN_TIME_TILE = 64
MOE_BLOCK = 512
GATHER_TILE = 128
RUN_CHUNK = 16
QUERY_SPLIT = 128
SAMPLE_ATTN_TILE = 32
MIB = 1 << 20


def _alibi_slopes():
    h = np.arange(1, N_HEADS + 1, dtype=np.float32)
    return (np.float32(2.0) ** (np.float32(-8.0) * h / np.float32(N_HEADS))).astype(np.float32)


def _params(semantics, vmem_mib):
    return pltpu.CompilerParams(dimension_semantics=semantics, vmem_limit_bytes=vmem_mib * MIB)


def _const_spec(shape):
    nd = len(shape)
    return pl.BlockSpec(shape, lambda *_: (0,) * nd)


def _proj_kernel(x_ref, w_ref, q_ref, kv_ref, xr_ref, yr_ref, ga_ref, gb_ref):
    xb = x_ref[...].astype(BF16)
    outs = (q_ref, kv_ref, xr_ref, yr_ref, ga_ref, gb_ref)
    for n, o_ref in enumerate(outs):
        w = w_ref[0, :, PROJ_OFFS[n]:PROJ_OFFS[n + 1]]
        res = jnp.dot(xb, w, preferred_element_type=F32)
        if len(o_ref.shape) == 3:
            for c in range(o_ref.shape[0]):
                o_ref[c] = res[:, c * LANES:(c + 1) * LANES]
        else:
            o_ref[...] = res


def _project(x, w_in_bf, layer):
    n = x.shape[0]
    tm = min(ROW_TILE, n)
    widths = [PROJ_OFFS[i + 1] - PROJ_OFFS[i] for i in range(6)]
    row = lambda w: (pl.BlockSpec((tm, w), lambda i: (i, 0)), jax.ShapeDtypeStruct((n, w), F32))
    stack = lambda w: (pl.BlockSpec((w // LANES, tm, LANES), lambda i: (0, i, 0)),
                       jax.ShapeDtypeStruct((w // LANES, n, LANES), F32))
    outs = [stack(widths[0]), stack(widths[1])] + [row(w) for w in widths[2:]]
    return pl.pallas_call(
        _proj_kernel,
        grid=(n // tm,),
        in_specs=[pl.BlockSpec((tm, D_MODEL), lambda i: (i, 0)),
                  pl.BlockSpec((1, D_MODEL, PROJ_W), lambda i: (layer, 0, 0))],
        out_specs=[o[0] for o in outs],
        out_shape=[o[1] for o in outs],
        compiler_params=_params(("parallel",), 56),
        name="in_proj",
    )(x, w_in_bf)


def _prompt_attn_kernel(sink_ref, q_ref, kvp_ref, kvc_ref, o_ref, *, nb):
    tb = pl.program_id(0)
    seq_rows = pl.ds(pl.program_id(1), WINDOW, stride=nb)
    kcat = jnp.concatenate([kvp_ref[0, seq_rows, :], kvc_ref[0, seq_rows, :]], axis=0).astype(BF16)
    vcat = jnp.concatenate([kvp_ref[1, seq_rows, :], kvc_ref[1, seq_rows, :]], axis=0).astype(BF16)
    band = QUERY_SPLIT + WINDOW
    qi = lax.broadcasted_iota(jnp.int32, (QUERY_SPLIT, band), 0)
    kj = lax.broadcasted_iota(jnp.int32, (QUERY_SPLIT, band), 1)
    dist = qi - kj + WINDOW
    in_window = jnp.where(dist >= 0, jnp.where(dist < WINDOW, 1, 0), 0)
    neg_dists = []
    for r0 in range(0, WINDOW, QUERY_SPLIT):
        exists = jnp.where(kj + r0 >= WINDOW, 1, jnp.where(tb > 0, 1, 0))
        neg_dists.append(jnp.where(in_window * exists > 0, -dist.astype(F32), NEG_BIG))
    slopes = _alibi_slopes()
    scale = HEAD_DIM ** -0.5
    heads_per_chunk = LANES // HEAD_DIM
    for c in range(ATTN_W // LANES):
        qc = (q_ref[c, seq_rows, :] * scale).astype(BF16)
        outs = []
        for j in range(heads_per_chunk):
            h = c * heads_per_chunk + j
            g = h // GROUP
            sink = sink_ref[h]
            parts = []
            for s_idx, r0 in enumerate(range(0, WINDOW, QUERY_SPLIT)):
                qh = qc[r0:r0 + QUERY_SPLIT, j * HEAD_DIM:(j + 1) * HEAD_DIM]
                kg = kcat[r0:r0 + band, g * HEAD_DIM:(g + 1) * HEAD_DIM]
                vg = vcat[r0:r0 + band, g * HEAD_DIM:(g + 1) * HEAD_DIM]
                s = lax.dot_general(qh, kg, (((1,), (1,)), ((), ())), preferred_element_type=F32)
                logits = s + float(slopes[h]) * neg_dists[s_idx]
                m = jnp.maximum(jnp.max(logits, axis=1, keepdims=True), sink)
                p = jnp.exp(logits - m)
                denom = jnp.sum(p, axis=1, keepdims=True) + jnp.exp(sink - m)
                parts.append(jnp.dot(p.astype(BF16), vg, preferred_element_type=F32) / denom)
            outs.append(jnp.concatenate(parts, axis=0))
        o_ref[c, seq_rows, :] = jnp.concatenate(outs, axis=1)


def _prompt_attention(q, kv, sinks, batch, seq):
    nblk = seq // WINDOW
    rows = WINDOW * batch
    cur = lambda t, b: (0, t, 0)
    prev = lambda t, b: (0, jnp.maximum(t - 1, 0), 0)
    return pl.pallas_call(
        functools.partial(_prompt_attn_kernel, nb=batch),
        grid=(nblk, batch),
        in_specs=[pl.BlockSpec(memory_space=pltpu.SMEM),
                  pl.BlockSpec((q.shape[0], rows, LANES), cur),
                  pl.BlockSpec((kv.shape[0], rows, LANES), prev),
                  pl.BlockSpec((kv.shape[0], rows, LANES), cur)],
        out_specs=pl.BlockSpec((q.shape[0], rows, LANES), cur),
        out_shape=jax.ShapeDtypeStruct(q.shape, F32),
        compiler_params=_params(("parallel", "arbitrary"), 40),
        name="prompt_attn",
    )(sinks, q, kv, kv)


def _sample_attn_kernel(qz_ref, ck_ref, cv_ref, kn_ref, vn_ref, bias_ref, sink_ref, o_ref):
    qz = qz_ref[...]
    s_c = jnp.einsum('bhd,bjd->bhj', qz, ck_ref[...].astype(BF16),
                     preferred_element_type=F32) * (HEAD_DIM ** -0.5)
    s_n = jnp.sum(qz.astype(F32) * kn_ref[...], axis=-1, keepdims=True) * (HEAD_DIM ** -0.5)
    logits = s_c + bias_ref[...][None]
    sink = sink_ref[...][None]
    m = jnp.maximum(jnp.maximum(jnp.max(logits, axis=-1, keepdims=True), s_n), sink)
    p_c = jnp.exp(logits - m)
    p_n = jnp.exp(s_n - m)
    denom = jnp.sum(p_c, axis=-1, keepdims=True) + p_n + jnp.exp(sink - m)
    o = jnp.einsum('bhj,bjd->bhd', p_c.astype(BF16), cv_ref[...].astype(BF16),
                   preferred_element_type=F32)
    o_ref[...] = (o + p_n * vn_ref[...]) / denom


def _sample_attention(q, kv, sinks, ck, cv):
    b, cw = ck.shape[0], ck.shape[1]
    bt = min(SAMPLE_ATTN_TILE, b)
    q = q.transpose(1, 0, 2).reshape(b, ATTN_W).astype(BF16)
    kv = kv.transpose(1, 0, 2).reshape(b, 2 * KV_W)
    q4 = q.reshape(b, N_KV_HEADS, GROUP, HEAD_DIM)
    eye = jnp.eye(N_KV_HEADS, dtype=q.dtype)
    qz = (q4[:, :, :, None, :] * eye[None, :, None, :, None]).reshape(b, N_HEADS, KV_W)
    kn = kv[:, :KV_W].reshape(b, 1, KV_W)
    vn = kv[:, KV_W:].reshape(b, 1, KV_W)
    dist = (cw - np.arange(cw)).astype(np.float32)
    bias = np.where(dist[None, :] < WINDOW, -_alibi_slopes()[:, None] * dist[None, :], NEG_BIG)
    o = pl.pallas_call(
        _sample_attn_kernel,
        grid=(b // bt,),
        in_specs=[pl.BlockSpec((bt, N_HEADS, KV_W), lambda i: (i, 0, 0)),
                  pl.BlockSpec((bt, cw, KV_W), lambda i: (i, 0, 0)),
                  pl.BlockSpec((bt, cw, KV_W), lambda i: (i, 0, 0)),
                  pl.BlockSpec((bt, 1, KV_W), lambda i: (i, 0, 0)),
                  pl.BlockSpec((bt, 1, KV_W), lambda i: (i, 0, 0)),
                  _const_spec((N_HEADS, cw)),
                  _const_spec((N_HEADS, 1))],
        out_specs=pl.BlockSpec((bt, N_HEADS, KV_W), lambda i: (i, 0, 0)),
        out_shape=jax.ShapeDtypeStruct((b, N_HEADS, KV_W), F32),
        compiler_params=_params(("parallel",), 32),
        name="sample_attn",
    )(qz, ck.reshape(b, cw, KV_W), cv.reshape(b, cw, KV_W), kn, vn,
      jnp.asarray(bias, F32), sinks.reshape(N_HEADS, 1))
    o4 = o.reshape(b, N_KV_HEADS, GROUP, N_KV_HEADS, HEAD_DIM)
    o_sel = jnp.stack([o4[:, g, :, g, :] for g in range(N_KV_HEADS)], axis=1)
    return o_sel.reshape(b, ATTN_W // LANES, LANES).transpose(1, 0, 2)


def _rnn_kernel(xr_ref, yr_ref, cw_ref, cb_ref, wa_ref, ba_ref, wx_ref, bx_ref, lam_ref,
                conv0_ref, h0_ref, y_ref, convo_ref, ho_ref, xp_s, a_s, b_s, h_s, *, nb, tt):
    rows = tt * nb
    halo = (CONV_W - 1) * nb

    @pl.when(pl.program_id(0) == 0)
    def _():
        xp_s[0:halo] = conv0_ref[...]
        h_s[...] = h0_ref[...]

    xp_s[halo:halo + rows] = xr_ref[...]
    nl = -lam_ref[...]
    coef = -LRU_C * (jnp.maximum(nl, 0.0) + jnp.log1p(jnp.exp(-jnp.abs(nl))))
    for n in range(N_RNN_BLOCKS):
        sl = slice(n * RNN_BLOCK, (n + 1) * RNN_BLOCK)
        xc = cb_ref[:, sl] + xp_s[0:rows, sl] * cw_ref[0:1, sl]
        for j in range(1, CONV_W):
            xc = xc + xp_s[j * nb:j * nb + rows, sl] * cw_ref[j:j + 1, sl]
        xcb = xc.astype(BF16)
        r = jax.nn.sigmoid(jnp.dot(xcb, wa_ref[n], preferred_element_type=F32) + ba_ref[:, sl])
        ig = jax.nn.sigmoid(jnp.dot(xcb, wx_ref[n], preferred_element_type=F32) + bx_ref[:, sl])
        log_a = coef[:, sl] * r
        a = jnp.exp(log_a)
        a_s[:, sl] = a
        b_s[:, sl] = jnp.sqrt(1.0 - a * a) * (ig * xc)

    def step(t, h):
        off = pl.multiple_of(t * nb, nb)
        h = a_s[pl.ds(off, nb), :] * h + b_s[pl.ds(off, nb), :]
        a_s[pl.ds(off, nb), :] = h
        return h

    h = lax.fori_loop(0, tt, step, h_s[...])
    h_s[...] = h
    yr = yr_ref[...]
    gelu = 0.5 * yr * (1.0 + jnp.tanh(np.float32(np.sqrt(2.0 / np.pi))
                                      * (yr + np.float32(0.044715) * (yr * yr * yr))))
    y_ref[...] = (a_s[...] * gelu).astype(y_ref.dtype)
    tail = xp_s[rows:rows + halo]
    convo_ref[...] = tail
    ho_ref[...] = h
    xp_s[0:halo] = tail


def _rnn_branch(xr, yr, p, conv0, h0, nb):
    rows_total = xr.shape[0]
    t_total = rows_total // nb
    tt = min(RNN_TIME_TILE, t_total)
    rows = tt * nb
    halo = (CONV_W - 1) * nb
    row_spec = pl.BlockSpec((rows, D_RNN), lambda i: (i, 0))
    vec = _const_spec((1, D_RNN))
    return pl.pallas_call(
        functools.partial(_rnn_kernel, nb=nb, tt=tt),
        grid=(t_total // tt,),
        in_specs=[row_spec, row_spec, _const_spec((CONV_W, D_RNN)), vec,
                  _const_spec((N_RNN_BLOCKS, RNN_BLOCK, RNN_BLOCK)), vec,
                  _const_spec((N_RNN_BLOCKS, RNN_BLOCK, RNN_BLOCK)), vec, vec,
                  _const_spec((halo, D_RNN)), _const_spec((nb, D_RNN))],
        out_specs=[row_spec, _const_spec((halo, D_RNN)), _const_spec((nb, D_RNN))],
        out_shape=[jax.ShapeDtypeStruct((rows_total, D_RNN), BF16),
                   jax.ShapeDtypeStruct((halo, D_RNN), F32),
                   jax.ShapeDtypeStruct((nb, D_RNN), F32)],
        scratch_shapes=[pltpu.VMEM((rows + halo, D_RNN), F32),
                        pltpu.VMEM((rows, D_RNN), F32),
                        pltpu.VMEM((rows, D_RNN), F32),
                        pltpu.VMEM((nb, D_RNN), F32)],
        compiler_params=_params(("arbitrary",), 48),
        name="rnn_branch",
    )(xr, yr, p['conv_w'], p['conv_b'], p['rg_wa'], p['rg_ba'], p['rg_wx'], p['rg_bx'],
      p['rg_lambda'], conv0, h0)


def _load_token_tiles(ref, n, lead=()):
    return [ref[lead + (pl.ds(c, n, stride=LANE_GROUPS), slice(None))] for c in range(LANE_GROUPS)]


def _store_token_tiles(ref, x, n):
    for c in range(LANE_GROUPS):
        ref[pl.ds(c, n, stride=LANE_GROUPS), :] = x[:, c * LANES:(c + 1) * LANES]


def _layer_norm(z, g, b):
    mu = jnp.mean(z, axis=-1, keepdims=True)
    zc = z - mu
    var = jnp.mean(zc * zc, axis=-1, keepdims=True)
    return zc * lax.rsqrt(var + LN_EPS) * g + b


def _mix_kernel(o_ref, y_ref, ga_ref, gb_ref, x_ref, wao_ref, wro_ref, wo_ref, g1_ref, b1_ref,
                wrt_ref, br_ref, cnt0_ref, x1_ref, tope_ref, gate_ref, rank_ref, cnt_ref, runs_ref,
                run_s, *, alpha, tm):
    @pl.when(pl.program_id(0) == 0)
    def _():
        run_s[...] = cnt0_ref[...]

    o_attn = jnp.concatenate([o_ref[c].astype(BF16) for c in range(ATTN_W // LANES)], axis=1)
    att = jnp.dot(o_attn, wao_ref[...], preferred_element_type=F32)
    rn = jnp.dot(y_ref[...], wro_ref[...], preferred_element_type=F32)
    merged = jax.nn.sigmoid(ga_ref[...]) * att + jax.nn.sigmoid(gb_ref[...]) * rn
    mixed = jnp.dot(merged.astype(BF16), wo_ref[...], preferred_element_type=F32)
    x1 = _layer_norm(alpha * x_ref[...] + mixed, g1_ref[...], b1_ref[...])
    _store_token_tiles(x1_ref, x1, tm)

    logits = lax.dot_general(wrt_ref[...], x1.astype(BF16), (((1,), (1,)), ((), ())),
                             preferred_element_type=F32) + br_ref[...]
    eidx = lax.broadcasted_iota(jnp.int32, (N_EXPERTS, tm), 0)
    vals, idxs, hots = [], [], []
    for _ in range(TOP_K):
        m = jnp.max(logits, axis=0, keepdims=True)
        idx = jnp.min(jnp.where(logits == m, eidx, N_EXPERTS), axis=0, keepdims=True)
        hot = eidx == idx
        logits = jnp.where(hot, -jnp.inf, logits)
        vals.append(m)
        idxs.append(idx)
        hots.append(hot)
    exps = [jnp.exp(v - vals[0]) for v in vals]
    total = exps[0] + exps[1] + exps[2] + exps[3]
    gate_ref[...] = jnp.concatenate([e / total for e in exps], axis=0)
    tope_ref[...] = jnp.concatenate(idxs, axis=0)

    assigned = sum(h.astype(F32) for h in hots)
    earlier = (lax.broadcasted_iota(jnp.int32, (tm, tm), 0)
               < lax.broadcasted_iota(jnp.int32, (tm, tm), 1)).astype(BF16)
    before = jnp.dot(assigned.astype(BF16), earlier, preferred_element_type=F32) + run_s[...]
    ranks = [jnp.sum(jnp.where(h, before, 0.0), axis=0, keepdims=True) for h in hots]
    rank_ref[...] = jnp.concatenate(ranks, axis=0).astype(jnp.int32)
    runs_ref[0] = run_s[...]
    run_s[...] = run_s[...] + jnp.sum(assigned, axis=1, keepdims=True)
    cnt_ref[...] = run_s[...]


def _mix(o_attn, y_rnn, ga, gb, x, p, cnt0, alpha):
    n = x.shape[0]
    tm = min(ROW_TILE, n)
    row = lambda w: pl.BlockSpec((tm, w), lambda i: (i, 0))
    tok = pl.BlockSpec((TOP_K, tm), lambda i: (0, i))
    vec = _const_spec((1, D_MODEL))
    cnt = _const_spec((N_EXPERTS, 1))
    return pl.pallas_call(
        functools.partial(_mix_kernel, alpha=alpha, tm=tm),
        grid=(n // tm,),
        in_specs=[pl.BlockSpec((ATTN_W // LANES, tm, LANES), lambda i: (0, i, 0)),
                  row(D_RNN), row(D_MODEL), row(D_MODEL), row(D_MODEL),
                  _const_spec((ATTN_W, D_MODEL)), _const_spec((D_RNN, D_MODEL)),
                  _const_spec((D_MODEL, D_MODEL)), vec, vec,
                  _const_spec((N_EXPERTS, D_MODEL)), cnt, cnt],
        out_specs=[pl.BlockSpec((tm * LANE_GROUPS, LANES), lambda i: (i, 0)), tok, tok, tok, cnt,
                   pl.BlockSpec((1, N_EXPERTS, 1), lambda i: (i, 0, 0))],
        out_shape=[jax.ShapeDtypeStruct((n * LANE_GROUPS, LANES), F32),
                   jax.ShapeDtypeStruct((TOP_K, n), jnp.int32),
                   jax.ShapeDtypeStruct((TOP_K, n), F32),
                   jax.ShapeDtypeStruct((TOP_K, n), jnp.int32),
                   jax.ShapeDtypeStruct((N_EXPERTS, 1), F32),
                   jax.ShapeDtypeStruct((n // tm, N_EXPERTS, 1), F32)],
        scratch_shapes=[pltpu.VMEM((N_EXPERTS, 1), F32)],
        compiler_params=_params(("arbitrary",), 48),
        name="mix_norm_router",
    )(o_attn, y_rnn, ga, gb, x, p['w_attn_out'], p['w_rnn_out'], p['w_out'], p['ln1_g'],
      p['ln1_b'], p['w_router_t'], p['b_router'], cnt0)


def _tile_copy(src, src_tok, dst, dst_tok, sem):
    rows = lambda t: pl.ds(t * LANE_GROUPS if isinstance(t, int)
                           else pl.multiple_of(t * LANE_GROUPS, LANE_GROUPS), LANE_GROUPS)
    return pltpu.make_async_copy(src.at[rows(src_tok)], dst.at[rows(dst_tok)], sem)


_start_copy = lambda cp, k: cp.start(priority=k % 2)
_wait_copy = lambda cp, k: cp.wait()


def _dispatch_runs_kernel(src_ref, nch_ref, loc_ref, tot_ref, destb_ref, zstart_ref, zlen_ref,
                          xa_ref, cola_ref, xb_ref, xs_ref, stage_a, sems_a, stage_b, sem_b,
                          zeros, zsem, *, nb, tma, tmb, sa, steps_a, nseg):
    i = pl.program_id(0)
    slot = i % 2
    piece = RUN_CHUNK * LANE_GROUPS

    def runs(step, s, act):
        def one_expert(e, prio):
            idx = step * N_EXPERTS + e
            src0, loc0 = src_ref[idx], loc_ref[idx]

            def body(j, c):
                d = pl.multiple_of((src0 + j * RUN_CHUNK) * LANE_GROUPS, LANE_GROUPS)
                l = pl.multiple_of((loc0 + j * RUN_CHUNK) * LANE_GROUPS, piece)
                act(pltpu.make_async_copy(stage_a.at[s, pl.ds(l, piece)],
                                          xs_ref.at[pl.ds(d, piece)], sems_a.at[s]), prio)
                return c
            lax.fori_loop(0, nch_ref[idx], body, 0)

        def pair(m, c):
            one_expert(2 * m, 0)
            one_expert(2 * m + 1, 1)
            return c
        lax.fori_loop(0, N_EXPERTS // 2, pair, 0)

    def wait_runs(step, s):
        def body(j, c):
            pltpu.make_async_copy(stage_a.at[s, pl.ds(0, piece)], xs_ref.at[pl.ds(0, piece)],
                                  sems_a.at[s]).wait()
            return c
        lax.fori_loop(0, tot_ref[step], body, 0)

    def tokens_b(step, act):
        base = step * tmb

        def body(r, c):
            for k in range(TOP_K):
                act(_tile_copy(stage_b, r, xs_ref, destb_ref[k * nb + base + r], sem_b), k)
            return c
        lax.fori_loop(0, tmb, body, 0)

    def zero_fill(act):
        def segment(g, c):
            start, length = zstart_ref[g], zlen_ref[g]
            whole = length // RUN_CHUNK

            def pieces(j, c2):
                d = pl.multiple_of((start + j * RUN_CHUNK) * LANE_GROUPS, LANE_GROUPS)
                act(pltpu.make_async_copy(zeros, xs_ref.at[pl.ds(d, piece)], zsem), 0)
                return c2

            def singles(j, c2):
                d = pl.multiple_of((start + whole * RUN_CHUNK + j) * LANE_GROUPS, LANE_GROUPS)
                act(pltpu.make_async_copy(zeros.at[pl.ds(0, LANE_GROUPS)],
                                          xs_ref.at[pl.ds(d, LANE_GROUPS)], zsem), 1)
                return c2
            lax.fori_loop(0, whole, pieces, 0)
            lax.fori_loop(0, length - whole * RUN_CHUNK, singles, 0)
            return c
        lax.fori_loop(0, nseg, segment, 0)

    @pl.when(i == 0)
    def _():
        zeros[...] = jnp.zeros_like(zeros)
        zero_fill(_start_copy)

    @pl.when(i < steps_a)
    def _():
        xa = jnp.concatenate([c.astype(BF16) for c in _load_token_tiles(xa_ref, tma)], axis=1)
        col = cola_ref[...]
        pos = lax.broadcasted_iota(jnp.int32, (sa, tma), 0)
        onehot = jnp.where(col[0:1, :] == pos, 1.0, 0.0)
        for k in range(1, TOP_K):
            onehot = onehot + jnp.where(col[k:k + 1, :] == pos, 1.0, 0.0)
        staged = jnp.dot(onehot.astype(BF16), xa, preferred_element_type=F32)
        for c in range(LANE_GROUPS):
            stage_a[slot, pl.ds(c, sa, stride=LANE_GROUPS), :] = staged[:, c * LANES:(c + 1) * LANES]

        @pl.when(i == 0)
        def _():
            zero_fill(_wait_copy)

        @pl.when(i > 0)
        def _():
            wait_runs(i - 1, 1 - slot)
        runs(i, slot, _start_copy)

    @pl.when(i >= steps_a)
    def _():
        @pl.when(i == steps_a)
        def _():
            wait_runs(steps_a - 1, (steps_a - 1) % 2)
        stage_b[...] = xb_ref[...]
        tokens_b(i - steps_a, _start_copy)
        tokens_b(i - steps_a, _wait_copy)


def _dispatch_runs(xa_tiles, plan_a, col_a, tma, xb_tiles, dest_b, rows, zero_plan):
    na, nb = xa_tiles.shape[0] // LANE_GROUPS, xb_tiles.shape[0] // LANE_GROUPS
    tmb = min(GATHER_TILE, nb)
    steps_a, steps_b = na // tma, nb // tmb
    sa = _staging_rows(tma)
    zstart, zlen = zero_plan
    return pl.pallas_call(
        functools.partial(_dispatch_runs_kernel, nb=nb, tma=tma, tmb=tmb, sa=sa, steps_a=steps_a,
                          nseg=zstart.shape[0]),
        grid_spec=pltpu.PrefetchScalarGridSpec(
            num_scalar_prefetch=7,
            grid=(steps_a + steps_b,),
            in_specs=[pl.BlockSpec((tma * LANE_GROUPS, LANES),
                                   lambda i, *_: (jnp.minimum(i, steps_a - 1), 0)),
                      pl.BlockSpec((TOP_K, tma), lambda i, *_: (0, jnp.minimum(i, steps_a - 1))),
                      pl.BlockSpec((tmb * LANE_GROUPS, LANES),
                                   lambda i, *_: (jnp.maximum(i - steps_a, 0), 0))],
            out_specs=pl.BlockSpec(memory_space=pl.ANY),
            scratch_shapes=[pltpu.VMEM((2, sa * LANE_GROUPS, LANES), F32),
                            pltpu.SemaphoreType.DMA((2,)),
                            pltpu.VMEM((tmb * LANE_GROUPS, LANES), F32),
                            pltpu.SemaphoreType.DMA(()),
                            pltpu.VMEM((RUN_CHUNK * LANE_GROUPS, LANES), F32),
                            pltpu.SemaphoreType.DMA(())]),
        out_shape=jax.ShapeDtypeStruct((rows * LANE_GROUPS, LANES), F32),
        compiler_params=_params(("arbitrary",), 48),
        name="moe_dispatch",
    )(*plan_a, dest_b, zstart, zlen, xa_tiles, col_a, xb_tiles)


def _staging_rows(tm):
    return -(-(TOP_K * tm + N_EXPERTS * (RUN_CHUNK - 1)) // LANES) * LANES


def _expert_kernel(be_ref, next_ref, nused_ref, x_ref, wu_hbm, bu_ref, wd_hbm, bd_ref, y_ref,
                   wu_f, wd_f, wu_s, wd_s, slot_s, sems, *, layer):
    i = pl.program_id(0)

    def weight_copies(e, slot):
        return (pltpu.make_async_copy(wu_hbm.at[layer, e], wu_f.at[slot], sems.at[slot, 0]),
                pltpu.make_async_copy(wd_hbm.at[layer, e], wd_f.at[slot], sems.at[slot, 1]))

    @pl.when(i < nused_ref[0])
    def _():
        e = be_ref[i]

        @pl.when(i == 0)
        def _():
            slot_s[0] = 0
            for cp in weight_copies(e, 0):
                cp.start()

        @pl.when(jnp.logical_or(i == 0, e != be_ref[jnp.maximum(i - 1, 0)]))
        def _():
            @pl.when(i > 0)
            def _():
                slot_s[0] = 1 - slot_s[0]
            slot = slot_s[0]
            for cp in weight_copies(e, slot):
                cp.wait()
            nxt = next_ref[i]

            @pl.when(nxt >= 0)
            def _():
                for cp in weight_copies(nxt, 1 - slot):
                    cp.start()
            wu_s[...] = wu_f[slot].astype(BF16)
            wd_s[...] = wd_f[slot].astype(BF16)

        xb = jnp.concatenate([c.astype(BF16) for c in _load_token_tiles(x_ref, MOE_BLOCK)], axis=1)
        h = jnp.dot(xb, wu_s[...], preferred_element_type=F32) + bu_ref[0, 0]
        glu = jnp.minimum(h[:, :D_FF], SWIGLU_LIMIT)
        lin = jnp.clip(h[:, D_FF:], -SWIGLU_LIMIT, SWIGLU_LIMIT)
        act = glu * jax.nn.sigmoid(SWIGLU_ALPHA * glu) * (lin + 1.0)
        y = jnp.dot(act.astype(BF16), wd_s[...], preferred_element_type=F32) + bd_ref[0, 0]
        _store_token_tiles(y_ref, y, MOE_BLOCK)

    @pl.when(i >= nused_ref[0])
    def _():
        y_ref[...] = jnp.zeros_like(y_ref)


def _experts(xs, block_e, next_e, nused, layer, w_up, b_up, w_down, b_down):
    rows = xs.shape[0] // LANE_GROUPS
    nblk = rows // MOE_BLOCK
    depth = w_up.shape[0]
    tile_blk = (MOE_BLOCK * LANE_GROUPS, LANES)
    blk = lambda i, be, nx, nu: (jnp.minimum(i, nu[0] - 1), 0)
    out_blk = lambda i, be, nx, nu: (i, 0)
    exp3 = lambda i, be, nx, nu: (layer, be[i], 0, 0)
    return pl.pallas_call(
        functools.partial(_expert_kernel, layer=layer),
        grid_spec=pltpu.PrefetchScalarGridSpec(
            num_scalar_prefetch=3,
            grid=(nblk,),
            in_specs=[pl.BlockSpec(tile_blk, blk),
                      pl.BlockSpec(memory_space=pl.ANY),
                      pl.BlockSpec((1, 1, 1, 2 * D_FF), exp3),
                      pl.BlockSpec(memory_space=pl.ANY),
                      pl.BlockSpec((1, 1, 1, D_MODEL), exp3)],
            out_specs=pl.BlockSpec(tile_blk, out_blk),
            scratch_shapes=[pltpu.VMEM((2, D_MODEL, 2 * D_FF), F32),
                            pltpu.VMEM((2, D_FF, D_MODEL), F32),
                            pltpu.VMEM((D_MODEL, 2 * D_FF), BF16),
                            pltpu.VMEM((D_FF, D_MODEL), BF16),
                            pltpu.SMEM((1,), jnp.int32),
                            pltpu.SemaphoreType.DMA((2, 2))]),
        out_shape=jax.ShapeDtypeStruct(xs.shape, F32),
        compiler_params=_params(("arbitrary",), 56),
        name="moe_experts",
    )(block_e, next_e, nused, xs, w_up, b_up.reshape(depth, N_EXPERTS, 1, 2 * D_FF), w_down,
      b_down.reshape(depth, N_EXPERTS, 1, D_MODEL))


def _combine_kernel(src_ref, nch_ref, loc_ref, tot_ref, yb_ref, x1_ref, gate_ref, col_ref, g2_ref,
                    b2_ref, x2_ref, buf, sems, *, tm, stage, alpha):
    i = pl.program_id(0)
    slot = i % 2
    piece = RUN_CHUNK * LANE_GROUPS

    def runs(step, buf_slot, act):
        def one_expert(e, prio):
            idx = step * N_EXPERTS + e
            src0, loc0 = src_ref[idx], loc_ref[idx]

            def body(j, c):
                s = pl.multiple_of((src0 + j * RUN_CHUNK) * LANE_GROUPS, LANE_GROUPS)
                d = pl.multiple_of((loc0 + j * RUN_CHUNK) * LANE_GROUPS, piece)
                act(pltpu.make_async_copy(yb_ref.at[pl.ds(s, piece)],
                                          buf.at[buf_slot, pl.ds(d, piece)], sems.at[buf_slot]), prio)
                return c
            lax.fori_loop(0, nch_ref[idx], body, 0)

        def pair(m, c):
            one_expert(2 * m, 0)
            one_expert(2 * m + 1, 1)
            return c
        lax.fori_loop(0, N_EXPERTS // 2, pair, 0)

    @pl.when(i == 0)
    def _():
        buf[...] = jnp.zeros_like(buf)
        runs(0, 0, _start_copy)

    @pl.when(i + 1 < pl.num_programs(0))
    def _():
        runs(i + 1, 1 - slot, _start_copy)

    def wait_piece(j, c):
        pltpu.make_async_copy(yb_ref.at[pl.ds(0, piece)], buf.at[slot, pl.ds(0, piece)],
                              sems.at[slot]).wait()
        return c
    lax.fori_loop(0, tot_ref[i], wait_piece, 0)
    staged = jnp.concatenate([c.astype(BF16) for c in _load_token_tiles(buf, stage, lead=(slot,))],
                             axis=1)
    gate, col = gate_ref[...], col_ref[...]
    pos = lax.broadcasted_iota(jnp.int32, (tm, stage), 1)
    weights = jnp.where(col[:, 0:1] == pos, gate[:, 0:1], 0.0)
    for k in range(1, TOP_K):
        weights = weights + jnp.where(col[:, k:k + 1] == pos, gate[:, k:k + 1], 0.0)
    y = jnp.dot(weights.astype(BF16), staged, preferred_element_type=F32)
    x1 = jnp.concatenate(_load_token_tiles(x1_ref, tm), axis=1)
    x2_ref[...] = _layer_norm(alpha * x1 + y, g2_ref[...], b2_ref[...])


def _combine(yb, x1_tiles, runs_plan, gate_rows, col_rows, tm, ln_g, ln_b, alpha):
    n = x1_tiles.shape[0] // LANE_GROUPS
    stage = _staging_rows(tm)
    vec = pl.BlockSpec((1, D_MODEL), lambda i, *_: (0, 0))
    tok = pl.BlockSpec((tm, TOP_K), lambda i, *_: (i, 0))
    return pl.pallas_call(
        functools.partial(_combine_kernel, tm=tm, stage=stage, alpha=alpha),
        grid_spec=pltpu.PrefetchScalarGridSpec(
            num_scalar_prefetch=4,
            grid=(n // tm,),
            in_specs=[pl.BlockSpec(memory_space=pl.ANY),
                      pl.BlockSpec((tm * LANE_GROUPS, LANES), lambda i, *_: (i, 0)),
                      tok, tok, vec, vec],
            out_specs=pl.BlockSpec((tm, D_MODEL), lambda i, *_: (i, 0)),
            scratch_shapes=[pltpu.VMEM((2, stage * LANE_GROUPS, LANES), F32),
                            pltpu.SemaphoreType.DMA((2,))]),
        out_shape=jax.ShapeDtypeStruct((n, D_MODEL), F32),
        compiler_params=_params(("arbitrary",), 48),
        name="moe_combine",
    )(*runs_plan, yb, x1_tiles, gate_rows, col_rows, ln_g, ln_b)


def _combine_plan(top_e, rank, runs, counts_after, pad_start, tm):
    nt = runs.shape[0]
    tile_cnt = jnp.concatenate([runs[1:], counts_after[None]], axis=0) - runs
    nch = (tile_cnt + RUN_CHUNK - 1) // RUN_CHUNK
    room = nch * RUN_CHUNK
    loc = jnp.cumsum(room, axis=1) - room
    src = pad_start[None, :] + runs
    eids = jnp.arange(N_EXPERTS, dtype=jnp.int32)
    base = (loc - runs)[None, :, None, :]
    hit = top_e.reshape(TOP_K, nt, tm)[..., None] == eids
    col = jnp.sum(jnp.where(hit, base, 0), axis=-1).reshape(TOP_K, nt * tm) + rank
    flat = lambda a: a.reshape(-1).astype(jnp.int32)
    col = col.astype(jnp.int32)
    return (flat(src), flat(nch), flat(loc), flat(jnp.sum(nch, axis=1))), col, col.T


def _moe_layout(counts, groups):
    n_assign = sum(g[0].shape[1] for g in groups) * TOP_K
    spare = RUN_CHUNK - 1
    nblk = -(-(n_assign + N_EXPERTS * (MOE_BLOCK - 1 + spare)) // MOE_BLOCK) + 1
    padded = jnp.where(counts > 0, (counts + spare + MOE_BLOCK - 1) // MOE_BLOCK * MOE_BLOCK, 0)
    pad_end = jnp.cumsum(padded)
    pad_start = pad_end - padded
    nused = pad_end[-1] // MOE_BLOCK

    def expert_of(row):
        ended = (pad_end[None, :] <= row[:, None]).astype(jnp.int32)
        return jnp.minimum(jnp.sum(ended, axis=1), N_EXPERTS - 1)

    blk = jnp.arange(nblk, dtype=jnp.int32)
    block_e = expert_of(jnp.minimum(blk, nused - 1) * MOE_BLOCK)
    eids = jnp.arange(N_EXPERTS, dtype=jnp.int32)
    lookup = lambda table, idx: jnp.sum(jnp.where(idx[..., None] == eids, table, 0), axis=-1)
    after = lookup(pad_end, block_e)
    next_e = jnp.where(after < pad_end[-1], expert_of(after), -1).astype(jnp.int32)
    dests = [(lookup(pad_start, top_e) + rank).reshape(-1).astype(jnp.int32)
             for top_e, rank in groups]
    zero_plan = (jnp.concatenate([pad_start + counts, pad_end[-1:]]).astype(jnp.int32),
                 jnp.concatenate([padded - counts, nblk * MOE_BLOCK - pad_end[-1:]]).astype(jnp.int32))
    return (nblk, block_e.astype(jnp.int32), next_e, nused.astype(jnp.int32).reshape(1), dests,
            pad_start, zero_plan)


def _temporal(x, p, attend, conv0, h0, nb, cnt0, alpha):
    q, kv, xr, yr, ga, gb = _project(x, p['w_in'], p['layer'])
    o_attn = attend(q, kv)
    y_rnn, conv_new, h_new = _rnn_branch(xr, yr, p, conv0, h0, nb)
    x1, top_e, gate, rank, cnt, runs = _mix(o_attn, y_rnn, ga, gb, x, p, cnt0, alpha)
    route = (top_e, gate, rank, runs.reshape(-1, N_EXPERTS).astype(jnp.int32))
    return x1, route, cnt, kv, conv_new, h_new


def kernel(x_prompt, x_sample, cache_k, cache_v, state_conv, state_h, w_in, attn_sinks, conv_w,
           conv_b, rg_wa, rg_ba, rg_wx, rg_bx, rg_lambda, w_attn_out, w_rnn_out, w_out, ln1_g,
           ln1_b, w_router, b_router, w_up, b_up, w_down, b_down, ln2_g, ln2_b):
    depth = w_in.shape[0]
    alpha = float((2 * depth) ** 0.25)
    bp, seq, _ = x_prompt.shape
    bs = x_sample.shape[0]
    cw = cache_k.shape[2]
    halo = CONV_W - 1

    xp = x_prompt.transpose(1, 0, 2).reshape(seq * bp, D_MODEL)
    xs = x_sample.reshape(bs, D_MODEL)
    outs = [[] for _ in range(8)]
    zeros_conv = jnp.zeros((halo * bp, D_RNN), F32)
    zeros_h = jnp.zeros((bp, D_RNN), F32)
    zeros_cnt = jnp.zeros((N_EXPERTS, 1), F32)
    w_in_bf = w_in.astype(BF16)

    for l in range(depth):
        p = {
            'w_in': w_in_bf, 'layer': l,
            'conv_w': conv_w[l], 'conv_b': conv_b[l].reshape(1, D_RNN),
            'rg_wa': rg_wa[l].astype(BF16), 'rg_ba': rg_ba[l].reshape(1, D_RNN),
            'rg_wx': rg_wx[l].astype(BF16), 'rg_bx': rg_bx[l].reshape(1, D_RNN),
            'rg_lambda': rg_lambda[l].reshape(1, D_RNN),
            'w_attn_out': w_attn_out[l].astype(BF16), 'w_rnn_out': w_rnn_out[l].astype(BF16),
            'w_out': w_out[l].astype(BF16),
            'ln1_g': ln1_g[l].reshape(1, D_MODEL), 'ln1_b': ln1_b[l].reshape(1, D_MODEL),
            'w_router_t': w_router[l].T.astype(BF16),
            'b_router': b_router[l].reshape(N_EXPERTS, 1),
        }
        sinks = attn_sinks[l]
        g2, b2 = ln2_g[l].reshape(1, D_MODEL), ln2_b[l].reshape(1, D_MODEL)

        x1p, route_p, cnt_p, kv_p, conv_p, h_p = _temporal(
            xp, p, lambda q, kv: _prompt_attention(q, kv, sinks, bp, seq),
            zeros_conv, zeros_h, bp, zeros_cnt, alpha)
        conv0_s = state_conv[l].transpose(1, 0, 2).reshape(halo * bs, D_RNN)
        x1s, route_s, cnt_s, kv_s, conv_s, h_s = _temporal(
            xs, p, lambda q, kv: _sample_attention(q, kv, sinks, cache_k[l], cache_v[l]),
            conv0_s, state_h[l], bs, cnt_p, alpha)

        (te_p, gt_p, rk_p, runs_p), (te_s, gt_s, rk_s, runs_s) = route_p, route_s
        counts_p = cnt_p.reshape(N_EXPERTS).astype(jnp.int32)
        counts = cnt_s.reshape(N_EXPERTS).astype(jnp.int32)
        nblk, block_e, next_e, nused, (dest_p, dest_s), pad_start, zero_plan = _moe_layout(
            counts, [(te_p, rk_p), (te_s, rk_s)])
        tm_p, tm_s = min(ROW_TILE, seq * bp), min(ROW_TILE, bs)
        plan_p, col_p, colrows_p = _combine_plan(te_p, rk_p, runs_p, counts_p, pad_start, tm_p)
        plan_s, _, colrows_s = _combine_plan(te_s, rk_s, runs_s, counts, pad_start, tm_s)
        sorted_x = _dispatch_runs(x1p, plan_p, col_p, tm_p, x1s, dest_s, nblk * MOE_BLOCK, zero_plan)
        yb = _experts(sorted_x, block_e, next_e, nused, l, w_up, b_up, w_down, b_down)
        xp = _combine(yb, x1p, plan_p, gt_p.T, colrows_p, tm_p, g2, b2, alpha)
        xs = _combine(yb, x1s, plan_s, gt_s.T, colrows_s, tm_s, g2, b2, alpha)

        kv_p4 = kv_p[:, (seq - cw) * bp:].reshape(2, cw, bp, N_KV_HEADS, HEAD_DIM)
        outs[0].append(kv_p4[0].transpose(1, 0, 2, 3))
        outs[1].append(kv_p4[1].transpose(1, 0, 2, 3))
        outs[2].append(conv_p.reshape(halo, bp, D_RNN).transpose(1, 0, 2))
        outs[3].append(h_p)
        kv_s4 = kv_s.reshape(2, bs, 1, N_KV_HEADS, HEAD_DIM)
        outs[4].append(jnp.concatenate([cache_k[l], kv_s4[0]], axis=1)[:, -cw:])
        outs[5].append(jnp.concatenate([cache_v[l], kv_s4[1]], axis=1)[:, -cw:])
        outs[6].append(conv_s.reshape(halo, bs, D_RNN).transpose(1, 0, 2))
        outs[7].append(h_s)

    y_prompt = xp.reshape(seq, bp, D_MODEL).transpose(1, 0, 2)
    y_sample = xs.reshape(bs, 1, D_MODEL)
    return (y_prompt, y_sample) + tuple(jnp.stack(o) for o in outs)
```

```python
import functools

import numpy as np
import jax
import jax.numpy as jnp
from jax import lax
from jax.experimental import pallas as pl
from jax.experimental.pallas import tpu as pltpu

F32 = jnp.float32
BF16 = jnp.bfloat16

D_MODEL = 1024
N_HEADS = 16
N_KV_HEADS = 2
HEAD_DIM = 64
GROUP = N_HEADS // N_KV_HEADS
WINDOW = 128
ATTN_W = N_HEADS * HEAD_DIM
KV_W = N_KV_HEADS * HEAD_DIM
D_RNN = 1280
RNN_BLOCK = 128
N_RNN_BLOCKS = D_RNN // RNN_BLOCK
CONV_W = 4
LRU_C = 8.0
N_EXPERTS = 32
TOP_K = 4
D_FF = 1024
SWIGLU_LIMIT = 7.0
SWIGLU_ALPHA = 1.702
LN_EPS = 1e-5
PAST_LEN = 8192
PROJ_OFFS = (0, ATTN_W, ATTN_W + 2 * KV_W, ATTN_W + 2 * KV_W + D_RNN,
             ATTN_W + 2 * KV_W + 2 * D_RNN, ATTN_W + 2 * KV_W + 2 * D_RNN + D_MODEL,
             ATTN_W + 2 * KV_W + 2 * D_RNN + 2 * D_MODEL)
PROJ_W = PROJ_OFFS[-1]
NEG_BIG = -1e30
LANES = 128
LANE_GROUPS = D_MODEL // LANES

ROW_TILE = 256
RNN_TIME_TILE = 64
MOE_BLOCK = 512
FF_CHUNK = 512
GATHER_TILE = 128
RUN_CHUNK = 16
QUERY_SPLIT = 128
SAMPLE_ATTN_TILE = 32
MIB = 1 << 20


def _alibi_slopes():
    h = np.arange(1, N_HEADS + 1, dtype=np.float32)
    return (np.float32(2.0) ** (np.float32(-8.0) * h / np.float32(N_HEADS))).astype(np.float32)


def _params(semantics, vmem_mib):
    return pltpu.CompilerParams(dimension_semantics=semantics, vmem_limit_bytes=vmem_mib * MIB)


def _const_spec(shape):
    nd = len(shape)
    return pl.BlockSpec(shape, lambda *_: (0,) * nd)


def _proj_kernel(x_ref, w_ref, q_ref, kv_ref, xr_ref, yr_ref, ga_ref, gb_ref):
    xb = x_ref[...].astype(BF16)
    outs = (q_ref, kv_ref, xr_ref, yr_ref, ga_ref, gb_ref)
    for n, o_ref in enumerate(outs):
        w = w_ref[0, :, PROJ_OFFS[n]:PROJ_OFFS[n + 1]]
        res = jnp.dot(xb, w, preferred_element_type=F32)
        if len(o_ref.shape) == 3:
            for c in range(o_ref.shape[0]):
                o_ref[c] = res[:, c * LANES:(c + 1) * LANES]
        else:
            o_ref[...] = res


def _project(x, w_in_bf, layer):
    n = x.shape[0]
    tm = min(ROW_TILE, n)
    widths = [PROJ_OFFS[i + 1] - PROJ_OFFS[i] for i in range(6)]
    row = lambda w: (pl.BlockSpec((tm, w), lambda i: (i, 0)), jax.ShapeDtypeStruct((n, w), F32))
    stack = lambda w: (pl.BlockSpec((w // LANES, tm, LANES), lambda i: (0, i, 0)),
                       jax.ShapeDtypeStruct((w // LANES, n, LANES), F32))
    outs = [stack(widths[0]), stack(widths[1])] + [row(w) for w in widths[2:]]
    return pl.pallas_call(
        _proj_kernel,
        grid=(n // tm,),
        in_specs=[pl.BlockSpec((tm, D_MODEL), lambda i: (i, 0)),
                  pl.BlockSpec((1, D_MODEL, PROJ_W), lambda i: (layer, 0, 0))],
        out_specs=[o[0] for o in outs],
        out_shape=[o[1] for o in outs],
        compiler_params=_params(("parallel",), 56),
        name="in_proj",
    )(x, w_in_bf)


def _prompt_attn_kernel(sink_ref, q_ref, kvp_ref, kvc_ref, o_ref, *, nb):
    tb = pl.program_id(0)
    seq_rows = pl.ds(pl.program_id(1), WINDOW, stride=nb)
    kcat = jnp.concatenate([kvp_ref[0, seq_rows, :], kvc_ref[0, seq_rows, :]], axis=0).astype(BF16)
    vcat = jnp.concatenate([kvp_ref[1, seq_rows, :], kvc_ref[1, seq_rows, :]], axis=0).astype(BF16)
    band = QUERY_SPLIT + WINDOW
    qi = lax.broadcasted_iota(jnp.int32, (QUERY_SPLIT, band), 0)
    kj = lax.broadcasted_iota(jnp.int32, (QUERY_SPLIT, band), 1)
    dist = qi - kj + WINDOW
    in_window = jnp.where(dist >= 0, jnp.where(dist < WINDOW, 1, 0), 0)
    neg_dists = []
    for r0 in range(0, WINDOW, QUERY_SPLIT):
        exists = jnp.where(kj + r0 >= WINDOW, 1, jnp.where(tb > 0, 1, 0))
        neg_dists.append(jnp.where(in_window * exists > 0, -dist.astype(F32), NEG_BIG))
    slopes = _alibi_slopes()
    scale = HEAD_DIM ** -0.5
    heads_per_chunk = LANES // HEAD_DIM
    for c in range(ATTN_W // LANES):
        qc = (q_ref[c, seq_rows, :] * scale).astype(BF16)
        outs = []
        for j in range(heads_per_chunk):
            h = c * heads_per_chunk + j
            g = h // GROUP
            sink = sink_ref[h]
            parts = []
            for s_idx, r0 in enumerate(range(0, WINDOW, QUERY_SPLIT)):
                qh = qc[r0:r0 + QUERY_SPLIT, j * HEAD_DIM:(j + 1) * HEAD_DIM]
                kg = kcat[r0:r0 + band, g * HEAD_DIM:(g + 1) * HEAD_DIM]
                vg = vcat[r0:r0 + band, g * HEAD_DIM:(g + 1) * HEAD_DIM]
                s = lax.dot_general(qh, kg, (((1,), (1,)), ((), ())), preferred_element_type=F32)
                logits = s + float(slopes[h]) * neg_dists[s_idx]
                m = jnp.maximum(jnp.max(logits, axis=1, keepdims=True), sink)
                p = jnp.exp(logits - m)
                denom = jnp.sum(p, axis=1, keepdims=True) + jnp.exp(sink - m)
                parts.append(jnp.dot(p.astype(BF16), vg, preferred_element_type=F32) / denom)
            outs.append(jnp.concatenate(parts, axis=0))
        o_ref[c, seq_rows, :] = jnp.concatenate(outs, axis=1)


def _prompt_attention(q, kv, sinks, batch, seq):
    nblk = seq // WINDOW
    rows = WINDOW * batch
    cur = lambda t, b: (0, t, 0)
    prev = lambda t, b: (0, jnp.maximum(t - 1, 0), 0)
    return pl.pallas_call(
        functools.partial(_prompt_attn_kernel, nb=batch),
        grid=(nblk, batch),
        in_specs=[pl.BlockSpec(memory_space=pltpu.SMEM),
                  pl.BlockSpec((q.shape[0], rows, LANES), cur),
                  pl.BlockSpec((kv.shape[0], rows, LANES), prev),
                  pl.BlockSpec((kv.shape[0], rows, LANES), cur)],
        out_specs=pl.BlockSpec((q.shape[0], rows, LANES), cur),
        out_shape=jax.ShapeDtypeStruct(q.shape, F32),
        compiler_params=_params(("parallel", "arbitrary"), 40),
        name="prompt_attn",
    )(sinks, q, kv, kv)


def _sample_attn_kernel(qz_ref, ck_ref, cv_ref, kn_ref, vn_ref, bias_ref, sink_ref, o_ref):
    qz = qz_ref[...]
    s_c = jnp.einsum('bhd,bjd->bhj', qz, ck_ref[...].astype(BF16),
                     preferred_element_type=F32) * (HEAD_DIM ** -0.5)
    s_n = jnp.sum(qz.astype(F32) * kn_ref[...], axis=-1, keepdims=True) * (HEAD_DIM ** -0.5)
    logits = s_c + bias_ref[...][None]
    sink = sink_ref[...][None]
    m = jnp.maximum(jnp.maximum(jnp.max(logits, axis=-1, keepdims=True), s_n), sink)
    p_c = jnp.exp(logits - m)
    p_n = jnp.exp(s_n - m)
    denom = jnp.sum(p_c, axis=-1, keepdims=True) + p_n + jnp.exp(sink - m)
    o = jnp.einsum('bhj,bjd->bhd', p_c.astype(BF16), cv_ref[...].astype(BF16),
                   preferred_element_type=F32)
    o_ref[...] = (o + p_n * vn_ref[...]) / denom


def _sample_attention(q, kv, sinks, ck, cv):
    b, cw = ck.shape[0], ck.shape[1]
    bt = min(SAMPLE_ATTN_TILE, b)
    q = q.transpose(1, 0, 2).reshape(b, ATTN_W).astype(BF16)
    kv = kv.transpose(1, 0, 2).reshape(b, 2 * KV_W)
    q4 = q.reshape(b, N_KV_HEADS, GROUP, HEAD_DIM)
    eye = jnp.eye(N_KV_HEADS, dtype=q.dtype)
    qz = (q4[:, :, :, None, :] * eye[None, :, None, :, None]).reshape(b, N_HEADS, KV_W)
    kn = kv[:, :KV_W].reshape(b, 1, KV_W)
    vn = kv[:, KV_W:].reshape(b, 1, KV_W)
    dist = (cw - np.arange(cw)).astype(np.float32)
    bias = np.where(dist[None, :] < WINDOW, -_alibi_slopes()[:, None] * dist[None, :], NEG_BIG)
    o = pl.pallas_call(
        _sample_attn_kernel,
        grid=(b // bt,),
        in_specs=[pl.BlockSpec((bt, N_HEADS, KV_W), lambda i: (i, 0, 0)),
                  pl.BlockSpec((bt, cw, KV_W), lambda i: (i, 0, 0)),
                  pl.BlockSpec((bt, cw, KV_W), lambda i: (i, 0, 0)),
                  pl.BlockSpec((bt, 1, KV_W), lambda i: (i, 0, 0)),
                  pl.BlockSpec((bt, 1, KV_W), lambda i: (i, 0, 0)),
                  _const_spec((N_HEADS, cw)),
                  _const_spec((N_HEADS, 1))],
        out_specs=pl.BlockSpec((bt, N_HEADS, KV_W), lambda i: (i, 0, 0)),
        out_shape=jax.ShapeDtypeStruct((b, N_HEADS, KV_W), F32),
        compiler_params=_params(("parallel",), 32),
        name="sample_attn",
    )(qz, ck.reshape(b, cw, KV_W), cv.reshape(b, cw, KV_W), kn, vn,
      jnp.asarray(bias, F32), sinks.reshape(N_HEADS, 1))
    o4 = o.reshape(b, N_KV_HEADS, GROUP, N_KV_HEADS, HEAD_DIM)
    o_sel = jnp.stack([o4[:, g, :, g, :] for g in range(N_KV_HEADS)], axis=1)
    return o_sel.reshape(b, ATTN_W // LANES, LANES).transpose(1, 0, 2)


def _rnn_kernel(xr_ref, yr_ref, cw_ref, cb_ref, wa_ref, ba_ref, wx_ref, bx_ref, lam_ref,
                conv0_ref, h0_ref, y_ref, convo_ref, ho_ref, xp_s, a_s, b_s, h_s, *, nb, tt):
    rows = tt * nb
    halo = (CONV_W - 1) * nb

    @pl.when(pl.program_id(0) == 0)
    def _():
        xp_s[0:halo] = conv0_ref[...]
        h_s[...] = h0_ref[...]

    xp_s[halo:halo + rows] = xr_ref[...]
    nl = -lam_ref[...]
    coef = -LRU_C * (jnp.maximum(nl, 0.0) + jnp.log1p(jnp.exp(-jnp.abs(nl))))
    for n in range(N_RNN_BLOCKS):
        sl = slice(n * RNN_BLOCK, (n + 1) * RNN_BLOCK)
        xc = cb_ref[:, sl] + xp_s[0:rows, sl] * cw_ref[0:1, sl]
        for j in range(1, CONV_W):
            xc = xc + xp_s[j * nb:j * nb + rows, sl] * cw_ref[j:j + 1, sl]
        xcb = xc.astype(BF16)
        r = jax.nn.sigmoid(jnp.dot(xcb, wa_ref[n], preferred_element_type=F32) + ba_ref[:, sl])
        ig = jax.nn.sigmoid(jnp.dot(xcb, wx_ref[n], preferred_element_type=F32) + bx_ref[:, sl])
        log_a = coef[:, sl] * r
        a = jnp.exp(log_a)
        a_s[:, sl] = a
        b_s[:, sl] = jnp.sqrt(1.0 - a * a) * (ig * xc)

    def step(t, h):
        off = pl.multiple_of(t * nb, nb)
        h = a_s[pl.ds(off, nb), :] * h + b_s[pl.ds(off, nb), :]
        a_s[pl.ds(off, nb), :] = h
        return h

    h = lax.fori_loop(0, tt, step, h_s[...])
    h_s[...] = h
    yr = yr_ref[...]
    gelu = 0.5 * yr * (1.0 + jnp.tanh(np.float32(np.sqrt(2.0 / np.pi))
                                      * (yr + np.float32(0.044715) * (yr * yr * yr))))
    y_ref[...] = (a_s[...] * gelu).astype(y_ref.dtype)
    tail = xp_s[rows:rows + halo]
    convo_ref[...] = tail
    ho_ref[...] = h
    xp_s[0:halo] = tail


def _rnn_branch(xr, yr, p, conv0, h0, nb):
    rows_total = xr.shape[0]
    t_total = rows_total // nb
    tt = min(RNN_TIME_TILE, t_total)
    rows = tt * nb
    halo = (CONV_W - 1) * nb
    row_spec = pl.BlockSpec((rows, D_RNN), lambda i: (i, 0))
    vec = _const_spec((1, D_RNN))
    return pl.pallas_call(
        functools.partial(_rnn_kernel, nb=nb, tt=tt),
        grid=(t_total // tt,),
        in_specs=[row_spec, row_spec, _const_spec((CONV_W, D_RNN)), vec,
                  _const_spec((N_RNN_BLOCKS, RNN_BLOCK, RNN_BLOCK)), vec,
                  _const_spec((N_RNN_BLOCKS, RNN_BLOCK, RNN_BLOCK)), vec, vec,
                  _const_spec((halo, D_RNN)), _const_spec((nb, D_RNN))],
        out_specs=[row_spec, _const_spec((halo, D_RNN)), _const_spec((nb, D_RNN))],
        out_shape=[jax.ShapeDtypeStruct((rows_total, D_RNN), BF16),
                   jax.ShapeDtypeStruct((halo, D_RNN), F32),
                   jax.ShapeDtypeStruct((nb, D_RNN), F32)],
        scratch_shapes=[pltpu.VMEM((rows + halo, D_RNN), F32),
                        pltpu.VMEM((rows, D_RNN), F32),
                        pltpu.VMEM((rows, D_RNN), F32),
                        pltpu.VMEM((nb, D_RNN), F32)],
        compiler_params=_params(("arbitrary",), 48),
        name="rnn_branch",
    )(xr, yr, p['conv_w'], p['conv_b'], p['rg_wa'], p['rg_ba'], p['rg_wx'], p['rg_bx'],
      p['rg_lambda'], conv0, h0)


def _load_token_tiles(ref, n, lead=()):
    return [ref[lead + (pl.ds(c, n, stride=LANE_GROUPS), slice(None))] for c in range(LANE_GROUPS)]


def _store_token_tiles(ref, x, n):
    for c in range(LANE_GROUPS):
        ref[pl.ds(c, n, stride=LANE_GROUPS), :] = x[:, c * LANES:(c + 1) * LANES]


def _layer_norm(z, g, b):
    mu = jnp.mean(z, axis=-1, keepdims=True)
    zc = z - mu
    var = jnp.mean(zc * zc, axis=-1, keepdims=True)
    return zc * lax.rsqrt(var + LN_EPS) * g + b


def _mix_kernel(o_ref, y_ref, ga_ref, gb_ref, x_ref, wao_ref, wro_ref, wo_ref, g1_ref, b1_ref,
                wrt_ref, br_ref, cnt0_ref, x1_ref, tope_ref, gate_ref, rank_ref, cnt_ref, runs_ref,
                run_s, *, alpha, tm):
    @pl.when(pl.program_id(0) == 0)
    def _():
        run_s[...] = cnt0_ref[...]

    o_attn = jnp.concatenate([o_ref[c].astype(BF16) for c in range(ATTN_W // LANES)], axis=1)
    att = jnp.dot(o_attn, wao_ref[...], preferred_element_type=F32)
    rn = jnp.dot(y_ref[...], wro_ref[...], preferred_element_type=F32)
    merged = jax.nn.sigmoid(ga_ref[...]) * att + jax.nn.sigmoid(gb_ref[...]) * rn
    mixed = jnp.dot(merged.astype(BF16), wo_ref[...], preferred_element_type=F32)
    x1 = _layer_norm(alpha * x_ref[...] + mixed, g1_ref[...], b1_ref[...])
    _store_token_tiles(x1_ref, x1, tm)

    logits = lax.dot_general(wrt_ref[...], x1.astype(BF16), (((1,), (1,)), ((), ())),
                             preferred_element_type=F32) + br_ref[...]
    eidx = lax.broadcasted_iota(jnp.int32, (N_EXPERTS, tm), 0)
    vals, idxs, hots = [], [], []
    for _ in range(TOP_K):
        m = jnp.max(logits, axis=0, keepdims=True)
        idx = jnp.min(jnp.where(logits == m, eidx, N_EXPERTS), axis=0, keepdims=True)
        hot = eidx == idx
        logits = jnp.where(hot, -jnp.inf, logits)
        vals.append(m)
        idxs.append(idx)
        hots.append(hot)
    exps = [jnp.exp(v - vals[0]) for v in vals]
    total = exps[0] + exps[1] + exps[2] + exps[3]
    gate_ref[...] = jnp.concatenate([e / total for e in exps], axis=0)
    tope_ref[...] = jnp.concatenate(idxs, axis=0)

    assigned = sum(h.astype(F32) for h in hots)
    earlier = (lax.broadcasted_iota(jnp.int32, (tm, tm), 0)
               < lax.broadcasted_iota(jnp.int32, (tm, tm), 1)).astype(BF16)
    before = jnp.dot(assigned.astype(BF16), earlier, preferred_element_type=F32) + run_s[...]
    ranks = [jnp.sum(jnp.where(h, before, 0.0), axis=0, keepdims=True) for h in hots]
    rank_ref[...] = jnp.concatenate(ranks, axis=0).astype(jnp.int32)
    runs_ref[0] = run_s[...]
    run_s[...] = run_s[...] + jnp.sum(assigned, axis=1, keepdims=True)
    cnt_ref[...] = run_s[...]


def _mix(o_attn, y_rnn, ga, gb, x, p, cnt0, alpha):
    n = x.shape[0]
    tm = min(ROW_TILE, n)
    row = lambda w: pl.BlockSpec((tm, w), lambda i: (i, 0))
    tok = pl.BlockSpec((TOP_K, tm), lambda i: (0, i))
    vec = _const_spec((1, D_MODEL))
    cnt = _const_spec((N_EXPERTS, 1))
    return pl.pallas_call(
        functools.partial(_mix_kernel, alpha=alpha, tm=tm),
        grid=(n // tm,),
        in_specs=[pl.BlockSpec((ATTN_W // LANES, tm, LANES), lambda i: (0, i, 0)),
                  row(D_RNN), row(D_MODEL), row(D_MODEL), row(D_MODEL),
                  _const_spec((ATTN_W, D_MODEL)), _const_spec((D_RNN, D_MODEL)),
                  _const_spec((D_MODEL, D_MODEL)), vec, vec,
                  _const_spec((N_EXPERTS, D_MODEL)), cnt, cnt],
        out_specs=[pl.BlockSpec((tm * LANE_GROUPS, LANES), lambda i: (i, 0)), tok, tok, tok, cnt,
                   pl.BlockSpec((1, N_EXPERTS, 1), lambda i: (i, 0, 0))],
        out_shape=[jax.ShapeDtypeStruct((n * LANE_GROUPS, LANES), F32),
                   jax.ShapeDtypeStruct((TOP_K, n), jnp.int32),
                   jax.ShapeDtypeStruct((TOP_K, n), F32),
                   jax.ShapeDtypeStruct((TOP_K, n), jnp.int32),
                   jax.ShapeDtypeStruct((N_EXPERTS, 1), F32),
                   jax.ShapeDtypeStruct((n // tm, N_EXPERTS, 1), F32)],
        scratch_shapes=[pltpu.VMEM((N_EXPERTS, 1), F32)],
        compiler_params=_params(("arbitrary",), 48),
        name="mix_norm_router",
    )(o_attn, y_rnn, ga, gb, x, p['w_attn_out'], p['w_rnn_out'], p['w_out'], p['ln1_g'],
      p['ln1_b'], p['w_router_t'], p['b_router'], cnt0)


def _tile_copy(src, src_tok, dst, dst_tok, sem):
    rows = lambda t: pl.ds(t * LANE_GROUPS if isinstance(t, int)
                           else pl.multiple_of(t * LANE_GROUPS, LANE_GROUPS), LANE_GROUPS)
    return pltpu.make_async_copy(src.at[rows(src_tok)], dst.at[rows(dst_tok)], sem)


_start_copy = lambda cp, k: cp.start(priority=k % 2)
_wait_copy = lambda cp, k: cp.wait()


def _dispatch_runs_kernel(src_ref, nch_ref, loc_ref, tot_ref, destb_ref, zstart_ref, zlen_ref,
                          xa_ref, cola_ref, xb_ref, xs_ref, stage_a, sems_a, stage_b, sem_b,
                          zeros, zsem, *, nb, tma, tmb, sa, steps_a, nseg):
    i = pl.program_id(0)
    slot = i % 2
    piece = RUN_CHUNK * LANE_GROUPS

    def runs(step, s, act):
        def one_expert(e, prio):
            idx = step * N_EXPERTS + e
            src0, loc0 = src_ref[idx], loc_ref[idx]

            def body(j, c):
                d = pl.multiple_of((src0 + j * RUN_CHUNK) * LANE_GROUPS, LANE_GROUPS)
                l = pl.multiple_of((loc0 + j * RUN_CHUNK) * LANE_GROUPS, piece)
                act(pltpu.make_async_copy(stage_a.at[s, pl.ds(l, piece)],
                                          xs_ref.at[pl.ds(d, piece)], sems_a.at[s]), prio)
                return c
            lax.fori_loop(0, nch_ref[idx], body, 0)

        def pair(m, c):
            one_expert(2 * m, 0)
            one_expert(2 * m + 1, 1)
            return c
        lax.fori_loop(0, N_EXPERTS // 2, pair, 0)

    def wait_runs(step, s):
        def body(j, c):
            pltpu.make_async_copy(stage_a.at[s, pl.ds(0, piece)], xs_ref.at[pl.ds(0, piece)],
                                  sems_a.at[s]).wait()
            return c
        lax.fori_loop(0, tot_ref[step], body, 0)

    def tokens_b(step, act):
        base = step * tmb

        def body(r, c):
            for k in range(TOP_K):
                act(_tile_copy(stage_b, r, xs_ref, destb_ref[k * nb + base + r], sem_b), k)
            return c
        lax.fori_loop(0, tmb, body, 0)

    def zero_fill(act):
        def segment(g, c):
            start, length = zstart_ref[g], zlen_ref[g]
            whole = length // RUN_CHUNK

            def pieces(j, c2):
                d = pl.multiple_of((start + j * RUN_CHUNK) * LANE_GROUPS, LANE_GROUPS)
                act(pltpu.make_async_copy(zeros, xs_ref.at[pl.ds(d, piece)], zsem), 0)
                return c2

            def singles(j, c2):
                d = pl.multiple_of((start + whole * RUN_CHUNK + j) * LANE_GROUPS, LANE_GROUPS)
                act(pltpu.make_async_copy(zeros.at[pl.ds(0, LANE_GROUPS)],
                                          xs_ref.at[pl.ds(d, LANE_GROUPS)], zsem), 1)
                return c2
            lax.fori_loop(0, whole, pieces, 0)
            lax.fori_loop(0, length - whole * RUN_CHUNK, singles, 0)
            return c
        lax.fori_loop(0, nseg, segment, 0)

    @pl.when(i == 0)
    def _():
        zeros[...] = jnp.zeros_like(zeros)
        zero_fill(_start_copy)

    @pl.when(i < steps_a)
    def _():
        xa = jnp.concatenate([c.astype(BF16) for c in _load_token_tiles(xa_ref, tma)], axis=1)
        col = cola_ref[...]
        pos = lax.broadcasted_iota(jnp.int32, (sa, tma), 0)
        onehot = jnp.where(col[0:1, :] == pos, 1.0, 0.0)
        for k in range(1, TOP_K):
            onehot = onehot + jnp.where(col[k:k + 1, :] == pos, 1.0, 0.0)
        staged = jnp.dot(onehot.astype(BF16), xa, preferred_element_type=F32)
        for c in range(LANE_GROUPS):
            stage_a[slot, pl.ds(c, sa, stride=LANE_GROUPS), :] = staged[:, c * LANES:(c + 1) * LANES]

        @pl.when(i == 0)
        def _():
            zero_fill(_wait_copy)

        @pl.when(i > 0)
        def _():
            wait_runs(i - 1, 1 - slot)
        runs(i, slot, _start_copy)

    @pl.when(i >= steps_a)
    def _():
        @pl.when(i == steps_a)
        def _():
            wait_runs(steps_a - 1, (steps_a - 1) % 2)
        stage_b[...] = xb_ref[...]
        tokens_b(i - steps_a, _start_copy)
        tokens_b(i - steps_a, _wait_copy)


def _dispatch_runs(xa_tiles, plan_a, col_a, tma, xb_tiles, dest_b, rows, zero_plan):
    na, nb = xa_tiles.shape[0] // LANE_GROUPS, xb_tiles.shape[0] // LANE_GROUPS
    tmb = min(GATHER_TILE, nb)
    steps_a, steps_b = na // tma, nb // tmb
    sa = _staging_rows(tma)
    zstart, zlen = zero_plan
    return pl.pallas_call(
        functools.partial(_dispatch_runs_kernel, nb=nb, tma=tma, tmb=tmb, sa=sa, steps_a=steps_a,
                          nseg=zstart.shape[0]),
        grid_spec=pltpu.PrefetchScalarGridSpec(
            num_scalar_prefetch=7,
            grid=(steps_a + steps_b,),
            in_specs=[pl.BlockSpec((tma * LANE_GROUPS, LANES),
                                   lambda i, *_: (jnp.minimum(i, steps_a - 1), 0)),
                      pl.BlockSpec((TOP_K, tma), lambda i, *_: (0, jnp.minimum(i, steps_a - 1))),
                      pl.BlockSpec((tmb * LANE_GROUPS, LANES),
                                   lambda i, *_: (jnp.maximum(i - steps_a, 0), 0))],
            out_specs=pl.BlockSpec(memory_space=pl.ANY),
            scratch_shapes=[pltpu.VMEM((2, sa * LANE_GROUPS, LANES), F32),
                            pltpu.SemaphoreType.DMA((2,)),
                            pltpu.VMEM((tmb * LANE_GROUPS, LANES), F32),
                            pltpu.SemaphoreType.DMA(()),
                            pltpu.VMEM((RUN_CHUNK * LANE_GROUPS, LANES), F32),
                            pltpu.SemaphoreType.DMA(())]),
        out_shape=jax.ShapeDtypeStruct((rows * LANE_GROUPS, LANES), F32),
        compiler_params=_params(("arbitrary",), 48),
        name="moe_dispatch",
    )(*plan_a, dest_b, zstart, zlen, xa_tiles, col_a, xb_tiles)


def _staging_rows(tm):
    return -(-(TOP_K * tm + N_EXPERTS * (RUN_CHUNK - 1)) // LANES) * LANES


def _expert_kernel(be_ref, next_ref, nused_ref, x_ref, wu_hbm, bu_ref, wd_hbm, bd_ref, y_ref,
                   wu_f, wd_f, wu_s, wd_s, slot_s, sems, *, layer):
    i = pl.program_id(0)

    def weight_copies(e, slot):
        return (pltpu.make_async_copy(wu_hbm.at[layer, e], wu_f.at[slot], sems.at[slot, 0]),
                pltpu.make_async_copy(wd_hbm.at[layer, e], wd_f.at[slot], sems.at[slot, 1]))

    @pl.when(i < nused_ref[0])
    def _():
        e = be_ref[i]

        @pl.when(i == 0)
        def _():
            slot_s[0] = 0
            for cp in weight_copies(e, 0):
                cp.start()

        @pl.when(jnp.logical_or(i == 0, e != be_ref[jnp.maximum(i - 1, 0)]))
        def _():
            @pl.when(i > 0)
            def _():
                slot_s[0] = 1 - slot_s[0]
            slot = slot_s[0]
            for cp in weight_copies(e, slot):
                cp.wait()
            nxt = next_ref[i]

            @pl.when(nxt >= 0)
            def _():
                for cp in weight_copies(nxt, 1 - slot):
                    cp.start()
            wu_s[...] = wu_f[slot].astype(BF16)
            wd_s[...] = wd_f[slot].astype(BF16)

        xb = jnp.concatenate([c.astype(BF16) for c in _load_token_tiles(x_ref, MOE_BLOCK)], axis=1)
        y = bd_ref[0, 0]
        for c in range(0, D_FF, FF_CHUNK):
            glu = (jnp.dot(xb, wu_s[:, c:c + FF_CHUNK], preferred_element_type=F32)
                   + bu_ref[0, 0][:, c:c + FF_CHUNK])
            lin = (jnp.dot(xb, wu_s[:, D_FF + c:D_FF + c + FF_CHUNK], preferred_element_type=F32)
                   + bu_ref[0, 0][:, D_FF + c:D_FF + c + FF_CHUNK])
            glu = jnp.minimum(glu, SWIGLU_LIMIT)
            lin = jnp.clip(lin, -SWIGLU_LIMIT, SWIGLU_LIMIT)
            act = glu * jax.nn.sigmoid(SWIGLU_ALPHA * glu) * (lin + 1.0)
            y = y + jnp.dot(act.astype(BF16), wd_s[c:c + FF_CHUNK, :], preferred_element_type=F32)
        _store_token_tiles(y_ref, y, MOE_BLOCK)

    @pl.when(i >= nused_ref[0])
    def _():
        y_ref[...] = jnp.zeros_like(y_ref)


def _experts(xs, block_e, next_e, nused, layer, w_up, b_up, w_down, b_down):
    rows = xs.shape[0] // LANE_GROUPS
    nblk = rows // MOE_BLOCK
    depth = w_up.shape[0]
    tile_blk = (MOE_BLOCK * LANE_GROUPS, LANES)
    blk = lambda i, be, nx, nu: (jnp.minimum(i, nu[0] - 1), 0)
    out_blk = lambda i, be, nx, nu: (i, 0)
    exp3 = lambda i, be, nx, nu: (layer, be[i], 0, 0)
    return pl.pallas_call(
        functools.partial(_expert_kernel, layer=layer),
        grid_spec=pltpu.PrefetchScalarGridSpec(
            num_scalar_prefetch=3,
            grid=(nblk,),
            in_specs=[pl.BlockSpec(tile_blk, blk),
                      pl.BlockSpec(memory_space=pl.ANY),
                      pl.BlockSpec((1, 1, 1, 2 * D_FF), exp3),
                      pl.BlockSpec(memory_space=pl.ANY),
                      pl.BlockSpec((1, 1, 1, D_MODEL), exp3)],
            out_specs=pl.BlockSpec(tile_blk, out_blk),
            scratch_shapes=[pltpu.VMEM((2, D_MODEL, 2 * D_FF), F32),
                            pltpu.VMEM((2, D_FF, D_MODEL), F32),
                            pltpu.VMEM((D_MODEL, 2 * D_FF), BF16),
                            pltpu.VMEM((D_FF, D_MODEL), BF16),
                            pltpu.SMEM((1,), jnp.int32),
                            pltpu.SemaphoreType.DMA((2, 2))]),
        out_shape=jax.ShapeDtypeStruct(xs.shape, F32),
        compiler_params=_params(("arbitrary",), 56),
        name="moe_experts",
    )(block_e, next_e, nused, xs, w_up, b_up.reshape(depth, N_EXPERTS, 1, 2 * D_FF), w_down,
      b_down.reshape(depth, N_EXPERTS, 1, D_MODEL))


def _combine_kernel(src_ref, nch_ref, loc_ref, tot_ref, yb_ref, x1_ref, gate_ref, col_ref, g2_ref,
                    b2_ref, x2_ref, buf, sems, *, tm, stage, alpha):
    i = pl.program_id(0)
    slot = i % 2
    piece = RUN_CHUNK * LANE_GROUPS

    def runs(step, buf_slot, act):
        def one_expert(e, prio):
            idx = step * N_EXPERTS + e
            src0, loc0 = src_ref[idx], loc_ref[idx]

            def body(j, c):
                s = pl.multiple_of((src0 + j * RUN_CHUNK) * LANE_GROUPS, LANE_GROUPS)
                d = pl.multiple_of((loc0 + j * RUN_CHUNK) * LANE_GROUPS, piece)
                act(pltpu.make_async_copy(yb_ref.at[pl.ds(s, piece)],
                                          buf.at[buf_slot, pl.ds(d, piece)], sems.at[buf_slot]), prio)
                return c
            lax.fori_loop(0, nch_ref[idx], body, 0)

        def pair(m, c):
            one_expert(2 * m, 0)
            one_expert(2 * m + 1, 1)
            return c
        lax.fori_loop(0, N_EXPERTS // 2, pair, 0)

    @pl.when(i == 0)
    def _():
        buf[...] = jnp.zeros_like(buf)
        runs(0, 0, _start_copy)

    @pl.when(i + 1 < pl.num_programs(0))
    def _():
        runs(i + 1, 1 - slot, _start_copy)

    def wait_piece(j, c):
        pltpu.make_async_copy(yb_ref.at[pl.ds(0, piece)], buf.at[slot, pl.ds(0, piece)],
                              sems.at[slot]).wait()
        return c
    lax.fori_loop(0, tot_ref[i], wait_piece, 0)
    staged = jnp.concatenate([c.astype(BF16) for c in _load_token_tiles(buf, stage, lead=(slot,))],
                             axis=1)
    gate, col = gate_ref[...], col_ref[...]
    pos = lax.broadcasted_iota(jnp.int32, (tm, stage), 1)
    weights = jnp.where(col[:, 0:1] == pos, gate[:, 0:1], 0.0)
    for k in range(1, TOP_K):
        weights = weights + jnp.where(col[:, k:k + 1] == pos, gate[:, k:k + 1], 0.0)
    y = jnp.dot(weights.astype(BF16), staged, preferred_element_type=F32)
    x1 = jnp.concatenate(_load_token_tiles(x1_ref, tm), axis=1)
    x2_ref[...] = _layer_norm(alpha * x1 + y, g2_ref[...], b2_ref[...])


def _combine(yb, x1_tiles, runs_plan, gate_rows, col_rows, tm, ln_g, ln_b, alpha):
    n = x1_tiles.shape[0] // LANE_GROUPS
    stage = _staging_rows(tm)
    vec = pl.BlockSpec((1, D_MODEL), lambda i, *_: (0, 0))
    tok = pl.BlockSpec((tm, TOP_K), lambda i, *_: (i, 0))
    return pl.pallas_call(
        functools.partial(_combine_kernel, tm=tm, stage=stage, alpha=alpha),
        grid_spec=pltpu.PrefetchScalarGridSpec(
            num_scalar_prefetch=4,
            grid=(n // tm,),
            in_specs=[pl.BlockSpec(memory_space=pl.ANY),
                      pl.BlockSpec((tm * LANE_GROUPS, LANES), lambda i, *_: (i, 0)),
                      tok, tok, vec, vec],
            out_specs=pl.BlockSpec((tm, D_MODEL), lambda i, *_: (i, 0)),
            scratch_shapes=[pltpu.VMEM((2, stage * LANE_GROUPS, LANES), F32),
                            pltpu.SemaphoreType.DMA((2,))]),
        out_shape=jax.ShapeDtypeStruct((n, D_MODEL), F32),
        compiler_params=_params(("arbitrary",), 48),
        name="moe_combine",
    )(*runs_plan, yb, x1_tiles, gate_rows, col_rows, ln_g, ln_b)


def _combine_plan(top_e, rank, runs, counts_after, pad_start, tm):
    nt = runs.shape[0]
    tile_cnt = jnp.concatenate([runs[1:], counts_after[None]], axis=0) - runs
    nch = (tile_cnt + RUN_CHUNK - 1) // RUN_CHUNK
    room = nch * RUN_CHUNK
    loc = jnp.cumsum(room, axis=1) - room
    src = pad_start[None, :] + runs
    eids = jnp.arange(N_EXPERTS, dtype=jnp.int32)
    base = (loc - runs)[None, :, None, :]
    hit = top_e.reshape(TOP_K, nt, tm)[..., None] == eids
    col = jnp.sum(jnp.where(hit, base, 0), axis=-1).reshape(TOP_K, nt * tm) + rank
    flat = lambda a: a.reshape(-1).astype(jnp.int32)
    col = col.astype(jnp.int32)
    return (flat(src), flat(nch), flat(loc), flat(jnp.sum(nch, axis=1))), col, col.T


def _moe_layout(counts, groups):
    n_assign = sum(g[0].shape[1] for g in groups) * TOP_K
    spare = RUN_CHUNK - 1
    nblk = -(-(n_assign + N_EXPERTS * (MOE_BLOCK - 1 + spare)) // MOE_BLOCK) + 1
    padded = jnp.where(counts > 0, (counts + spare + MOE_BLOCK - 1) // MOE_BLOCK * MOE_BLOCK, 0)
    pad_end = jnp.cumsum(padded)
    pad_start = pad_end - padded
    nused = pad_end[-1] // MOE_BLOCK

    def expert_of(row):
        ended = (pad_end[None, :] <= row[:, None]).astype(jnp.int32)
        return jnp.minimum(jnp.sum(ended, axis=1), N_EXPERTS - 1)

    blk = jnp.arange(nblk, dtype=jnp.int32)
    block_e = expert_of(jnp.minimum(blk, nused - 1) * MOE_BLOCK)
    eids = jnp.arange(N_EXPERTS, dtype=jnp.int32)
    lookup = lambda table, idx: jnp.sum(jnp.where(idx[..., None] == eids, table, 0), axis=-1)
    after = lookup(pad_end, block_e)
    next_e = jnp.where(after < pad_end[-1], expert_of(after), -1).astype(jnp.int32)
    dests = [(lookup(pad_start, top_e) + rank).reshape(-1).astype(jnp.int32)
             for top_e, rank in groups]
    zero_plan = (jnp.concatenate([pad_start + counts, pad_end[-1:]]).astype(jnp.int32),
                 jnp.concatenate([padded - counts, nblk * MOE_BLOCK - pad_end[-1:]]).astype(jnp.int32))
    return (nblk, block_e.astype(jnp.int32), next_e, nused.astype(jnp.int32).reshape(1), dests,
            pad_start, zero_plan)


def _temporal(x, p, attend, conv0, h0, nb, cnt0, alpha):
    q, kv, xr, yr, ga, gb = _project(x, p['w_in'], p['layer'])
    o_attn = attend(q, kv)
    y_rnn, conv_new, h_new = _rnn_branch(xr, yr, p, conv0, h0, nb)
    x1, top_e, gate, rank, cnt, runs = _mix(o_attn, y_rnn, ga, gb, x, p, cnt0, alpha)
    route = (top_e, gate, rank, runs.reshape(-1, N_EXPERTS).astype(jnp.int32))
    return x1, route, cnt, kv, conv_new, h_new


def kernel(x_prompt, x_sample, cache_k, cache_v, state_conv, state_h, w_in, attn_sinks, conv_w,
           conv_b, rg_wa, rg_ba, rg_wx, rg_bx, rg_lambda, w_attn_out, w_rnn_out, w_out, ln1_g,
           ln1_b, w_router, b_router, w_up, b_up, w_down, b_down, ln2_g, ln2_b):
    depth = w_in.shape[0]
    alpha = float((2 * depth) ** 0.25)
    bp, seq, _ = x_prompt.shape
    bs = x_sample.shape[0]
    cw = cache_k.shape[2]
    halo = CONV_W - 1

    xp = x_prompt.transpose(1, 0, 2).reshape(seq * bp, D_MODEL)
    xs = x_sample.reshape(bs, D_MODEL)
    outs = [[] for _ in range(8)]
    zeros_conv = jnp.zeros((halo * bp, D_RNN), F32)
    zeros_h = jnp.zeros((bp, D_RNN), F32)
    zeros_cnt = jnp.zeros((N_EXPERTS, 1), F32)
    w_in_bf = w_in.astype(BF16)

    for l in range(depth):
        p = {
            'w_in': w_in_bf, 'layer': l,
            'conv_w': conv_w[l], 'conv_b': conv_b[l].reshape(1, D_RNN),
            'rg_wa': rg_wa[l].astype(BF16), 'rg_ba': rg_ba[l].reshape(1, D_RNN),
            'rg_wx': rg_wx[l].astype(BF16), 'rg_bx': rg_bx[l].reshape(1, D_RNN),
            'rg_lambda': rg_lambda[l].reshape(1, D_RNN),
            'w_attn_out': w_attn_out[l].astype(BF16), 'w_rnn_out': w_rnn_out[l].astype(BF16),
            'w_out': w_out[l].astype(BF16),
            'ln1_g': ln1_g[l].reshape(1, D_MODEL), 'ln1_b': ln1_b[l].reshape(1, D_MODEL),
            'w_router_t': w_router[l].T.astype(BF16),
            'b_router': b_router[l].reshape(N_EXPERTS, 1),
        }
        sinks = attn_sinks[l]
        g2, b2 = ln2_g[l].reshape(1, D_MODEL), ln2_b[l].reshape(1, D_MODEL)

        x1p, route_p, cnt_p, kv_p, conv_p, h_p = _temporal(
            xp, p, lambda q, kv: _prompt_attention(q, kv, sinks, bp, seq),
            zeros_conv, zeros_h, bp, zeros_cnt, alpha)
        conv0_s = state_conv[l].transpose(1, 0, 2).reshape(halo * bs, D_RNN)
        x1s, route_s, cnt_s, kv_s, conv_s, h_s = _temporal(
            xs, p, lambda q, kv: _sample_attention(q, kv, sinks, cache_k[l], cache_v[l]),
            conv0_s, state_h[l], bs, cnt_p, alpha)

        (te_p, gt_p, rk_p, runs_p), (te_s, gt_s, rk_s, runs_s) = route_p, route_s
        counts_p = cnt_p.reshape(N_EXPERTS).astype(jnp.int32)
        counts = cnt_s.reshape(N_EXPERTS).astype(jnp.int32)
        nblk, block_e, next_e, nused, (dest_p, dest_s), pad_start, zero_plan = _moe_layout(
            counts, [(te_p, rk_p), (te_s, rk_s)])
        tm_p, tm_s = min(ROW_TILE, seq * bp), min(ROW_TILE, bs)
        plan_p, col_p, colrows_p = _combine_plan(te_p, rk_p, runs_p, counts_p, pad_start, tm_p)
        plan_s, _, colrows_s = _combine_plan(te_s, rk_s, runs_s, counts, pad_start, tm_s)
        sorted_x = _dispatch_runs(x1p, plan_p, col_p, tm_p, x1s, dest_s, nblk * MOE_BLOCK, zero_plan)
        yb = _experts(sorted_x, block_e, next_e, nused, l, w_up, b_up, w_down, b_down)
        xp = _combine(yb, x1p, plan_p, gt_p.T, colrows_p, tm_p, g2, b2, alpha)
        xs = _combine(yb, x1s, plan_s, gt_s.T, colrows_s, tm_s, g2, b2, alpha)

        kv_p4 = kv_p[:, (seq - cw) * bp:].reshape(2, cw, bp, N_KV_HEADS, HEAD_DIM)
        outs[0].append(kv_p4[0].transpose(1, 0, 2, 3))
        outs[1].append(kv_p4[1].transpose(1, 0, 2, 3))
        outs[2].append(conv_p.reshape(halo, bp, D_RNN).transpose(1, 0, 2))
        outs[3].append(h_p)
        kv_s4 = kv_s.reshape(2, bs, 1, N_KV_HEADS, HEAD_DIM)
        outs[4].append(jnp.concatenate([cache_k[l], kv_s4[0]], axis=1)[:, -cw:])
        outs[5].append(jnp.concatenate([cache_v[l], kv_s4[1]], axis=1)[:, -cw:])
        outs[6].append(conv_s.reshape(halo, bs, D_RNN).transpose(1, 0, 2))
        outs[7].append(h_s)

    y_prompt = xp.reshape(seq, bp, D_MODEL).transpose(1, 0, 2)
    y_sample = xs.reshape(bs, 1, D_MODEL)
    return (y_prompt, y_sample) + tuple(jnp.stack(o) for o in outs)
```
